```python
import math
import jax, jax.numpy as jnp
from jax import lax
import numpy as np


D_MODEL = 1024
BATCH = 8
SEQ = 8192
DEPTH = 4

MLSTM_HEADS = 4
MLSTM_QK_DIM = 64
MLSTM_V_DIM = 128
MLSTM_CONV = 4
MLSTM_CHUNK = 64
DSA_HEADS = 8
DSA_HEAD_DIM = 64
DSA_LATENT = 128
IDX_HEADS = 8
IDX_DIM = 32
IDX_TOPK_MAX = 256
QUERY_BLOCK = 128
REL_BUCKETS = 32
REL_MAX_DIST = 128
D_FF = 2816
N_EXPERTS = 8
TOP_K = 2
D_FF_EXPERT = 3584
MOE_BLOCK = 256
N_DENSE = (DEPTH + 1) // 2
N_MOE = DEPTH // 2
DN_ALPHA = (2 * DEPTH) ** 0.25
DN_BETA = (8 * DEPTH) ** -0.25
LN_EPS = 1e-5

A_QK = 2 * MLSTM_HEADS * MLSTM_QK_DIM
A_V = MLSTM_HEADS * MLSTM_V_DIM
A_GATE = 2 * MLSTM_HEADS
B_Q = DSA_HEADS * DSA_HEAD_DIM
I_Q = IDX_HEADS * IDX_DIM
PROJ_SIZES = (A_QK, A_V, A_V, A_GATE, B_Q, DSA_LATENT, I_Q, IDX_DIM, IDX_HEADS, D_MODEL, D_MODEL)
D_IN_PROJ = sum(PROJ_SIZES)

kernel_name = 'hybrid_mlstm_dsa_moe_deepnorm'


def layer_norm(x, g, b):
    xf = x.astype(jnp.float32)
    mu = jnp.mean(xf, -1, keepdims=True)
    var = jnp.mean(jnp.square(xf - mu), -1, keepdims=True)
    return ((xf - mu) * lax.rsqrt(var + LN_EPS)).astype(x.dtype) * g + b


def rms_norm(x, g):
    xf = x.astype(jnp.float32)
    return (xf * lax.rsqrt(jnp.mean(jnp.square(xf), -1, keepdims=True) + LN_EPS)).astype(x.dtype) * g


def causal_depthwise_conv(u, w):
    taps, s = w.shape[0], u.shape[1]
    up = jnp.pad(u, ((0, 0), (taps - 1, 0), (0, 0)))
    return sum(up[:, j:j + s] * w[j] for j in range(taps))


def mlstm_chunkwise(q, k, v, i_pre, log_f):
    bsz, s, h, dk = q.shape
    dv = v.shape[-1]
    ch = MLSTM_CHUNK
    nc = s // ch
    to_c = lambda a: a.reshape(bsz, nc, ch, h, -1).transpose(1, 0, 3, 2, 4)
    to_cg = lambda a: a.reshape(bsz, nc, ch, h).transpose(1, 0, 3, 2)
    causal = jnp.tril(jnp.ones((ch, ch), dtype=bool))

    def step(carry, inp):
        c_st, n_st, m_st = carry
        qc, kc, vc, ic, fc = inp
        b = jnp.cumsum(fc, axis=-1)
        d_log = jnp.where(causal, b[..., :, None] - b[..., None, :] + ic[..., None, :], -jnp.inf)
        inter = b + m_st[..., None]
        m_t = jnp.maximum(inter, jnp.max(d_log, -1))
        s_w = jnp.einsum('bhtd,bhsd->bhts', qc, kc) * jnp.exp(d_log - m_t[..., None])
        inter_w = jnp.exp(inter - m_t)
        num = jnp.einsum('bhts,bhsv->bhtv', s_w, vc) + inter_w[..., None] * jnp.einsum('bhtd,bhdv->bhtv', qc, c_st)
        den = jnp.sum(s_w, -1) + inter_w * jnp.einsum('bhtd,bhd->bht', qc, n_st)
        h_out = num / jnp.maximum(jnp.abs(den), jnp.exp(-m_t))[..., None]
        b_last = b[..., -1]
        g_log = b_last[..., None] - b + ic
        m_new = jnp.maximum(b_last + m_st, jnp.max(g_log, -1))
        decay = jnp.exp(b_last + m_st - m_new)
        w_k = kc * jnp.exp(g_log - m_new[..., None])[..., None]
        c_new = decay[..., None, None] * c_st + jnp.einsum('bhsd,bhsv->bhdv', w_k, vc)
        n_new = decay[..., None] * n_st + jnp.sum(w_k, -2)
        return (c_new, n_new, m_new), h_out

    init = (jnp.zeros((bsz, h, dk, dv), jnp.float32), jnp.zeros((bsz, h, dk), jnp.float32),
            jnp.zeros((bsz, h), jnp.float32))
    _, hs = lax.scan(step, init, (to_c(q), to_c(k), to_c(v), to_cg(i_pre), to_cg(log_f)))
    return hs.transpose(1, 0, 3, 2, 4).reshape(bsz, s, h, dv)


def t5_bucket(dist):
    dist = jnp.maximum(dist, 0)
    exact = REL_BUCKETS // 2
    log_ratio = jnp.log(jnp.maximum(dist, 1).astype(jnp.float32) / exact) / math.log(REL_MAX_DIST / exact)
    large = jnp.minimum(exact + (log_ratio * (REL_BUCKETS - exact)).astype(jnp.int32), REL_BUCKETS - 1)
    return jnp.where(dist < exact, dist, large)


def dsa_sparse_attention(q_lat, c_kv, q_idx, k_idx, w_idx, rel_bias):
    bsz, s, h, dc = q_lat.shape
    nb = s // QUERY_BLOCK
    topk = min(IDX_TOPK_MAX, s // 4)
    scale = DSA_HEAD_DIM ** -0.5
    spos = jnp.arange(s)
    blocks = lambda a: a.reshape((bsz, nb, QUERY_BLOCK) + a.shape[2:]).swapaxes(0, 1)

    def one_block(args):
        ql, qi, wi, t0 = args
        tpos = t0 + jnp.arange(QUERY_BLOCK)
        score = jnp.einsum('bqhs,bqh->bqs', jax.nn.relu(jnp.einsum('bqhd,bsd->bqhs', qi, k_idx)), wi)
        score = jnp.where(spos[None, None, :] <= tpos[None, :, None], score.astype(jnp.float32), -jnp.inf)
        _, idx = lax.top_k(score, topk)
        valid = idx <= tpos[None, :, None]
        c_sel = jax.vmap(lambda cb, ib: cb[ib])(c_kv, idx)
        logits = jnp.einsum('bqhc,bqkc->bqhk', ql, c_sel).astype(jnp.float32) * scale
        bias = rel_bias[t5_bucket(tpos[None, :, None] - idx)].transpose(0, 1, 3, 2)
        logits = jnp.where(valid[:, :, None, :], logits + bias.astype(jnp.float32), -jnp.inf)
        p = jax.nn.softmax(logits, axis=-1).astype(c_sel.dtype)
        return jnp.einsum('bqhk,bqkc->bqhc', p, c_sel)

    out = lax.map(one_block, (blocks(q_lat), blocks(q_idx), blocks(w_idx), jnp.arange(nb) * QUERY_BLOCK))
    return out.swapaxes(0, 1).reshape(bsz, s, h, dc)


def token_mixer(x, w_in, conv_w, gate_bias, norm_g, kv_norm_g, w_uk, w_uv, rel_bias,
                w_branch_a, w_branch_b, w_out):
    bsz, s, _ = x.shape
    offsets = np.cumsum(PROJ_SIZES)[:-1].tolist()
    a_qk, a_v, a_o, a_if, b_q, b_c, i_q, i_k, i_w, g_a, g_b = jnp.split(x @ w_in, offsets, axis=-1)
    qk = jax.nn.silu(causal_depthwise_conv(a_qk, conv_w)).astype(jnp.float32)
    q, k = jnp.split(qk.reshape(bsz, s, 2 * MLSTM_HEADS, MLSTM_QK_DIM), 2, axis=2)
    k = k * MLSTM_QK_DIM ** -0.5
    v = a_v.astype(jnp.float32).reshape(bsz, s, MLSTM_HEADS, MLSTM_V_DIM)
    gates = a_if.astype(jnp.float32).reshape(bsz, s, 2, MLSTM_HEADS) + gate_bias
    h = mlstm_chunkwise(q, k, v, gates[:, :, 0], jax.nn.log_sigmoid(gates[:, :, 1]))
    mu = jnp.mean(h, -1, keepdims=True)
    h = (h - mu) * lax.rsqrt(jnp.mean(jnp.square(h - mu), -1, keepdims=True) + LN_EPS)
    y_a = h.reshape(bsz, s, A_V).astype(x.dtype) * norm_g * jax.nn.sigmoid(a_o)
    q_lat = jnp.einsum('bshd,hdc->bshc', b_q.reshape(bsz, s, DSA_HEADS, DSA_HEAD_DIM), w_uk)
    c_kv = rms_norm(b_c, kv_norm_g)
    w_idx = i_w * (IDX_HEADS * IDX_DIM) ** -0.5
    o_lat = dsa_sparse_attention(q_lat, c_kv, i_q.reshape(bsz, s, IDX_HEADS, IDX_DIM), i_k, w_idx, rel_bias)
    y_b = jnp.einsum('bshc,hcd->bshd', o_lat, w_uv).reshape(bsz, s, B_Q)
    merged = jax.nn.sigmoid(g_a) * (y_a @ w_branch_a) + jax.nn.sigmoid(g_b) * (y_b @ w_branch_b)
    return merged @ w_out


def swiglu(x, wg, wu, wd):
    return (jax.nn.silu(x @ wg) * (x @ wu)) @ wd


def moe_swiglu(x2, router_w, wg, wu, wd):
    n, d = x2.shape
    logits = (x2 @ router_w).astype(jnp.float32)
    top_logit, top_e = lax.top_k(logits, TOP_K)
    gates = jax.nn.softmax(top_logit, axis=-1)
    flat_e = top_e.reshape(-1)
    flat_tok = jnp.repeat(jnp.arange(n, dtype=jnp.int32), TOP_K)
    flat_g = gates.reshape(-1)
    order = jnp.argsort(flat_e, stable=True)
    se, stok, sg = flat_e[order], flat_tok[order], flat_g[order]
    counts = jnp.bincount(flat_e, length=N_EXPERTS)
    starts = jnp.cumsum(counts) - counts
    padded = ((counts + MOE_BLOCK - 1) // MOE_BLOCK) * MOE_BLOCK
    pad_end = jnp.cumsum(padded)
    pad_start = pad_end - padded
    dest = pad_start[se] + (jnp.arange(n * TOP_K) - starts[se])
    total = n * TOP_K + N_EXPERTS * MOE_BLOCK
    nblk = total // MOE_BLOCK
    buf_tok = jnp.zeros((total,), jnp.int32).at[dest].set(stok)
    buf_g = jnp.zeros((total,), jnp.float32).at[dest].set(sg)
    blk_e = jnp.minimum(jnp.searchsorted(pad_end, jnp.arange(nblk) * MOE_BLOCK, side='right'), N_EXPERTS - 1)

    def run(args):
        tok, g, e = args
        xb = x2[tok]
        hb = jax.nn.silu(xb @ wg[e]) * (xb @ wu[e])
        return (hb @ wd[e]) * g[:, None].astype(x2.dtype)

    y = lax.map(run, (buf_tok.reshape(nblk, MOE_BLOCK), buf_g.reshape(nblk, MOE_BLOCK), blk_e))
    return jnp.zeros_like(x2).at[buf_tok].add(y.reshape(total, d))


def setup_inputs(seed: int = 0) -> dict:
    key = jax.random.key(seed)
    ks = jax.random.split(key, 24)
    nrm = lambda k, shape, sc: jax.random.normal(k, shape, jnp.float32) * sc
    D = D_MODEL
    gate_bias = jnp.stack([nrm(ks[3], (DEPTH, MLSTM_HEADS), 0.1),
                           3.0 + nrm(ks[4], (DEPTH, MLSTM_HEADS), 0.5)], axis=1)
    return {
        'x': nrm(ks[0], (BATCH, SEQ, D), 1.0),
        'w_in': nrm(ks[1], (DEPTH, D, D_IN_PROJ), D ** -0.5),
        'mlstm_conv_w': nrm(ks[2], (DEPTH, MLSTM_CONV, A_QK), MLSTM_CONV ** -0.5),
        'mlstm_gate_bias': gate_bias,
        'mlstm_norm_g': 1.0 + nrm(ks[5], (DEPTH, A_V), 0.02),
        'dsa_kv_norm_g': 1.0 + nrm(ks[6], (DEPTH, DSA_LATENT), 0.02),
        'dsa_w_uk': nrm(ks[7], (DEPTH, DSA_HEADS, DSA_HEAD_DIM, DSA_LATENT), DSA_HEAD_DIM ** -0.5),
        'dsa_w_uv': nrm(ks[8], (DEPTH, DSA_HEADS, DSA_LATENT, DSA_HEAD_DIM), DSA_LATENT ** -0.5),
        'rel_bias': nrm(ks[9], (REL_BUCKETS, DSA_HEADS), 0.5),
        'w_branch_a': nrm(ks[10], (DEPTH, A_V, D), A_V ** -0.5 * DN_BETA),
        'w_branch_b': nrm(ks[11], (DEPTH, B_Q, D), B_Q ** -0.5 * DN_BETA),
        'w_out': nrm(ks[12], (DEPTH, D, D), D ** -0.5 * DN_BETA),
        'ln_g': 1.0 + nrm(ks[13], (DEPTH, 2, D), 0.02),
        'ln_b': nrm(ks[14], (DEPTH, 2, D), 0.02),
        'dense_w_gate': nrm(ks[15], (N_DENSE, D, D_FF), D ** -0.5),
        'dense_w_up': nrm(ks[16], (N_DENSE, D, D_FF), D ** -0.5),
        'dense_w_down': nrm(ks[17], (N_DENSE, D_FF, D), D_FF ** -0.5 * DN_BETA),
        'router_w': nrm(ks[18], (N_MOE, D, N_EXPERTS), D ** -0.5),
        'expert_w_gate': nrm(ks[19], (N_MOE, N_EXPERTS, D, D_FF_EXPERT), D ** -0.5),
        'expert_w_up': nrm(ks[20], (N_MOE, N_EXPERTS, D, D_FF_EXPERT), D ** -0.5),
        'expert_w_down': nrm(ks[21], (N_MOE, N_EXPERTS, D_FF_EXPERT, D), D_FF_EXPERT ** -0.5 * DN_BETA),
    }


def reference(x, w_in, mlstm_conv_w, mlstm_gate_bias, mlstm_norm_g, dsa_kv_norm_g, dsa_w_uk, dsa_w_uv,
              rel_bias, w_branch_a, w_branch_b, w_out, ln_g, ln_b, dense_w_gate, dense_w_up, dense_w_down,
              router_w, expert_w_gate, expert_w_up, expert_w_down):
    for l in range(DEPTH):
        mix = token_mixer(x, w_in[l], mlstm_conv_w[l], mlstm_gate_bias[l], mlstm_norm_g[l], dsa_kv_norm_g[l],
                          dsa_w_uk[l], dsa_w_uv[l], rel_bias, w_branch_a[l], w_branch_b[l], w_out[l])
        x = layer_norm(DN_ALPHA * x + mix, ln_g[l, 0], ln_b[l, 0])
        if l % 2 == 0:
            f = swiglu(x, dense_w_gate[l // 2], dense_w_up[l // 2], dense_w_down[l // 2])
        else:
            j = l // 2
            f = moe_swiglu(x.reshape(-1, x.shape[-1]), router_w[j], expert_w_gate[j], expert_w_up[j],
                           expert_w_down[j]).reshape(x.shape)
        x = layer_norm(DN_ALPHA * x + f, ln_g[l, 1], ln_b[l, 1])
    return x
```

```python
import functools
import math

import numpy as np
import jax
import jax.numpy as jnp
from jax import lax
from jax.experimental import pallas as pl
from jax.experimental.pallas import tpu as pltpu

D_MODEL = 1024
DEPTH = 4
MLSTM_HEADS = 4
MLSTM_QK_DIM = 64
MLSTM_V_DIM = 128
MLSTM_CONV = 4
DSA_HEADS = 8
DSA_HEAD_DIM = 64
DSA_LATENT = 128
IDX_HEADS = 8
IDX_DIM = 32
IDX_TOPK_MAX = 256
REL_BUCKETS = 32
REL_MAX_DIST = 128
D_FF = 2816
N_EXPERTS = 8
TOP_K = 2
D_FF_EXPERT = 3584
DN_ALPHA = (2 * DEPTH) ** 0.25
LN_EPS = 1e-5

A_QK = 2 * MLSTM_HEADS * MLSTM_QK_DIM
A_V = MLSTM_HEADS * MLSTM_V_DIM
A_GATE = 2 * MLSTM_HEADS
B_Q = DSA_HEADS * DSA_HEAD_DIM
I_Q = IDX_HEADS * IDX_DIM
PROJ_SIZES = (A_QK, A_V, A_V, A_GATE, B_Q, DSA_LATENT, I_Q, IDX_DIM, IDX_HEADS, D_MODEL, D_MODEL)

LANES = 128
SUBLANES = 8
VMEM_LIMIT = 56 * 1024 * 1024
TOK_TILE = 512
MLSTM_TILE = 256
DSA_TQ = 128
DSA_KC = 256
MOE_TILE = 512
MOE_FT = 512
COMB_TILE = 256
FF_CHUNK = 256
INT_MIN = -(2 ** 31)
NEG_BIG = -1e30

BF16 = jnp.bfloat16
F32 = jnp.float32
NT_DIMS = (((1,), (1,)), ((), ()))


def _cparams(sem):
    return pltpu.CompilerParams(dimension_semantics=sem, vmem_limit_bytes=VMEM_LIMIT)


def _dot(a, b):
    return jnp.dot(a, b, preferred_element_type=F32)


def _dot_nt(a, b):
    return lax.dot_general(a, b, NT_DIMS, preferred_element_type=F32)


def _layer_norm(r, g, b):
    mu = jnp.mean(r, axis=-1, keepdims=True)
    d = r - mu
    var = jnp.mean(d * d, axis=-1, keepdims=True)
    return d * lax.rsqrt(var + LN_EPS) * g + b


def _silu(x):
    return x * jax.nn.sigmoid(x)


IN_PROJ_OUTS = (("qk", A_QK, F32), ("av", A_V, BF16), ("ao", A_V, BF16), ("bq", B_Q, BF16),
                ("iq", I_Q, BF16), ("ga", D_MODEL, BF16), ("gb", D_MODEL, BF16),
                ("aif", LANES, F32), ("bc", LANES, F32), ("ikw", LANES, F32))
IN_PROJ_COLS = sum(w for _, w, _ in IN_PROJ_OUTS)
IKW_W_OFF = IDX_DIM


def _regroup_in_proj(w):
    offs = np.concatenate([[0], np.cumsum(PROJ_SIZES)])
    a_qk, a_v, a_o, a_if, b_q, b_c, i_q, i_k, i_w, g_a, g_b = (w[:, offs[i]:offs[i + 1]] for i in range(11))
    pad = lambda a: jnp.pad(a, ((0, 0), (0, LANES - a.shape[1])))
    groups = [a_qk, a_v, a_o, b_q, i_q, g_a, g_b, pad(a_if), b_c, pad(jnp.concatenate([i_k, i_w], axis=1))]
    return jnp.concatenate(groups, axis=1).astype(BF16)


def _in_proj_kernel(x_ref, w_ref, *out_refs):
    xb = x_ref[...].astype(BF16)
    off = 0
    for (_, width, dtype), o_ref in zip(IN_PROJ_OUTS, out_refs):
        o_ref[...] = _dot(xb, w_ref[:, off:off + width]).astype(dtype)
        off += width


def _in_proj(x2, w):
    n = x2.shape[0]
    tm = min(TOK_TILE, n)
    return pl.pallas_call(
        _in_proj_kernel,
        grid=(n // tm,),
        in_specs=[pl.BlockSpec((tm, D_MODEL), lambda i: (i, 0)),
                  pl.BlockSpec((D_MODEL, IN_PROJ_COLS), lambda i: (0, 0))],
        out_specs=[pl.BlockSpec((tm, wd), lambda i: (i, 0)) for _, wd, _ in IN_PROJ_OUTS],
        out_shape=[jax.ShapeDtypeStruct((n, wd), dt) for _, wd, dt in IN_PROJ_OUTS],
        compiler_params=_cparams(("parallel",)),
        name="in_proj",
    )(x2, w)


def _log_sigmoid(x):
    return jnp.minimum(x, 0.0) - jnp.log1p(jnp.exp(-jnp.abs(x)))


def _mlstm_kernel(qk_ref, av_ref, aif_ref, ao_ref, convw_ref, gbias_ref, ng_ref, out_ref,
                  ext_ref, ct_ref, m_ref, *, tile):
    L = tile
    H, DK, DV = MLSTM_HEADS, MLSTM_QK_DIM, MLSTM_V_DIM

    @pl.when(pl.program_id(1) == 0)
    def _():
        ext_ref[0:SUBLANES, :] = jnp.zeros((SUBLANES, A_QK), F32)
        ct_ref[...] = jnp.zeros_like(ct_ref)
        m_ref[...] = jnp.zeros_like(m_ref)

    u = qk_ref[...]
    ext_ref[SUBLANES:SUBLANES + L, :] = u
    conv = jnp.zeros((L, A_QK), F32)
    for j in range(MLSTM_CONV):
        conv = conv + ext_ref[pl.ds(SUBLANES - (MLSTM_CONV - 1) + j, L), :] * convw_ref[j:j + 1, :]
    ext_ref[0:SUBLANES, :] = u[L - SUBLANES:L, :]
    qk = _silu(conv)

    g_col = aif_ref[...] + gbias_ref[...]
    g_row = g_col.T
    row = lax.broadcasted_iota(jnp.int32, (L, L), 0)
    col = lax.broadcasted_iota(jnp.int32, (L, L), 1)
    causal = row >= col
    tril = jnp.where(causal, 1.0, 0.0).astype(F32)
    triu = jnp.where(row <= col, 1.0, 0.0).astype(F32)
    b_col = jnp.dot(tril, _log_sigmoid(g_col), preferred_element_type=F32, precision=lax.Precision.HIGHEST)
    b_row = jnp.dot(_log_sigmoid(g_row), triu, preferred_element_type=F32, precision=lax.Precision.HIGHEST)

    ones_v = jnp.ones((L, DV), BF16)
    ones_t = jnp.ones((DV, L), F32)
    for h in range(H):
        bcol = b_col[:, H + h:H + h + 1]
        icol = g_col[:, h:h + 1]
        brow = b_row[H + h:H + h + 1, :]
        irow = g_row[h:h + 1, :]
        mprev = m_ref[h:h + 1, 0:1]
        dlog = jnp.where(causal, bcol - brow + irow, -jnp.inf)
        m_t = jnp.maximum(bcol + mprev, jnp.max(dlog, axis=1, keepdims=True))
        q = qk[:, h * DK:(h + 1) * DK].astype(BF16)
        k = qk[:, H * DK + h * DK:H * DK + (h + 1) * DK] * (DK ** -0.5)
        s_w = _dot_nt(q, k.astype(BF16)) * jnp.exp(dlog - m_t)
        inter = jnp.exp(bcol + mprev - m_t)
        v_h = av_ref[:, h * DV:(h + 1) * DV]
        v_ext = jnp.concatenate([v_h, ones_v], axis=1)
        ct_st = ct_ref[h]
        ne = _dot(s_w.astype(BF16), v_ext) + inter * _dot_nt(q, ct_st.astype(BF16))
        h_out = ne[:, :DV] / jnp.maximum(jnp.abs(ne[:, DV:]), jnp.exp(-m_t))
        blast = bcol[L - 1:L, :]
        glog = blast - bcol + icol
        m_new = jnp.maximum(blast + mprev, jnp.max(glog, axis=0, keepdims=True))
        decay = jnp.exp(blast + mprev - m_new)
        w_k = k * jnp.exp(glog - m_new)
        v_ext_t = jnp.concatenate([v_h.astype(F32).T, ones_t], axis=0).astype(BF16)
        ct_ref[h] = decay * ct_st + _dot(v_ext_t, w_k.astype(BF16))
        m_ref[h:h + 1, :] = jnp.broadcast_to(m_new, (1, LANES))
        mu = jnp.mean(h_out, axis=-1, keepdims=True)
        d = h_out - mu
        hn = d * lax.rsqrt(jnp.mean(d * d, axis=-1, keepdims=True) + LN_EPS)
        y = hn * ng_ref[:, h * DV:(h + 1) * DV] * jax.nn.sigmoid(ao_ref[:, h * DV:(h + 1) * DV].astype(F32))
        out_ref[:, h * DV:(h + 1) * DV] = y.astype(BF16)


def _mlstm(qk, av, aif, ao, conv_w, gate_bias, norm_g, bsz, seq):
    tile = min(MLSTM_TILE, seq)
    nc = seq // tile
    gbias = jnp.zeros((1, LANES), F32).at[0, :A_GATE].set(gate_bias.reshape(-1))
    tok = lambda w: pl.BlockSpec((tile, w), lambda b, c: (b * nc + c, 0))
    full = lambda shape: pl.BlockSpec(shape, lambda b, c: (0,) * len(shape))
    return pl.pallas_call(
        functools.partial(_mlstm_kernel, tile=tile),
        grid=(bsz, nc),
        in_specs=[tok(A_QK), tok(A_V), tok(LANES), tok(A_V),
                  full((MLSTM_CONV, A_QK)), full((1, LANES)), full((1, A_V))],
        out_specs=tok(A_V),
        out_shape=jax.ShapeDtypeStruct((bsz * seq, A_V), BF16),
        scratch_shapes=[pltpu.VMEM((tile + SUBLANES, A_QK), F32),
                        pltpu.VMEM((MLSTM_HEADS, 2 * MLSTM_V_DIM, MLSTM_QK_DIM), F32),
                        pltpu.VMEM((SUBLANES, LANES), F32)],
        compiler_params=_cparams(("parallel", "arbitrary")),
        name="mlstm",
    )(qk, av, aif, ao, conv_w, gbias, norm_g.reshape(1, A_V))


def _t5_bucket_np(dist):
    dist = np.maximum(dist, 0)
    exact = REL_BUCKETS // 2
    log_ratio = (np.log(np.maximum(dist, 1).astype(np.float32) / np.float32(exact))
                 / np.float32(math.log(REL_MAX_DIST / exact))).astype(np.float32)
    large = np.minimum(exact + (log_ratio * np.float32(REL_BUCKETS - exact)).astype(np.int32), REL_BUCKETS - 1)
    return np.where(dist < exact, dist, large)


def _near_bias_tables(rel_bias, tq, kc):
    n_off = kc // tq + 1
    t = np.arange(tq)[None, :, None]
    s = np.arange(kc)[None, None, :]
    off = (np.arange(n_off) * tq)[:, None, None]
    dist = off + t - s
    bucket = _t5_bucket_np(dist)
    assert int(_t5_bucket_np(np.array([tq + 1]))[0]) == REL_BUCKETS - 1
    tab = rel_bias[bucket] - rel_bias[REL_BUCKETS - 1]
    tab = jnp.where((dist >= 0)[..., None], tab, 0.0)
    return tab.transpose(0, 3, 1, 2).reshape(n_off, DSA_HEADS * tq, kc).astype(F32)


def _dsa_kernel(bq_ref, iq_ref, ikwq_ref, ikw_ref, ckv_ref, tab_ref, wuk_ref, wuv_ref, out_ref,
                key_ref, lg_ref, acc_ref, m_ref, ql_ref, wb_ref, *, tq, kc, topk):
    H = DSA_HEADS
    j = pl.program_id(1)
    t0 = j * tq
    nch = (t0 + tq - 1) // kc + 1
    ncol = kc // LANES

    bq = bq_ref[...]
    for h in range(H):
        ql_ref[h * tq:(h + 1) * tq, :] = (_dot(bq, wuk_ref[h]) * (DSA_HEAD_DIM ** -0.5)).astype(BF16)
    w_idx = ikwq_ref[:, IKW_W_OFF:IKW_W_OFF + IDX_HEADS] * ((IDX_HEADS * IDX_DIM) ** -0.5)
    for h in range(IDX_HEADS):
        wb_ref[h] = jnp.broadcast_to(w_idx[:, h:h + 1], (tq, kc))
    iq = iq_ref[...]
    qis = [iq[:, h * IDX_DIM:(h + 1) * IDX_DIM] for h in range(IDX_HEADS)]
    tpos = t0 + lax.broadcasted_iota(jnp.int32, (tq, kc), 0)
    lpos = lax.broadcasted_iota(jnp.int32, (tq, kc), 1)

    def score_chunk(c, carry):
        s0 = pl.multiple_of(c * kc, kc)
        kidx = ikw_ref[pl.ds(s0, kc), :][:, 0:IDX_DIM].astype(BF16)
        sc = jnp.zeros((tq, kc), F32)
        for h in range(IDX_HEADS):
            sc = sc + wb_ref[h] * jnp.maximum(_dot_nt(qis[h], kidx), 0.0)
        bits = lax.bitcast_convert_type(sc, jnp.int32)
        key = bits ^ ((bits >> 31) & 0x7FFFFFFF)
        key_ref[:, pl.ds(s0, kc)] = jnp.where(s0 + lpos <= tpos, key, INT_MIN)
        return carry

    lax.fori_loop(0, nch, score_chunk, 0)

    def count_ge(cand):
        def body(c, acc):
            s0 = pl.multiple_of(c * kc, kc)
            blk = key_ref[:, pl.ds(s0, kc)]
            for g in range(ncol):
                acc = acc + jnp.where(blk[:, g * LANES:(g + 1) * LANES] >= cand, 1.0, 0.0)
            return acc
        acc = lax.fori_loop(0, nch, body, jnp.zeros((tq, LANES), F32))
        return jnp.sum(acc, axis=1, keepdims=True)

    zero = jnp.zeros((tq, LANES), jnp.int32)
    thr = jnp.where(count_ge(zero) >= topk, zero, INT_MIN)

    def bit_step(i, thr):
        cand = thr + lax.shift_left(jnp.int32(1), 30 - i)
        return jnp.where(count_ge(cand) >= topk, cand, thr)

    thr = lax.fori_loop(0, 31, bit_step, thr)
    thr = jnp.maximum(thr, INT_MIN + 1)
    thr_w = jnp.concatenate([thr] * ncol, axis=1)

    m_ref[...] = jnp.full(m_ref.shape, NEG_BIG, F32)
    acc_ref[...] = jnp.zeros_like(acc_ref)
    ones_c = jnp.ones((kc, DSA_LATENT), BF16)

    def attend_chunk(c, carry):
        s0 = pl.multiple_of(c * kc, kc)
        kv = ckv_ref[pl.ds(s0, kc), :]
        lg_ref[...] = _dot_nt(ql_ref[...], kv)
        off = t0 - s0

        @pl.when(off <= kc)
        def _():
            lg_ref[...] = lg_ref[...] + tab_ref[off // tq]

        sel = key_ref[:, pl.ds(s0, kc)] >= thr_w
        kv_ext = jnp.concatenate([kv, ones_c], axis=1)
        for h in range(H):
            rows = slice(h * tq, (h + 1) * tq)
            x = jnp.where(sel, lg_ref[rows, :], NEG_BIG)
            m_old = m_ref[rows, :]
            m_new = jnp.maximum(m_old, jnp.max(x, axis=1, keepdims=True))
            alpha = jnp.exp(m_old - m_new)
            p = jnp.where(sel, jnp.exp(x - jnp.concatenate([m_new] * ncol, axis=1)), 0.0)
            acc_ref[rows, :] = jnp.concatenate([alpha, alpha], axis=1) * acc_ref[rows, :] + _dot(p.astype(BF16), kv_ext)
            m_ref[rows, :] = m_new
        return carry

    lax.fori_loop(0, nch, attend_chunk, 0)

    y = jnp.zeros((tq, B_Q), F32)
    for h in range(H):
        a = acc_ref[h * tq:(h + 1) * tq, :]
        o = a[:, :DSA_LATENT] / a[:, DSA_LATENT:]
        y = y + _dot(o.astype(BF16), wuv_ref[h])
    out_ref[...] = y.astype(BF16)


def _dsa(bq, iq, ikw, ckv, tab, wuk_pad, wuv_pad, bsz, seq):
    tq = min(DSA_TQ, seq)
    kc = min(DSA_KC, seq)
    nq = seq // tq
    topk = min(IDX_TOPK_MAX, seq // 4)
    n_off = kc // tq + 1
    qblk = lambda w: pl.BlockSpec((tq, w), lambda b, j: (b * nq + j, 0))
    seqblk = lambda w: pl.BlockSpec((seq, w), lambda b, j: (b, 0))
    full = lambda shape: pl.BlockSpec(shape, lambda b, j: (0,) * len(shape))
    return pl.pallas_call(
        functools.partial(_dsa_kernel, tq=tq, kc=kc, topk=topk),
        grid=(bsz, nq),
        in_specs=[qblk(B_Q), qblk(I_Q), qblk(LANES), seqblk(LANES), seqblk(DSA_LATENT),
                  full((n_off, DSA_HEADS * tq, kc)), full((DSA_HEADS, B_Q, DSA_LATENT)),
                  full((DSA_HEADS, DSA_LATENT, B_Q))],
        out_specs=qblk(B_Q),
        out_shape=jax.ShapeDtypeStruct((bsz * seq, B_Q), BF16),
        scratch_shapes=[pltpu.VMEM((tq, seq), jnp.int32),
                        pltpu.VMEM((DSA_HEADS * tq, kc), F32),
                        pltpu.VMEM((DSA_HEADS * tq, 2 * DSA_LATENT), F32),
                        pltpu.VMEM((DSA_HEADS * tq, LANES), F32),
                        pltpu.VMEM((DSA_HEADS * tq, DSA_LATENT), BF16),
                        pltpu.VMEM((IDX_HEADS, tq, kc), F32)],
        compiler_params=_cparams(("parallel", "arbitrary")),
        name="dsa",
    )(bq, iq, ikw, ikw, ckv, tab, wuk_pad, wuv_pad)


def _kv_norm_kernel(bc_ref, g_ref, out_ref):
    x = bc_ref[...]
    out_ref[...] = (x * lax.rsqrt(jnp.mean(x * x, axis=-1, keepdims=True) + LN_EPS) * g_ref[...]).astype(BF16)


def _kv_norm(bc, g):
    n = bc.shape[0]
    tm = min(4 * TOK_TILE, n)
    return pl.pallas_call(
        _kv_norm_kernel,
        grid=(n // tm,),
        in_specs=[pl.BlockSpec((tm, DSA_LATENT), lambda i: (i, 0)), pl.BlockSpec((1, DSA_LATENT), lambda i: (0, 0))],
        out_specs=pl.BlockSpec((tm, DSA_LATENT), lambda i: (i, 0)),
        out_shape=jax.ShapeDtypeStruct((n, DSA_LATENT), BF16),
        compiler_params=_cparams(("parallel",)),
        name="kv_norm",
    )(bc, g.reshape(1, DSA_LATENT))


def _merge_kernel(x_ref, ya_ref, yb_ref, ga_ref, gb_ref, wa_ref, wb_ref, wo_ref, g_ref, b_ref, out_ref):
    merged = (jax.nn.sigmoid(ga_ref[...].astype(F32)) * _dot(ya_ref[...], wa_ref[...])
              + jax.nn.sigmoid(gb_ref[...].astype(F32)) * _dot(yb_ref[...], wb_ref[...]))
    mix = _dot(merged.astype(BF16), wo_ref[...])
    out_ref[...] = _layer_norm(DN_ALPHA * x_ref[...] + mix, g_ref[...], b_ref[...])


def _merge(x2, ya, yb, ga, gb, wa, wb, wo, g, b):
    n = x2.shape[0]
    tm = min(TOK_TILE, n)
    tok = lambda w: pl.BlockSpec((tm, w), lambda i: (i, 0))
    full = lambda shape: pl.BlockSpec(shape, lambda i: (0,) * len(shape))
    return pl.pallas_call(
        _merge_kernel,
        grid=(n // tm,),
        in_specs=[tok(D_MODEL), tok(A_V), tok(B_Q), tok(D_MODEL), tok(D_MODEL),
                  full((A_V, D_MODEL)), full((B_Q, D_MODEL)), full((D_MODEL, D_MODEL)),
                  full((1, D_MODEL)), full((1, D_MODEL))],
        out_specs=tok(D_MODEL),
        out_shape=jax.ShapeDtypeStruct((n, D_MODEL), F32),
        compiler_params=_cparams(("parallel",)),
        name="merge_ln",
    )(x2, ya, yb, ga, gb, wa, wb, wo, g.reshape(1, D_MODEL), b.reshape(1, D_MODEL))


def _dense_ffn_kernel(x_ref, wg_ref, wu_ref, wd_ref, g_ref, b_ref, out_ref):
    x = x_ref[...]
    xb = x.astype(BF16)
    f = jnp.zeros(x.shape, F32)
    for c in range(D_FF // FF_CHUNK):
        cols = slice(c * FF_CHUNK, (c + 1) * FF_CHUNK)
        hdn = _silu(_dot(xb, wg_ref[:, cols])) * _dot(xb, wu_ref[:, cols])
        f = f + _dot(hdn.astype(BF16), wd_ref[cols, :])
    out_ref[...] = _layer_norm(DN_ALPHA * x + f, g_ref[...], b_ref[...])


def _dense_ffn(x2, wg, wu, wd, g, b):
    n = x2.shape[0]
    tm = min(TOK_TILE, n)
    tok = pl.BlockSpec((tm, D_MODEL), lambda i: (i, 0))
    full = lambda shape: pl.BlockSpec(shape, lambda i: (0,) * len(shape))
    return pl.pallas_call(
        _dense_ffn_kernel,
        grid=(n // tm,),
        in_specs=[tok, full((D_MODEL, D_FF)), full((D_MODEL, D_FF)), full((D_FF, D_MODEL)),
                  full((1, D_MODEL)), full((1, D_MODEL))],
        out_specs=tok,
        out_shape=jax.ShapeDtypeStruct((n, D_MODEL), F32),
        compiler_params=_cparams(("parallel",)),
        name="dense_ffn_ln",
    )(x2, wg, wu, wd, g.reshape(1, D_MODEL), b.reshape(1, D_MODEL))


def _router_kernel(x_ref, rw_ref, e_ref, g_ref):
    logits = lax.dot_general(rw_ref[...], x_ref[...], NT_DIMS, preferred_element_type=F32,
                             precision=lax.Precision.HIGHEST)
    eidx = lax.broadcasted_iota(jnp.int32, logits.shape, 0)
    m1 = jnp.max(logits, axis=0, keepdims=True)
    i1 = jnp.min(jnp.where(logits == m1, eidx, N_EXPERTS), axis=0, keepdims=True)
    rest = jnp.where(eidx == i1, -jnp.inf, logits)
    m2 = jnp.max(rest, axis=0, keepdims=True)
    i2 = jnp.min(jnp.where(rest == m2, eidx, N_EXPERTS), axis=0, keepdims=True)
    e2 = jnp.exp(m2 - m1)
    den = 1.0 + e2
    e_ref[0:1, :] = i1
    e_ref[1:2, :] = i2
    g_ref[0:1, :] = 1.0 / den
    g_ref[1:2, :] = e2 / den


def _router(x2, rw_t):
    n = x2.shape[0]
    tm = min(TOK_TILE, n)
    return pl.pallas_call(
        _router_kernel,
        grid=(n // tm,),
        in_specs=[pl.BlockSpec((tm, D_MODEL), lambda i: (i, 0)), pl.BlockSpec((N_EXPERTS, D_MODEL), lambda i: (0, 0))],
        out_specs=[pl.BlockSpec((TOP_K, tm), lambda i: (0, i)), pl.BlockSpec((TOP_K, tm), lambda i: (0, i))],
        out_shape=[jax.ShapeDtypeStruct((TOP_K, n), jnp.int32), jax.ShapeDtypeStruct((TOP_K, n), F32)],
        compiler_params=_cparams(("parallel",)),
        name="router",
    )(x2, rw_t)


def _row_copy(src_hbm, row, dst, r, sem):
    return pltpu.make_async_copy(src_hbm.at[pl.ds(row, 1)], dst.at[pl.ds(r, 1)], sem)


def _expert_kernel(blk_e_ref, nused_ref, tok_ref, g_ref, x_hbm, wg_ref, wu_ref, wd_ref, y_ref,
                   xbuf, xb, acc, sem, *, tile):
    i = pl.program_id(0)
    f = pl.program_id(1)
    used = i < nused_ref[0]
    last = f == pl.num_programs(1) - 1

    @pl.when(jnp.logical_and(used, f == 0))
    def _():
        def start(r, c):
            _row_copy(x_hbm, tok_ref[0, r], xbuf, r, sem).start()
            return c
        lax.fori_loop(0, tile, start, 0)

        def wait(r, c):
            _row_copy(x_hbm, 0, xbuf, r, sem).wait()
            return c
        lax.fori_loop(0, tile, wait, 0)
        xb[...] = xbuf[...].astype(BF16)
        acc[...] = jnp.zeros_like(acc)

    @pl.when(used)
    def _():
        x = xb[...]
        hdn = _silu(_dot(x, wg_ref[...])) * _dot(x, wu_ref[...])
        acc[...] += _dot(hdn.astype(BF16), wd_ref[...])

    @pl.when(jnp.logical_and(used, last))
    def _():
        y_ref[...] = acc[...] * g_ref[...]

    @pl.when(jnp.logical_and(jnp.logical_not(used), last))
    def _():
        y_ref[...] = jnp.zeros_like(y_ref)


def _experts(x2, buf_tok, buf_g, blk_e, nused, wg, wu, wd):
    total = buf_tok.shape[0]
    tile = MOE_TILE
    nblk = total // tile
    nf = D_FF_EXPERT // MOE_FT
    grid_spec = pltpu.PrefetchScalarGridSpec(
        num_scalar_prefetch=2,
        grid=(nblk, nf),
        in_specs=[pl.BlockSpec((None, 1, tile), lambda i, f, be, nu: (i, 0, 0), memory_space=pltpu.SMEM),
                  pl.BlockSpec((tile, 1), lambda i, f, be, nu: (i, 0)),
                  pl.BlockSpec(memory_space=pl.ANY),
                  pl.BlockSpec((None, D_MODEL, MOE_FT), lambda i, f, be, nu: (be[i], 0, f)),
                  pl.BlockSpec((None, D_MODEL, MOE_FT), lambda i, f, be, nu: (be[i], 0, f)),
                  pl.BlockSpec((None, MOE_FT, D_MODEL), lambda i, f, be, nu: (be[i], f, 0))],
        out_specs=pl.BlockSpec((tile, D_MODEL), lambda i, f, be, nu: (i, 0)),
        scratch_shapes=[pltpu.VMEM((tile, D_MODEL), F32), pltpu.VMEM((tile, D_MODEL), BF16),
                        pltpu.VMEM((tile, D_MODEL), F32), pltpu.SemaphoreType.DMA(())],
    )
    return pl.pallas_call(
        functools.partial(_expert_kernel, tile=tile),
        grid_spec=grid_spec,
        out_shape=jax.ShapeDtypeStruct((total, D_MODEL), F32),
        compiler_params=_cparams(("arbitrary", "arbitrary")),
        name="experts",
    )(blk_e, nused, buf_tok.reshape(nblk, 1, tile), buf_g.reshape(total, 1), x2, wg, wu, wd)


def _combine_kernel(p0_ref, p1_ref, y_hbm, x_ref, g_ref, b_ref, out_ref, buf0, buf1, sem, *, tile):
    def start(r, c):
        _row_copy(y_hbm, p0_ref[0, r], buf0, r, sem).start()
        _row_copy(y_hbm, p1_ref[0, r], buf1, r, sem).start()
        return c
    lax.fori_loop(0, tile, start, 0)

    def wait(r, c):
        _row_copy(y_hbm, 0, buf0, r, sem).wait()
        _row_copy(y_hbm, 0, buf1, r, sem).wait()
        return c
    lax.fori_loop(0, tile, wait, 0)
    f = buf0[...] + buf1[...]
    out_ref[...] = _layer_norm(DN_ALPHA * x_ref[...] + f, g_ref[...], b_ref[...])


def _combine(y, pos, x2, g, b):
    n = x2.shape[0]
    tile = min(COMB_TILE, n)
    nb = n // tile
    smem = pl.BlockSpec((None, 1, tile), lambda i: (i, 0, 0), memory_space=pltpu.SMEM)
    full = pl.BlockSpec((1, D_MODEL), lambda i: (0, 0))
    return pl.pallas_call(
        functools.partial(_combine_kernel, tile=tile),
        grid=(nb,),
        in_specs=[smem, smem, pl.BlockSpec(memory_space=pl.ANY),
                  pl.BlockSpec((tile, D_MODEL), lambda i: (i, 0)), full, full],
        out_specs=pl.BlockSpec((tile, D_MODEL), lambda i: (i, 0)),
        out_shape=jax.ShapeDtypeStruct((n, D_MODEL), F32),
        scratch_shapes=[pltpu.VMEM((tile, D_MODEL), F32), pltpu.VMEM((tile, D_MODEL), F32),
                        pltpu.SemaphoreType.DMA(())],
        compiler_params=_cparams(("arbitrary",)),
        name="moe_combine_ln",
    )(pos[:, 0].reshape(nb, 1, tile), pos[:, 1].reshape(nb, 1, tile), y, x2,
      g.reshape(1, D_MODEL), b.reshape(1, D_MODEL))


def _moe(x2, router_w, wg, wu, wd, g, b):
    n = x2.shape[0]
    top_e, gates = _router(x2, router_w.T)
    flat_e = top_e.T.reshape(-1)
    flat_g = gates.T.reshape(-1)
    flat_tok = jnp.repeat(jnp.arange(n, dtype=jnp.int32), TOP_K)
    onehot = (flat_e[:, None] == jnp.arange(N_EXPERTS, dtype=jnp.int32)[None, :]).astype(jnp.int32)
    rank = jnp.sum((jnp.cumsum(onehot, axis=0) - onehot) * onehot, axis=1)
    counts = jnp.sum(onehot, axis=0)
    padded = ((counts + MOE_TILE - 1) // MOE_TILE) * MOE_TILE
    pad_end = jnp.cumsum(padded)
    pad_start = pad_end - padded
    dest = (pad_start[flat_e] + rank).astype(jnp.int32)
    total = n * TOP_K + N_EXPERTS * MOE_TILE
    nblk = total // MOE_TILE
    buf_tok = jnp.zeros((total,), jnp.int32).at[dest].set(flat_tok)
    buf_g = jnp.zeros((total,), F32).at[dest].set(flat_g)
    blk_e = jnp.minimum(jnp.searchsorted(pad_end, jnp.arange(nblk, dtype=jnp.int32) * MOE_TILE, side='right'),
                        N_EXPERTS - 1).astype(jnp.int32)
    nused = (pad_end[-1:] // MOE_TILE).astype(jnp.int32)
    y = _experts(x2, buf_tok, buf_g, blk_e, nused, wg, wu, wd)
    return _combine(y, dest.reshape(n, TOP_K), x2, g, b)


def _pad_head_proj(w_uk, w_uv):
    H, DH, DC = DSA_HEADS, DSA_HEAD_DIM, DSA_LATENT
    eye = jnp.eye(H, dtype=w_uk.dtype)
    wuk_pad = jnp.einsum('hg,hdc->hgdc', eye, w_uk).reshape(H, H * DH, DC)
    wuv_pad = jnp.einsum('hg,hcd->hcgd', eye, w_uv).reshape(H, DC, H * DH)
    return wuk_pad.astype(BF16), wuv_pad.astype(BF16)


def kernel(x, w_in, mlstm_conv_w, mlstm_gate_bias, mlstm_norm_g, dsa_kv_norm_g, dsa_w_uk, dsa_w_uv, rel_bias,
           w_branch_a, w_branch_b, w_out, ln_g, ln_b, dense_w_gate, dense_w_up, dense_w_down, router_w,
           expert_w_gate, expert_w_up, expert_w_down):
    bsz, seq, d = x.shape
    n = bsz * seq
    x2 = x.reshape(n, d)
    tab = _near_bias_tables(rel_bias, min(DSA_TQ, seq), min(DSA_KC, seq))
    for l in range(DEPTH):
        p = dict(zip([nm for nm, _, _ in IN_PROJ_OUTS], _in_proj(x2, _regroup_in_proj(w_in[l]))))
        ya = _mlstm(p["qk"], p["av"], p["aif"], p["ao"], mlstm_conv_w[l], mlstm_gate_bias[l], mlstm_norm_g[l],
                    bsz, seq)
        ckv = _kv_norm(p["bc"], dsa_kv_norm_g[l])
        wuk_pad, wuv_pad = _pad_head_proj(dsa_w_uk[l], dsa_w_uv[l])
        yb = _dsa(p["bq"], p["iq"], p["ikw"], ckv, tab, wuk_pad, wuv_pad, bsz, seq)
        x2 = _merge(x2, ya, yb, p["ga"], p["gb"], w_branch_a[l].astype(BF16), w_branch_b[l].astype(BF16),
                    w_out[l].astype(BF16), ln_g[l, 0], ln_b[l, 0])
        j = l // 2
        if l % 2 == 0:
            x2 = _dense_ffn(x2, dense_w_gate[j].astype(BF16), dense_w_up[j].astype(BF16),
                            dense_w_down[j].astype(BF16), ln_g[l, 1], ln_b[l, 1])
        else:
            x2 = _moe(x2, router_w[j], expert_w_gate[j].astype(BF16), expert_w_up[j].astype(BF16),
                      expert_w_down[j].astype(BF16), ln_g[l, 1], ln_b[l, 1])
    return x2.reshape(bsz, seq, d)
```

```python
import functools
import math

import numpy as np
import jax
import jax.numpy as jnp
from jax import lax
from jax.experimental import pallas as pl
from jax.experimental.pallas import tpu as pltpu

D_MODEL = 1024
DEPTH = 4
MLSTM_HEADS = 4
MLSTM_QK_DIM = 64
MLSTM_V_DIM = 128
MLSTM_CONV = 4
DSA_HEADS = 8
DSA_HEAD_DIM = 64
DSA_LATENT = 128
IDX_HEADS = 8
IDX_DIM = 32
IDX_TOPK_MAX = 256
REL_BUCKETS = 32
REL_MAX_DIST = 128
D_FF = 2816
N_EXPERTS = 8
TOP_K = 2
D_FF_EXPERT = 3584
DN_ALPHA = (2 * DEPTH) ** 0.25
LN_EPS = 1e-5

A_QK = 2 * MLSTM_HEADS * MLSTM_QK_DIM
A_V = MLSTM_HEADS * MLSTM_V_DIM
A_GATE = 2 * MLSTM_HEADS
B_Q = DSA_HEADS * DSA_HEAD_DIM
I_Q = IDX_HEADS * IDX_DIM
PROJ_SIZES = (A_QK, A_V, A_V, A_GATE, B_Q, DSA_LATENT, I_Q, IDX_DIM, IDX_HEADS, D_MODEL, D_MODEL)

LANES = 128
SUBLANES = 8
VMEM_LIMIT = 56 * 1024 * 1024
TOK_TILE = 512
MLSTM_TILE = 256
DSA_TQ = 128
DSA_KN = 256
DSA_KF = 512
MOE_TILE = 512
MOE_FT = 512
COMB_TILE = 256
DMA_UNROLL = 8
FF_CHUNK = 256
INT_MIN = -(2 ** 31)
NEG_BIG = -1e30
M_FLOOR = -1e29
LOG2E = math.log2(math.e)

BF16 = jnp.bfloat16
F32 = jnp.float32
NT_DIMS = (((1,), (1,)), ((), ()))


def _cparams(sem):
    return pltpu.CompilerParams(dimension_semantics=sem, vmem_limit_bytes=VMEM_LIMIT)


def _dot(a, b):
    return jnp.dot(a, b, preferred_element_type=F32)


def _dot_nt(a, b):
    return lax.dot_general(a, b, NT_DIMS, preferred_element_type=F32)


def _layer_norm(r, g, b):
    mu = jnp.mean(r, axis=-1, keepdims=True)
    d = r - mu
    var = jnp.mean(d * d, axis=-1, keepdims=True)
    return d * lax.rsqrt(var + LN_EPS) * g + b


def _silu(x):
    return x * jax.nn.sigmoid(x)


IN_PROJ_OUTS = (("qk", A_QK, F32), ("av", A_V, BF16), ("ao", A_V, BF16), ("bq", B_Q, BF16),
                ("iq", I_Q, BF16), ("ga", D_MODEL, BF16), ("gb", D_MODEL, BF16),
                ("aif", LANES, F32), ("bc", LANES, F32), ("ikw", LANES, F32))
IN_PROJ_COLS = sum(w for _, w, _ in IN_PROJ_OUTS)
IKW_W_OFF = IDX_DIM


def _regroup_in_proj(w):
    offs = np.concatenate([[0], np.cumsum(PROJ_SIZES)])
    a_qk, a_v, a_o, a_if, b_q, b_c, i_q, i_k, i_w, g_a, g_b = (w[:, offs[i]:offs[i + 1]] for i in range(11))
    pad = lambda a: jnp.pad(a, ((0, 0), (0, LANES - a.shape[1])))
    groups = [a_qk, a_v, a_o, b_q, i_q, g_a, g_b, pad(a_if), b_c, pad(jnp.concatenate([i_k, i_w], axis=1))]
    return jnp.concatenate(groups, axis=1).astype(BF16)


def _in_proj_kernel(x_ref, w_ref, *out_refs):
    xb = x_ref[...].astype(BF16)
    off = 0
    for (_, width, dtype), o_ref in zip(IN_PROJ_OUTS, out_refs):
        o_ref[...] = _dot(xb, w_ref[:, off:off + width]).astype(dtype)
        off += width


def _in_proj(x2, w):
    n = x2.shape[0]
    tm = min(TOK_TILE, n)
    return pl.pallas_call(
        _in_proj_kernel,
        grid=(n // tm,),
        in_specs=[pl.BlockSpec((tm, D_MODEL), lambda i: (i, 0)),
                  pl.BlockSpec((D_MODEL, IN_PROJ_COLS), lambda i: (0, 0))],
        out_specs=[pl.BlockSpec((tm, wd), lambda i: (i, 0)) for _, wd, _ in IN_PROJ_OUTS],
        out_shape=[jax.ShapeDtypeStruct((n, wd), dt) for _, wd, dt in IN_PROJ_OUTS],
        compiler_params=_cparams(("parallel",)),
        name="in_proj",
    )(x2, w)


def _log_sigmoid(x):
    return jnp.minimum(x, 0.0) - jnp.log1p(jnp.exp(-jnp.abs(x)))


def _mlstm_kernel(qk_ref, av_ref, aif_ref, ao_ref, convw_ref, gbias_ref, ng_ref, out_ref,
                  ext_ref, ct_ref, m_ref, *, tile):
    L = tile
    H, DK, DV = MLSTM_HEADS, MLSTM_QK_DIM, MLSTM_V_DIM

    @pl.when(pl.program_id(1) == 0)
    def _():
        ext_ref[0:SUBLANES, :] = jnp.zeros((SUBLANES, A_QK), F32)
        ct_ref[...] = jnp.zeros_like(ct_ref)
        m_ref[...] = jnp.zeros_like(m_ref)

    u = qk_ref[...]
    ext_ref[SUBLANES:SUBLANES + L, :] = u
    conv = jnp.zeros((L, A_QK), F32)
    for j in range(MLSTM_CONV):
        conv = conv + ext_ref[pl.ds(SUBLANES - (MLSTM_CONV - 1) + j, L), :] * convw_ref[j:j + 1, :]
    ext_ref[0:SUBLANES, :] = u[L - SUBLANES:L, :]
    qk = _silu(conv)

    g_col = aif_ref[...] + gbias_ref[...]
    g_row = g_col.T
    row = lax.broadcasted_iota(jnp.int32, (L, L), 0)
    col = lax.broadcasted_iota(jnp.int32, (L, L), 1)
    causal = row >= col
    tril = jnp.where(causal, 1.0, 0.0).astype(F32)
    triu = jnp.where(row <= col, 1.0, 0.0).astype(F32)
    b_col = jnp.dot(tril, _log_sigmoid(g_col), preferred_element_type=F32, precision=lax.Precision.HIGHEST)
    b_row = jnp.dot(_log_sigmoid(g_row), triu, preferred_element_type=F32, precision=lax.Precision.HIGHEST)

    ones_v = jnp.ones((L, DV), BF16)
    ones_t = jnp.ones((DV, L), F32)
    for h in range(H):
        bcol = b_col[:, H + h:H + h + 1]
        icol = g_col[:, h:h + 1]
        brow = b_row[H + h:H + h + 1, :]
        irow = g_row[h:h + 1, :]
        mprev = m_ref[h:h + 1, 0:1]
        dlog = jnp.where(causal, bcol - brow + irow, -jnp.inf)
        m_t = jnp.maximum(bcol + mprev, jnp.max(dlog, axis=1, keepdims=True))
        q = qk[:, h * DK:(h + 1) * DK].astype(BF16)
        k = qk[:, H * DK + h * DK:H * DK + (h + 1) * DK] * (DK ** -0.5)
        s_w = _dot_nt(q, k.astype(BF16)) * jnp.exp(dlog - m_t)
        inter = jnp.exp(bcol + mprev - m_t)
        v_h = av_ref[:, h * DV:(h + 1) * DV]
        v_ext = jnp.concatenate([v_h, ones_v], axis=1)
        ct_st = ct_ref[h]
        ne = _dot(s_w.astype(BF16), v_ext) + inter * _dot_nt(q, ct_st.astype(BF16))
        h_out = ne[:, :DV] / jnp.maximum(jnp.abs(ne[:, DV:]), jnp.exp(-m_t))
        blast = bcol[L - 1:L, :]
        glog = blast - bcol + icol
        m_new = jnp.maximum(blast + mprev, jnp.max(glog, axis=0, keepdims=True))
        decay = jnp.exp(blast + mprev - m_new)
        w_k = k * jnp.exp(glog - m_new)
        v_ext_t = jnp.concatenate([v_h.astype(F32).T, ones_t], axis=0).astype(BF16)
        ct_ref[h] = decay * ct_st + _dot(v_ext_t, w_k.astype(BF16))
        m_ref[h:h + 1, :] = jnp.broadcast_to(m_new, (1, LANES))
        mu = jnp.mean(h_out, axis=-1, keepdims=True)
        d = h_out - mu
        hn = d * lax.rsqrt(jnp.mean(d * d, axis=-1, keepdims=True) + LN_EPS)
        y = hn * ng_ref[:, h * DV:(h + 1) * DV] * jax.nn.sigmoid(ao_ref[:, h * DV:(h + 1) * DV].astype(F32))
        out_ref[:, h * DV:(h + 1) * DV] = y.astype(BF16)


def _mlstm(qk, av, aif, ao, conv_w, gate_bias, norm_g, bsz, seq):
    tile = min(MLSTM_TILE, seq)
    nc = seq // tile
    gbias = jnp.zeros((1, LANES), F32).at[0, :A_GATE].set(gate_bias.reshape(-1))
    tok = lambda w: pl.BlockSpec((tile, w), lambda b, c: (b * nc + c, 0))
    full = lambda shape: pl.BlockSpec(shape, lambda b, c: (0,) * len(shape))
    return pl.pallas_call(
        functools.partial(_mlstm_kernel, tile=tile),
        grid=(bsz, nc),
        in_specs=[tok(A_QK), tok(A_V), tok(LANES), tok(A_V),
                  full((MLSTM_CONV, A_QK)), full((1, LANES)), full((1, A_V))],
        out_specs=tok(A_V),
        out_shape=jax.ShapeDtypeStruct((bsz * seq, A_V), BF16),
        scratch_shapes=[pltpu.VMEM((tile + SUBLANES, A_QK), F32),
                        pltpu.VMEM((MLSTM_HEADS, 2 * MLSTM_V_DIM, MLSTM_QK_DIM), F32),
                        pltpu.VMEM((SUBLANES, LANES), F32)],
        compiler_params=_cparams(("parallel", "arbitrary")),
        name="mlstm",
    )(qk, av, aif, ao, conv_w, gbias, norm_g.reshape(1, A_V))


def _t5_bucket_np(dist):
    dist = np.maximum(dist, 0)
    exact = REL_BUCKETS // 2
    log_ratio = (np.log(np.maximum(dist, 1).astype(np.float32) / np.float32(exact))
                 / np.float32(math.log(REL_MAX_DIST / exact))).astype(np.float32)
    large = np.minimum(exact + (log_ratio * np.float32(REL_BUCKETS - exact)).astype(np.int32), REL_BUCKETS - 1)
    return np.where(dist < exact, dist, large)


def _near_bias_tables(rel_bias, tq, kn, kf):
    n_off = kf // tq + 1
    t = np.arange(tq)[None, :, None]
    s = np.arange(kn)[None, None, :]
    off = (np.arange(n_off) * tq)[:, None, None]
    dist = off + t - s
    bucket = _t5_bucket_np(dist)
    assert np.all(_t5_bucket_np(np.arange(tq + 1, 4 * REL_MAX_DIST)) == REL_BUCKETS - 1)
    tab = rel_bias[bucket] - rel_bias[REL_BUCKETS - 1]
    tab = jnp.where((dist >= 0)[..., None], tab, 0.0) * LOG2E
    return tab.transpose(0, 3, 1, 2).reshape(n_off, DSA_HEADS * tq, kn).astype(F32)


def _dsa_kernel(bq_ref, iq_ref, ikwq_ref, ikw_ref, ckv_ref, tab_ref, wuk_ref, wuv_ref, out_ref,
                key_ref, acc_ref, m_ref, ql_ref, wb_ref, *, tq, kn, kf, topk):
    H = DSA_HEADS
    j = pl.program_id(1)
    t0 = j * tq
    n_score = (t0 + tq - 1) // kf + 1
    n_far = jnp.maximum(t0 - tq, 0) // kf
    near_lo = n_far * (kf // kn)
    near_hi = (t0 + tq - 1) // kn + 1

    bq = bq_ref[...]
    for h in range(H):
        ql_ref[h * tq:(h + 1) * tq, :] = (_dot(bq, wuk_ref[h]) * (DSA_HEAD_DIM ** -0.5 * LOG2E)).astype(BF16)
    w_idx = ikwq_ref[:, IKW_W_OFF:IKW_W_OFF + IDX_HEADS] * ((IDX_HEADS * IDX_DIM) ** -0.5)
    for h in range(IDX_HEADS):
        wb_ref[h] = jnp.broadcast_to(w_idx[:, h:h + 1], (tq, kf))
    iq = iq_ref[...]
    qis = [iq[:, h * IDX_DIM:(h + 1) * IDX_DIM] for h in range(IDX_HEADS)]
    tpos = t0 + lax.broadcasted_iota(jnp.int32, (tq, kf), 0)
    lpos = lax.broadcasted_iota(jnp.int32, (tq, kf), 1)

    def score_chunk(c, carry):
        s0 = pl.multiple_of(c * kf, kf)
        kidx = ikw_ref[pl.ds(s0, kf), :][:, 0:IDX_DIM].astype(BF16)
        sc = jnp.zeros((tq, kf), F32)
        for h in range(IDX_HEADS):
            sc = sc + wb_ref[h] * jnp.maximum(_dot_nt(qis[h], kidx), 0.0)
        bits = lax.bitcast_convert_type(sc, jnp.int32)
        key = bits ^ ((bits >> 31) & 0x7FFFFFFF)
        key_ref[:, pl.ds(s0, kf)] = jnp.where(s0 + lpos <= tpos, key, INT_MIN)
        return carry

    lax.fori_loop(0, n_score, score_chunk, 0)

    def count_ge(cand):
        def body(c, acc):
            s0 = pl.multiple_of(c * kf, kf)
            blk = key_ref[:, pl.ds(s0, kf)]
            for g in range(kf // LANES):
                acc = acc + jnp.where(blk[:, g * LANES:(g + 1) * LANES] >= cand, 1.0, 0.0)
            return acc
        acc = lax.fori_loop(0, n_score, body, jnp.zeros((tq, LANES), F32))
        return jnp.sum(acc, axis=1, keepdims=True)

    zero = jnp.zeros((tq, LANES), jnp.int32)
    thr = jnp.where(count_ge(zero) >= topk, zero, INT_MIN)

    def bit_step(i, thr):
        cand = thr + lax.shift_left(jnp.int32(1), 30 - i)
        return jnp.where(count_ge(cand) >= topk, cand, thr)

    thr = lax.fori_loop(0, 31, bit_step, thr)
    thr = jnp.maximum(thr, INT_MIN + 1)

    m_ref[...] = jnp.full(m_ref.shape, M_FLOOR, F32)
    acc_ref[...] = jnp.zeros_like(acc_ref)

    def attend(s0, width, table_idx):
        ncol = width // LANES
        kv = ckv_ref[pl.ds(s0, width), :]
        kv_ext = jnp.concatenate([kv, jnp.ones((width, DSA_LATENT), BF16)], axis=1)
        thr_w = jnp.concatenate([thr] * ncol, axis=1)
        madd = jnp.where(key_ref[:, pl.ds(s0, width)] >= thr_w, 0.0, NEG_BIG)
        for h in range(H):
            rows = slice(h * tq, (h + 1) * tq)
            x = _dot_nt(ql_ref[rows, :], kv) + madd
            if table_idx is not None:
                x = x + tab_ref[table_idx, rows, :]
            m_old = m_ref[rows, :]
            m_new = jnp.maximum(m_old, jnp.max(x, axis=1, keepdims=True))
            alpha = jnp.exp2(m_old - m_new)
            p = jnp.exp2(x - jnp.concatenate([m_new] * ncol, axis=1))
            acc_ref[rows, :] = jnp.concatenate([alpha, alpha], axis=1) * acc_ref[rows, :] + _dot(p.astype(BF16), kv_ext)
            m_ref[rows, :] = m_new

    def far_chunk(c, carry):
        attend(pl.multiple_of(c * kf, kf), kf, None)
        return carry

    def near_chunk(c, carry):
        s0 = pl.multiple_of(c * kn, kn)
        attend(s0, kn, (t0 - s0) // tq)
        return carry

    lax.fori_loop(0, n_far, far_chunk, 0)
    lax.fori_loop(near_lo, near_hi, near_chunk, 0)

    y = jnp.zeros((tq, B_Q), F32)
    for h in range(H):
        a = acc_ref[h * tq:(h + 1) * tq, :]
        o = a[:, :DSA_LATENT] / a[:, DSA_LATENT:]
        y = y + _dot(o.astype(BF16), wuv_ref[h])
    out_ref[...] = y.astype(BF16)


def _dsa_tiles(seq):
    tq = min(DSA_TQ, seq)
    return tq, min(DSA_KN, seq), min(DSA_KF, seq)


def _dsa(bq, iq, ikw, ckv, tab, wuk_pad, wuv_pad, bsz, seq):
    tq, kn, kf = _dsa_tiles(seq)
    nq = seq // tq
    topk = min(IDX_TOPK_MAX, seq // 4)
    n_off = kf // tq + 1
    qblk = lambda w: pl.BlockSpec((tq, w), lambda b, j: (b * nq + j, 0))
    seqblk = lambda w: pl.BlockSpec((seq, w), lambda b, j: (b, 0))
    full = lambda shape: pl.BlockSpec(shape, lambda b, j: (0,) * len(shape))
    return pl.pallas_call(
        functools.partial(_dsa_kernel, tq=tq, kn=kn, kf=kf, topk=topk),
        grid=(bsz, nq),
        in_specs=[qblk(B_Q), qblk(I_Q), qblk(LANES), seqblk(LANES), seqblk(DSA_LATENT),
                  full((n_off, DSA_HEADS * tq, kn)), full((DSA_HEADS, B_Q, DSA_LATENT)),
                  full((DSA_HEADS, DSA_LATENT, B_Q))],
        out_specs=qblk(B_Q),
        out_shape=jax.ShapeDtypeStruct((bsz * seq, B_Q), BF16),
        scratch_shapes=[pltpu.VMEM((tq, seq), jnp.int32),
                        pltpu.VMEM((DSA_HEADS * tq, 2 * DSA_LATENT), F32),
                        pltpu.VMEM((DSA_HEADS * tq, LANES), F32),
                        pltpu.VMEM((DSA_HEADS * tq, DSA_LATENT), BF16),
                        pltpu.VMEM((IDX_HEADS, tq, kf), F32)],
        compiler_params=_cparams(("parallel", "arbitrary")),
        name="dsa",
    )(bq, iq, ikw, ikw, ckv, tab, wuk_pad, wuv_pad)


def _kv_norm_kernel(bc_ref, g_ref, out_ref):
    x = bc_ref[...]
    out_ref[...] = (x * lax.rsqrt(jnp.mean(x * x, axis=-1, keepdims=True) + LN_EPS) * g_ref[...]).astype(BF16)


def _kv_norm(bc, g):
    n = bc.shape[0]
    tm = min(4 * TOK_TILE, n)
    return pl.pallas_call(
        _kv_norm_kernel,
        grid=(n // tm,),
        in_specs=[pl.BlockSpec((tm, DSA_LATENT), lambda i: (i, 0)), pl.BlockSpec((1, DSA_LATENT), lambda i: (0, 0))],
        out_specs=pl.BlockSpec((tm, DSA_LATENT), lambda i: (i, 0)),
        out_shape=jax.ShapeDtypeStruct((n, DSA_LATENT), BF16),
        compiler_params=_cparams(("parallel",)),
        name="kv_norm",
    )(bc, g.reshape(1, DSA_LATENT))


def _merge_kernel(x_ref, ya_ref, yb_ref, ga_ref, gb_ref, wa_ref, wb_ref, wo_ref, g_ref, b_ref, out_ref):
    merged = (jax.nn.sigmoid(ga_ref[...].astype(F32)) * _dot(ya_ref[...], wa_ref[...])
              + jax.nn.sigmoid(gb_ref[...].astype(F32)) * _dot(yb_ref[...], wb_ref[...]))
    mix = _dot(merged.astype(BF16), wo_ref[...])
    out_ref[...] = _layer_norm(DN_ALPHA * x_ref[...] + mix, g_ref[...], b_ref[...])


def _merge(x2, ya, yb, ga, gb, wa, wb, wo, g, b):
    n = x2.shape[0]
    tm = min(TOK_TILE, n)
    tok = lambda w: pl.BlockSpec((tm, w), lambda i: (i, 0))
    full = lambda shape: pl.BlockSpec(shape, lambda i: (0,) * len(shape))
    return pl.pallas_call(
        _merge_kernel,
        grid=(n // tm,),
        in_specs=[tok(D_MODEL), tok(A_V), tok(B_Q), tok(D_MODEL), tok(D_MODEL),
                  full((A_V, D_MODEL)), full((B_Q, D_MODEL)), full((D_MODEL, D_MODEL)),
                  full((1, D_MODEL)), full((1, D_MODEL))],
        out_specs=tok(D_MODEL),
        out_shape=jax.ShapeDtypeStruct((n, D_MODEL), F32),
        compiler_params=_cparams(("parallel",)),
        name="merge_ln",
    )(x2, ya, yb, ga, gb, wa, wb, wo, g.reshape(1, D_MODEL), b.reshape(1, D_MODEL))


def _dense_ffn_kernel(x_ref, wg_ref, wu_ref, wd_ref, g_ref, b_ref, out_ref):
    x = x_ref[...]
    xb = x.astype(BF16)
    f = jnp.zeros(x.shape, F32)
    for c in range(D_FF // FF_CHUNK):
        cols = slice(c * FF_CHUNK, (c + 1) * FF_CHUNK)
        hdn = _silu(_dot(xb, wg_ref[:, cols])) * _dot(xb, wu_ref[:, cols])
        f = f + _dot(hdn.astype(BF16), wd_ref[cols, :])
    out_ref[...] = _layer_norm(DN_ALPHA * x + f, g_ref[...], b_ref[...])


def _dense_ffn(x2, wg, wu, wd, g, b):
    n = x2.shape[0]
    tm = min(TOK_TILE, n)
    tok = pl.BlockSpec((tm, D_MODEL), lambda i: (i, 0))
    full = lambda shape: pl.BlockSpec(shape, lambda i: (0,) * len(shape))
    return pl.pallas_call(
        _dense_ffn_kernel,
        grid=(n // tm,),
        in_specs=[tok, full((D_MODEL, D_FF)), full((D_MODEL, D_FF)), full((D_FF, D_MODEL)),
                  full((1, D_MODEL)), full((1, D_MODEL))],
        out_specs=tok,
        out_shape=jax.ShapeDtypeStruct((n, D_MODEL), F32),
        compiler_params=_cparams(("parallel",)),
        name="dense_ffn_ln",
    )(x2, wg, wu, wd, g.reshape(1, D_MODEL), b.reshape(1, D_MODEL))


def _router_kernel(x_ref, rw_ref, e_ref, g_ref):
    logits = lax.dot_general(rw_ref[...], x_ref[...], NT_DIMS, preferred_element_type=F32,
                             precision=lax.Precision.HIGHEST)
    eidx = lax.broadcasted_iota(jnp.int32, logits.shape, 0)
    m1 = jnp.max(logits, axis=0, keepdims=True)
    i1 = jnp.min(jnp.where(logits == m1, eidx, N_EXPERTS), axis=0, keepdims=True)
    rest = jnp.where(eidx == i1, -jnp.inf, logits)
    m2 = jnp.max(rest, axis=0, keepdims=True)
    i2 = jnp.min(jnp.where(rest == m2, eidx, N_EXPERTS), axis=0, keepdims=True)
    e2 = jnp.exp(m2 - m1)
    den = 1.0 + e2
    e_ref[0:1, :] = i1
    e_ref[1:2, :] = i2
    g_ref[0:1, :] = 1.0 / den
    g_ref[1:2, :] = e2 / den


def _router(x2, rw_t):
    n = x2.shape[0]
    tm = min(TOK_TILE, n)
    return pl.pallas_call(
        _router_kernel,
        grid=(n // tm,),
        in_specs=[pl.BlockSpec((tm, D_MODEL), lambda i: (i, 0)), pl.BlockSpec((N_EXPERTS, D_MODEL), lambda i: (0, 0))],
        out_specs=[pl.BlockSpec((TOP_K, tm), lambda i: (0, i)), pl.BlockSpec((TOP_K, tm), lambda i: (0, i))],
        out_shape=[jax.ShapeDtypeStruct((TOP_K, n), jnp.int32), jax.ShapeDtypeStruct((TOP_K, n), F32)],
        compiler_params=_cparams(("parallel",)),
        name="router",
    )(x2, rw_t)


def _row_copy(src, src_row, dst, dst_row, sem):
    return pltpu.make_async_copy(src.at[pl.ds(src_row, 1)], dst.at[pl.ds(dst_row, 1)], sem)


def _dispatch_kernel(d0_ref, d1_ref, x_ref, xs_init_hbm, xs_hbm, sem, *, tile):
    del xs_init_hbm

    def start(r, c):
        _row_copy(x_ref, r, xs_hbm, d0_ref[0, r], sem).start()
        _row_copy(x_ref, r, xs_hbm, d1_ref[0, r], sem).start()
        return c
    lax.fori_loop(0, tile, start, 0, unroll=DMA_UNROLL)

    def wait(r, c):
        _row_copy(x_ref, r, xs_hbm, 0, sem).wait()
        _row_copy(x_ref, r, xs_hbm, 0, sem).wait()
        return c
    lax.fori_loop(0, tile, wait, 0, unroll=DMA_UNROLL)


def _dispatch(x2, pos, total):
    n = x2.shape[0]
    tile = min(COMB_TILE, n)
    nb = n // tile
    smem = pl.BlockSpec((None, 1, tile), lambda i: (i, 0, 0), memory_space=pltpu.SMEM)
    return pl.pallas_call(
        functools.partial(_dispatch_kernel, tile=tile),
        grid=(nb,),
        in_specs=[smem, smem, pl.BlockSpec((tile, D_MODEL), lambda i: (i, 0)), pl.BlockSpec(memory_space=pl.ANY)],
        out_specs=pl.BlockSpec(memory_space=pl.ANY),
        out_shape=jax.ShapeDtypeStruct((total, D_MODEL), F32),
        scratch_shapes=[pltpu.SemaphoreType.DMA(())],
        input_output_aliases={3: 0},
        compiler_params=_cparams(("arbitrary",)),
        name="moe_dispatch",
    )(pos[:, 0].reshape(nb, 1, tile), pos[:, 1].reshape(nb, 1, tile), x2, jnp.zeros((total, D_MODEL), F32))


def _expert_kernel(blk_e_ref, nused_ref, xs_ref, wg_ref, wu_ref, wd_ref, y_ref, xb, acc):
    i = pl.program_id(0)
    f = pl.program_id(1)
    used = i < nused_ref[0]
    last = f == pl.num_programs(1) - 1

    @pl.when(jnp.logical_and(used, f == 0))
    def _():
        xb[...] = xs_ref[...].astype(BF16)
        acc[...] = jnp.zeros_like(acc)

    @pl.when(used)
    def _():
        x = xb[...]
        hdn = _silu(_dot(x, wg_ref[...])) * _dot(x, wu_ref[...])
        acc[...] += _dot(hdn.astype(BF16), wd_ref[...])

    @pl.when(jnp.logical_and(used, last))
    def _():
        y_ref[...] = acc[...]

    @pl.when(jnp.logical_and(jnp.logical_not(used), last))
    def _():
        y_ref[...] = jnp.zeros_like(y_ref)


def _experts(xs, blk_e, nused, wg, wu, wd):
    total = xs.shape[0]
    tile = MOE_TILE
    nblk = total // tile
    nf = D_FF_EXPERT // MOE_FT
    grid_spec = pltpu.PrefetchScalarGridSpec(
        num_scalar_prefetch=2,
        grid=(nblk, nf),
        in_specs=[pl.BlockSpec((tile, D_MODEL), lambda i, f, be, nu: (i, 0)),
                  pl.BlockSpec((None, D_MODEL, MOE_FT), lambda i, f, be, nu: (be[i], 0, f)),
                  pl.BlockSpec((None, D_MODEL, MOE_FT), lambda i, f, be, nu: (be[i], 0, f)),
                  pl.BlockSpec((None, MOE_FT, D_MODEL), lambda i, f, be, nu: (be[i], f, 0))],
        out_specs=pl.BlockSpec((tile, D_MODEL), lambda i, f, be, nu: (i, 0)),
        scratch_shapes=[pltpu.VMEM((tile, D_MODEL), BF16), pltpu.VMEM((tile, D_MODEL), F32)],
    )
    return pl.pallas_call(
        _expert_kernel,
        grid_spec=grid_spec,
        out_shape=jax.ShapeDtypeStruct((total, D_MODEL), F32),
        compiler_params=_cparams(("arbitrary", "arbitrary")),
        name="experts",
    )(blk_e, nused, xs, wg, wu, wd)


def _combine_kernel(p0_ref, p1_ref, y_hbm, x_ref, gate_ref, g_ref, b_ref, out_ref, buf0, buf1, sem, *, tile):
    def start(r, c):
        _row_copy(y_hbm, p0_ref[0, r], buf0, r, sem).start()
        _row_copy(y_hbm, p1_ref[0, r], buf1, r, sem).start()
        return c
    lax.fori_loop(0, tile, start, 0, unroll=DMA_UNROLL)

    def wait(r, c):
        _row_copy(y_hbm, 0, buf0, r, sem).wait()
        _row_copy(y_hbm, 0, buf1, r, sem).wait()
        return c
    lax.fori_loop(0, tile, wait, 0, unroll=DMA_UNROLL)
    f = buf0[...] * gate_ref[:, 0:1] + buf1[...] * gate_ref[:, 1:2]
    out_ref[...] = _layer_norm(DN_ALPHA * x_ref[...] + f, g_ref[...], b_ref[...])


def _combine(y, pos, gates, x2, g, b):
    n = x2.shape[0]
    tile = min(COMB_TILE, n)
    nb = n // tile
    smem = pl.BlockSpec((None, 1, tile), lambda i: (i, 0, 0), memory_space=pltpu.SMEM)
    full = pl.BlockSpec((1, D_MODEL), lambda i: (0, 0))
    return pl.pallas_call(
        functools.partial(_combine_kernel, tile=tile),
        grid=(nb,),
        in_specs=[smem, smem, pl.BlockSpec(memory_space=pl.ANY),
                  pl.BlockSpec((tile, D_MODEL), lambda i: (i, 0)),
                  pl.BlockSpec((tile, TOP_K), lambda i: (i, 0)), full, full],
        out_specs=pl.BlockSpec((tile, D_MODEL), lambda i: (i, 0)),
        out_shape=jax.ShapeDtypeStruct((n, D_MODEL), F32),
        scratch_shapes=[pltpu.VMEM((tile, D_MODEL), F32), pltpu.VMEM((tile, D_MODEL), F32),
                        pltpu.SemaphoreType.DMA(())],
        compiler_params=_cparams(("arbitrary",)),
        name="moe_combine_ln",
    )(pos[:, 0].reshape(nb, 1, tile), pos[:, 1].reshape(nb, 1, tile), y, x2, gates,
      g.reshape(1, D_MODEL), b.reshape(1, D_MODEL))


def _moe(x2, router_w, wg, wu, wd, g, b):
    n = x2.shape[0]
    top_e, gates = _router(x2, router_w.T)
    flat_e = top_e.T.reshape(-1)
    onehot = (flat_e[:, None] == jnp.arange(N_EXPERTS, dtype=jnp.int32)[None, :]).astype(jnp.int32)
    rank = jnp.sum((jnp.cumsum(onehot, axis=0) - onehot) * onehot, axis=1)
    counts = jnp.sum(onehot, axis=0)
    padded = ((counts + MOE_TILE - 1) // MOE_TILE) * MOE_TILE
    pad_end = jnp.cumsum(padded)
    pad_start = pad_end - padded
    pos = (pad_start[flat_e] + rank).astype(jnp.int32).reshape(n, TOP_K)
    total = n * TOP_K + N_EXPERTS * MOE_TILE
    nblk = total // MOE_TILE
    blk_e = jnp.minimum(jnp.searchsorted(pad_end, jnp.arange(nblk, dtype=jnp.int32) * MOE_TILE, side='right'),
                        N_EXPERTS - 1).astype(jnp.int32)
    nused = (pad_end[-1:] // MOE_TILE).astype(jnp.int32)
    xs = _dispatch(x2, pos, total)
    y = _experts(xs, blk_e, nused, wg, wu, wd)
    return _combine(y, pos, gates.T, x2, g, b)


def _pad_head_proj(w_uk, w_uv):
    H, DH, DC = DSA_HEADS, DSA_HEAD_DIM, DSA_LATENT
    eye = jnp.eye(H, dtype=w_uk.dtype)
    wuk_pad = jnp.einsum('hg,hdc->hgdc', eye, w_uk).reshape(H, H * DH, DC)
    wuv_pad = jnp.einsum('hg,hcd->hcgd', eye, w_uv).reshape(H, DC, H * DH)
    return wuk_pad.astype(BF16), wuv_pad.astype(BF16)


def kernel(x, w_in, mlstm_conv_w, mlstm_gate_bias, mlstm_norm_g, dsa_kv_norm_g, dsa_w_uk, dsa_w_uv, rel_bias,
           w_branch_a, w_branch_b, w_out, ln_g, ln_b, dense_w_gate, dense_w_up, dense_w_down, router_w,
           expert_w_gate, expert_w_up, expert_w_down):
    bsz, seq, d = x.shape
    n = bsz * seq
    x2 = x.reshape(n, d)
    tab = _near_bias_tables(rel_bias, *_dsa_tiles(seq))
    for l in range(DEPTH):
        p = dict(zip([nm for nm, _, _ in IN_PROJ_OUTS], _in_proj(x2, _regroup_in_proj(w_in[l]))))
        ya = _mlstm(p["qk"], p["av"], p["aif"], p["ao"], mlstm_conv_w[l], mlstm_gate_bias[l], mlstm_norm_g[l],
                    bsz, seq)
        ckv = _kv_norm(p["bc"], dsa_kv_norm_g[l])
        wuk_pad, wuv_pad = _pad_head_proj(dsa_w_uk[l], dsa_w_uv[l])
        yb = _dsa(p["bq"], p["iq"], p["ikw"], ckv, tab, wuk_pad, wuv_pad, bsz, seq)
        x2 = _merge(x2, ya, yb, p["ga"], p["gb"], w_branch_a[l].astype(BF16), w_branch_b[l].astype(BF16),
                    w_out[l].astype(BF16), ln_g[l, 0], ln_b[l, 0])
        j = l // 2
        if l % 2 == 0:
            x2 = _dense_ffn(x2, dense_w_gate[j].astype(BF16), dense_w_up[j].astype(BF16),
                            dense_w_down[j].astype(BF16), ln_g[l, 1], ln_b[l, 1])
        else:
            x2 = _moe(x2, router_w[j], expert_w_gate[j].astype(BF16), expert_w_up[j].astype(BF16),
                      expert_w_down[j].astype(BF16), ln_g[l, 1], ln_b[l, 1])
    return x2.reshape(bsz, seq, d)
```

```python
import functools
import math

import numpy as np
import jax
import jax.numpy as jnp
from jax import lax
from jax.experimental import pallas as pl
from jax.experimental.pallas import tpu as pltpu

D_MODEL = 1024
DEPTH = 4
MLSTM_HEADS = 4
MLSTM_QK_DIM = 64
MLSTM_V_DIM = 128
MLSTM_CONV = 4
DSA_HEADS = 8
DSA_HEAD_DIM = 64
DSA_LATENT = 128
IDX_HEADS = 8
IDX_DIM = 32
IDX_TOPK_MAX = 256
REL_BUCKETS = 32
REL_MAX_DIST = 128
D_FF = 2816
N_EXPERTS = 8
TOP_K = 2
D_FF_EXPERT = 3584
DN_ALPHA = (2 * DEPTH) ** 0.25
LN_EPS = 1e-5

A_QK = 2 * MLSTM_HEADS * MLSTM_QK_DIM
A_V = MLSTM_HEADS * MLSTM_V_DIM
A_GATE = 2 * MLSTM_HEADS
B_Q = DSA_HEADS * DSA_HEAD_DIM
I_Q = IDX_HEADS * IDX_DIM
PROJ_SIZES = (A_QK, A_V, A_V, A_GATE, B_Q, DSA_LATENT, I_Q, IDX_DIM, IDX_HEADS, D_MODEL, D_MODEL)

LANES = 128
SUBLANES = 8
VMEM_LIMIT = 56 * 1024 * 1024
TOK_TILE = 512
MLSTM_TILE = 256
DSA_TQ = 128
DSA_KN = 256
DSA_KF = 512
MOE_TILE = 512
MOE_FT = 512
COMB_TILE = 256
DMA_UNROLL = 8
FF_CHUNK = 256
INT_MIN = -(2 ** 31)
NEG_BIG = -1e30
M_FLOOR = -1e29
LOG2E = math.log2(math.e)
F32_MIN_NORMAL = 2.0 ** -126
BF16_MIN_NORMAL_BITS = 0x0080

BF16 = jnp.bfloat16
F32 = jnp.float32
NT_DIMS = (((1,), (1,)), ((), ()))


def _cparams(sem):
    return pltpu.CompilerParams(dimension_semantics=sem, vmem_limit_bytes=VMEM_LIMIT)


def _dot(a, b):
    return jnp.dot(a, b, preferred_element_type=F32)


def _dot_nt(a, b):
    return lax.dot_general(a, b, NT_DIMS, preferred_element_type=F32)


def _layer_norm(r, g, b):
    mu = jnp.mean(r, axis=-1, keepdims=True)
    d = r - mu
    var = jnp.mean(d * d, axis=-1, keepdims=True)
    return d * lax.rsqrt(var + LN_EPS) * g + b


def _silu(x):
    return x * jax.nn.sigmoid(x)


IN_PROJ_OUTS = (("qk", A_QK, F32), ("av", A_V, BF16), ("ao", A_V, BF16), ("bq", B_Q, BF16),
                ("iq", I_Q, BF16), ("ga", D_MODEL, BF16), ("gb", D_MODEL, BF16),
                ("aif", LANES, F32), ("bc", LANES, F32), ("ikw", LANES, F32))
IN_PROJ_COLS = sum(w for _, w, _ in IN_PROJ_OUTS)
IKW_W_OFF = IDX_DIM


def _regroup_in_proj(w):
    offs = np.concatenate([[0], np.cumsum(PROJ_SIZES)])
    a_qk, a_v, a_o, a_if, b_q, b_c, i_q, i_k, i_w, g_a, g_b = (w[:, offs[i]:offs[i + 1]] for i in range(11))
    pad = lambda a: jnp.pad(a, ((0, 0), (0, LANES - a.shape[1])))
    groups = [a_qk, a_v, a_o, b_q, i_q, g_a, g_b, pad(a_if), b_c, pad(jnp.concatenate([i_k, i_w], axis=1))]
    return jnp.concatenate(groups, axis=1).astype(BF16)


def _in_proj_kernel(x_ref, w_ref, *out_refs):
    xb = x_ref[...].astype(BF16)
    off = 0
    for (_, width, dtype), o_ref in zip(IN_PROJ_OUTS, out_refs):
        o_ref[...] = _dot(xb, w_ref[:, off:off + width]).astype(dtype)
        off += width


def _in_proj(x2, w):
    n = x2.shape[0]
    tm = min(TOK_TILE, n)
    return pl.pallas_call(
        _in_proj_kernel,
        grid=(n // tm,),
        in_specs=[pl.BlockSpec((tm, D_MODEL), lambda i: (i, 0)),
                  pl.BlockSpec((D_MODEL, IN_PROJ_COLS), lambda i: (0, 0))],
        out_specs=[pl.BlockSpec((tm, wd), lambda i: (i, 0)) for _, wd, _ in IN_PROJ_OUTS],
        out_shape=[jax.ShapeDtypeStruct((n, wd), dt) for _, wd, dt in IN_PROJ_OUTS],
        compiler_params=_cparams(("parallel",)),
        name="in_proj",
    )(x2, w)


def _log_sigmoid(x):
    return jnp.minimum(x, 0.0) - jnp.log1p(jnp.exp(-jnp.abs(x)))


def _mlstm_kernel(qk_ref, av_ref, aif_ref, ao_ref, convw_ref, gbias_ref, ng_ref, out_ref,
                  ext_ref, ct_ref, m_ref, *, tile):
    L = tile
    H, DK, DV = MLSTM_HEADS, MLSTM_QK_DIM, MLSTM_V_DIM

    @pl.when(pl.program_id(1) == 0)
    def _():
        ext_ref[0:SUBLANES, :] = jnp.zeros((SUBLANES, A_QK), F32)
        ct_ref[...] = jnp.zeros_like(ct_ref)
        m_ref[...] = jnp.zeros_like(m_ref)

    u = qk_ref[...]
    ext_ref[SUBLANES:SUBLANES + L, :] = u
    conv = jnp.zeros((L, A_QK), F32)
    for j in range(MLSTM_CONV):
        conv = conv + ext_ref[pl.ds(SUBLANES - (MLSTM_CONV - 1) + j, L), :] * convw_ref[j:j + 1, :]
    ext_ref[0:SUBLANES, :] = u[L - SUBLANES:L, :]
    qk = _silu(conv)

    g_col = aif_ref[...] + gbias_ref[...]
    g_row = g_col.T
    row = lax.broadcasted_iota(jnp.int32, (L, L), 0)
    col = lax.broadcasted_iota(jnp.int32, (L, L), 1)
    causal = row >= col
    tril = jnp.where(causal, 1.0, 0.0).astype(F32)
    triu = jnp.where(row <= col, 1.0, 0.0).astype(F32)
    b_col = jnp.dot(tril, _log_sigmoid(g_col), preferred_element_type=F32, precision=lax.Precision.HIGHEST)
    b_row = jnp.dot(_log_sigmoid(g_row), triu, preferred_element_type=F32, precision=lax.Precision.HIGHEST)

    ones_v = jnp.ones((L, DV), BF16)
    ones_t = jnp.ones((DV, L), F32)
    for h in range(H):
        bcol = b_col[:, H + h:H + h + 1]
        icol = g_col[:, h:h + 1]
        brow = b_row[H + h:H + h + 1, :]
        irow = g_row[h:h + 1, :]
        mprev = m_ref[h:h + 1, 0:1]
        dlog = jnp.where(causal, bcol - brow + irow, -jnp.inf)
        m_t = jnp.maximum(bcol + mprev, jnp.max(dlog, axis=1, keepdims=True))
        q = qk[:, h * DK:(h + 1) * DK].astype(BF16)
        k = qk[:, H * DK + h * DK:H * DK + (h + 1) * DK] * (DK ** -0.5)
        s_w = _dot_nt(q, k.astype(BF16)) * jnp.exp(dlog - m_t)
        inter = jnp.exp(bcol + mprev - m_t)
        v_h = av_ref[:, h * DV:(h + 1) * DV]
        v_ext = jnp.concatenate([v_h, ones_v], axis=1)
        ct_st = ct_ref[h]
        ne = _dot(s_w.astype(BF16), v_ext) + inter * _dot_nt(q, ct_st.astype(BF16))
        h_out = ne[:, :DV] / jnp.maximum(jnp.abs(ne[:, DV:]), jnp.exp(-m_t))
        blast = bcol[L - 1:L, :]
        glog = blast - bcol + icol
        m_new = jnp.maximum(blast + mprev, jnp.max(glog, axis=0, keepdims=True))
        decay = jnp.exp(blast + mprev - m_new)
        w_k = k * jnp.exp(glog - m_new)
        v_ext_t = jnp.concatenate([v_h.astype(F32).T, ones_t], axis=0).astype(BF16)
        ct_ref[h] = decay * ct_st + _dot(v_ext_t, w_k.astype(BF16))
        m_ref[h:h + 1, :] = jnp.broadcast_to(m_new, (1, LANES))
        mu = jnp.mean(h_out, axis=-1, keepdims=True)
        d = h_out - mu
        hn = d * lax.rsqrt(jnp.mean(d * d, axis=-1, keepdims=True) + LN_EPS)
        y = hn * ng_ref[:, h * DV:(h + 1) * DV] * jax.nn.sigmoid(ao_ref[:, h * DV:(h + 1) * DV].astype(F32))
        out_ref[:, h * DV:(h + 1) * DV] = y.astype(BF16)


def _mlstm(qk, av, aif, ao, conv_w, gate_bias, norm_g, bsz, seq):
    tile = min(MLSTM_TILE, seq)
    nc = seq // tile
    gbias = jnp.zeros((1, LANES), F32).at[0, :A_GATE].set(gate_bias.reshape(-1))
    tok = lambda w: pl.BlockSpec((tile, w), lambda b, c: (b * nc + c, 0))
    full = lambda shape: pl.BlockSpec(shape, lambda b, c: (0,) * len(shape))
    return pl.pallas_call(
        functools.partial(_mlstm_kernel, tile=tile),
        grid=(bsz, nc),
        in_specs=[tok(A_QK), tok(A_V), tok(LANES), tok(A_V),
                  full((MLSTM_CONV, A_QK)), full((1, LANES)), full((1, A_V))],
        out_specs=tok(A_V),
        out_shape=jax.ShapeDtypeStruct((bsz * seq, A_V), BF16),
        scratch_shapes=[pltpu.VMEM((tile + SUBLANES, A_QK), F32),
                        pltpu.VMEM((MLSTM_HEADS, 2 * MLSTM_V_DIM, MLSTM_QK_DIM), F32),
                        pltpu.VMEM((SUBLANES, LANES), F32)],
        compiler_params=_cparams(("parallel", "arbitrary")),
        name="mlstm",
    )(qk, av, aif, ao, conv_w, gbias, norm_g.reshape(1, A_V))


def _t5_bucket_np(dist):
    dist = np.maximum(dist, 0)
    exact = REL_BUCKETS // 2
    log_ratio = (np.log(np.maximum(dist, 1).astype(np.float32) / np.float32(exact))
                 / np.float32(math.log(REL_MAX_DIST / exact))).astype(np.float32)
    large = np.minimum(exact + (log_ratio * np.float32(REL_BUCKETS - exact)).astype(np.int32), REL_BUCKETS - 1)
    return np.where(dist < exact, dist, large)


def _near_bias_tables(rel_bias, tq, kn, kf):
    n_off = kf // tq + 1
    t = np.arange(tq)[None, :, None]
    s = np.arange(kn)[None, None, :]
    off = (np.arange(n_off) * tq)[:, None, None]
    dist = off + t - s
    bucket = _t5_bucket_np(dist)
    assert np.all(_t5_bucket_np(np.arange(tq + 1, 4 * REL_MAX_DIST)) == REL_BUCKETS - 1)
    tab = rel_bias[bucket] - rel_bias[REL_BUCKETS - 1]
    tab = jnp.where((dist >= 0)[..., None], tab, 0.0) * LOG2E
    return tab.transpose(0, 3, 1, 2).reshape(n_off, DSA_HEADS * tq, kn).astype(F32)


def _dsa_kernel(bq_ref, iq_ref, ikwq_ref, ikw_ref, ckv_ref, tab_ref, wuk_ref, wuv_ref, out_ref,
                key_ref, stage_ref, acc_ref, m_ref, ql_ref, wb_ref, *, tq, kn, kf, topk):
    H = DSA_HEADS
    j = pl.program_id(1)
    t0 = j * tq
    n_score = (t0 + tq - 1) // kf + 1
    n_far = jnp.maximum(t0 - tq, 0) // kf
    near_lo = n_far * (kf // kn)
    near_hi = (t0 + tq - 1) // kn + 1

    bq = bq_ref[...]
    for h in range(H):
        ql_ref[h * tq:(h + 1) * tq, :] = (_dot(bq, wuk_ref[h]) * (DSA_HEAD_DIM ** -0.5 * LOG2E)).astype(BF16)
    w_idx = ikwq_ref[:, IKW_W_OFF:IKW_W_OFF + IDX_HEADS] * ((IDX_HEADS * IDX_DIM) ** -0.5)
    for h in range(IDX_HEADS):
        wb_ref[h] = jnp.broadcast_to(w_idx[:, h:h + 1], (tq, kf))
    iq = iq_ref[...]
    qis = [iq[:, h * IDX_DIM:(h + 1) * IDX_DIM] for h in range(IDX_HEADS)]
    tpos = t0 + lax.broadcasted_iota(jnp.int32, (tq, kf), 0)
    lpos = lax.broadcasted_iota(jnp.int32, (tq, kf), 1)

    ncol_f = kf // LANES
    tile_f = lambda a: jnp.concatenate([a] * ncol_f, axis=1)

    def score_chunk(c, masked):
        s0 = pl.multiple_of(c * kf, kf)
        kidx = ikw_ref[pl.ds(s0, kf), :][:, 0:IDX_DIM].astype(BF16)
        sc = jnp.zeros((tq, kf), F32)
        for h in range(IDX_HEADS):
            sc = sc + wb_ref[h] * jnp.maximum(_dot_nt(qis[h], kidx), 0.0)
        sc = jnp.where(jnp.abs(sc) < F32_MIN_NORMAL, 0.0, sc)
        bits = lax.bitcast_convert_type(sc, jnp.int32)
        key = bits ^ ((bits >> 31) & 0x7FFFFFFF)
        top = lax.bitcast_convert_type(bits & jnp.int32(-65536), F32)
        if masked:
            causal = s0 + lpos <= tpos
            key = jnp.where(causal, key, INT_MIN)
            top = jnp.where(causal, top, -jnp.inf)
        key_ref[:, pl.ds(s0, kf)] = key
        stage_ref[:, pl.ds(s0, kf)] = top.astype(BF16)

    def score_pair(i, carry):
        score_chunk(2 * i, False)
        score_chunk(2 * i + 1, False)
        return carry

    lax.fori_loop(0, (n_score - 1) // 2, score_pair, 0)

    @pl.when((n_score - 1) % 2 == 1)
    def _():
        score_chunk(n_score - 2, False)

    score_chunk(n_score - 1, True)

    one_b = jnp.ones((tq, kf), BF16)
    zero_b = jnp.zeros((tq, kf), BF16)

    def count_stage(cand):
        cand_w = tile_f(cand.astype(BF16))

        def body(c, acc):
            s0 = pl.multiple_of(c * kf, kf)
            ind = jnp.where(stage_ref[:, pl.ds(s0, kf)] >= cand_w, one_b, zero_b)
            for g in range(ncol_f):
                acc = acc + ind[:, g * LANES:(g + 1) * LANES]
            return acc
        acc = lax.fori_loop(0, n_score, body, jnp.zeros((tq, LANES), BF16)).astype(F32)
        return jnp.sum(acc, axis=1, keepdims=True)

    def rewrite_stage(value_of_key):
        def body(c, carry):
            s0 = pl.multiple_of(c * kf, kf)
            stage_ref[:, pl.ds(s0, kf)] = value_of_key(key_ref[:, pl.ds(s0, kf)]).astype(BF16)
            return carry
        lax.fori_loop(0, n_score, body, 0)

    def search(nbits, start, to_cand):
        def step(i, t):
            cand = t + lax.shift_left(jnp.int32(1), nbits - 1 - i)
            return jnp.where(count_stage(to_cand(cand)) >= topk, cand, t)
        return lax.fori_loop(0, nbits, step, start)

    def top16_value(k):
        k = jnp.where(jnp.logical_and(k > 0, k < BF16_MIN_NORMAL_BITS), BF16_MIN_NORMAL_BITS, k)
        pat = (k ^ ((k >> 31) & 0x7FFF)) & 0xFFFF
        return lax.bitcast_convert_type(lax.shift_left(pat, 16), F32)

    lowest = jnp.full((tq, LANES), -32768, jnp.int32)
    zero_i = jnp.zeros((tq, LANES), jnp.int32)
    t_a = jnp.where(count_stage(top16_value(zero_i)) >= topk, zero_i, lowest)
    t_a = search(15, t_a, top16_value)
    enough = t_a > lowest
    t_a = jnp.where(enough, t_a, zero_i)

    base_b = tile_f(lax.shift_left(t_a, 8))
    rewrite_stage(lambda key: ((key >> 8) - base_b).astype(F32))
    t_b = search(8, zero_i, lambda k: k.astype(F32))

    base_c = tile_f(lax.shift_left(t_a, 8) | t_b)
    rewrite_stage(lambda key: ((key >> 8) - base_c).astype(F32) * 256.0 + (key & 0xFF).astype(F32))
    t_c = search(8, zero_i, lambda k: k.astype(F32))

    thr = jnp.where(enough, lax.shift_left(t_a, 16) | lax.shift_left(t_b, 8) | t_c, INT_MIN + 1)

    def count_key(pred):
        def body(c, acc):
            s0 = pl.multiple_of(c * kf, kf)
            ind = jnp.where(pred(key_ref[:, pl.ds(s0, kf)]), 1.0, 0.0)
            for g in range(ncol_f):
                acc = acc + ind[:, g * LANES:(g + 1) * LANES]
            return acc
        acc = lax.fori_loop(0, n_score, body, jnp.zeros((tq, LANES), F32))
        return jnp.sum(acc, axis=1, keepdims=True)

    thr_f = tile_f(thr)
    surplus = jnp.where(enough[:, 0:1], count_key(lambda k: k >= thr_f) - topk, 0.0)

    @pl.when(jnp.max(surplus) > 0.0)
    def _():
        allowed = topk - count_key(lambda k: k > thr_f)
        r = lax.broadcasted_iota(jnp.int32, (kf, kf), 0)
        cidx = lax.broadcasted_iota(jnp.int32, (kf, kf), 1)
        upper = jnp.where(r <= cidx, 1.0, 0.0).astype(BF16)

        def body(c, seen):
            s0 = pl.multiple_of(c * kf, kf)
            key = key_ref[:, pl.ds(s0, kf)]
            tie = key == thr_f
            tie_f = jnp.where(tie, 1.0, 0.0)
            rank = _dot(tie_f.astype(BF16), upper) + seen
            key_ref[:, pl.ds(s0, kf)] = jnp.where(jnp.logical_and(tie, rank > allowed), thr_f - 1, key)
            return seen + jnp.sum(tie_f, axis=1, keepdims=True)
        lax.fori_loop(0, n_score, body, jnp.zeros((tq, 1), F32))

    m_ref[...] = jnp.full(m_ref.shape, M_FLOOR, F32)
    acc_ref[...] = jnp.zeros_like(acc_ref)

    def attend(s0, width, table_idx):
        ncol = width // LANES
        kv = ckv_ref[pl.ds(s0, width), :]
        kv_ext = jnp.concatenate([kv, jnp.ones((width, DSA_LATENT), BF16)], axis=1)
        thr_w = jnp.concatenate([thr] * ncol, axis=1)
        madd = jnp.where(key_ref[:, pl.ds(s0, width)] >= thr_w, 0.0, NEG_BIG)
        for h in range(H):
            rows = slice(h * tq, (h + 1) * tq)
            x = _dot_nt(ql_ref[rows, :], kv) + madd
            if table_idx is not None:
                x = x + tab_ref[table_idx, rows, :]
            m_old = m_ref[rows, :]
            m_new = jnp.maximum(m_old, jnp.max(x, axis=1, keepdims=True))
            alpha = jnp.exp2(m_old - m_new)
            p = jnp.exp2(x - jnp.concatenate([m_new] * ncol, axis=1))
            acc_ref[rows, :] = jnp.concatenate([alpha, alpha], axis=1) * acc_ref[rows, :] + _dot(p.astype(BF16), kv_ext)
            m_ref[rows, :] = m_new

    def far_pair(i, carry):
        attend(pl.multiple_of(2 * i * kf, kf), kf, None)
        attend(pl.multiple_of((2 * i + 1) * kf, kf), kf, None)
        return carry

    def near_chunk(c, carry):
        s0 = pl.multiple_of(c * kn, kn)
        attend(s0, kn, (t0 - s0) // tq)
        return carry

    lax.fori_loop(0, n_far // 2, far_pair, 0)

    @pl.when(n_far % 2 == 1)
    def _():
        attend(pl.multiple_of((n_far - 1) * kf, kf), kf, None)

    lax.fori_loop(near_lo, near_hi, near_chunk, 0)

    y = jnp.zeros((tq, B_Q), F32)
    for h in range(H):
        a = acc_ref[h * tq:(h + 1) * tq, :]
        o = a[:, :DSA_LATENT] / a[:, DSA_LATENT:]
        y = y + _dot(o.astype(BF16), wuv_ref[h])
    out_ref[...] = y.astype(BF16)


def _dsa_tiles(seq):
    tq = min(DSA_TQ, seq)
    return tq, min(DSA_KN, seq), min(DSA_KF, seq)


def _dsa(bq, iq, ikw, ckv, tab, wuk_pad, wuv_pad, bsz, seq):
    tq, kn, kf = _dsa_tiles(seq)
    nq = seq // tq
    topk = min(IDX_TOPK_MAX, seq // 4)
    n_off = kf // tq + 1
    qblk = lambda w: pl.BlockSpec((tq, w), lambda b, j: (b * nq + j, 0))
    seqblk = lambda w: pl.BlockSpec((seq, w), lambda b, j: (b, 0))
    full = lambda shape: pl.BlockSpec(shape, lambda b, j: (0,) * len(shape))
    return pl.pallas_call(
        functools.partial(_dsa_kernel, tq=tq, kn=kn, kf=kf, topk=topk),
        grid=(bsz, nq),
        in_specs=[qblk(B_Q), qblk(I_Q), qblk(LANES), seqblk(LANES), seqblk(DSA_LATENT),
                  full((n_off, DSA_HEADS * tq, kn)), full((DSA_HEADS, B_Q, DSA_LATENT)),
                  full((DSA_HEADS, DSA_LATENT, B_Q))],
        out_specs=qblk(B_Q),
        out_shape=jax.ShapeDtypeStruct((bsz * seq, B_Q), BF16),
        scratch_shapes=[pltpu.VMEM((tq, seq), jnp.int32),
                        pltpu.VMEM((tq, seq), BF16),
                        pltpu.VMEM((DSA_HEADS * tq, 2 * DSA_LATENT), F32),
                        pltpu.VMEM((DSA_HEADS * tq, LANES), F32),
                        pltpu.VMEM((DSA_HEADS * tq, DSA_LATENT), BF16),
                        pltpu.VMEM((IDX_HEADS, tq, kf), F32)],
        compiler_params=_cparams(("parallel", "arbitrary")),
        name="dsa",
    )(bq, iq, ikw, ikw, ckv, tab, wuk_pad, wuv_pad)


def _kv_norm_kernel(bc_ref, g_ref, out_ref):
    x = bc_ref[...]
    out_ref[...] = (x * lax.rsqrt(jnp.mean(x * x, axis=-1, keepdims=True) + LN_EPS) * g_ref[...]).astype(BF16)


def _kv_norm(bc, g):
    n = bc.shape[0]
    tm = min(4 * TOK_TILE, n)
    return pl.pallas_call(
        _kv_norm_kernel,
        grid=(n // tm,),
        in_specs=[pl.BlockSpec((tm, DSA_LATENT), lambda i: (i, 0)), pl.BlockSpec((1, DSA_LATENT), lambda i: (0, 0))],
        out_specs=pl.BlockSpec((tm, DSA_LATENT), lambda i: (i, 0)),
        out_shape=jax.ShapeDtypeStruct((n, DSA_LATENT), BF16),
        compiler_params=_cparams(("parallel",)),
        name="kv_norm",
    )(bc, g.reshape(1, DSA_LATENT))


def _merge_kernel(x_ref, ya_ref, yb_ref, ga_ref, gb_ref, wa_ref, wb_ref, wo_ref, g_ref, b_ref, out_ref):
    merged = (jax.nn.sigmoid(ga_ref[...].astype(F32)) * _dot(ya_ref[...], wa_ref[...])
              + jax.nn.sigmoid(gb_ref[...].astype(F32)) * _dot(yb_ref[...], wb_ref[...]))
    mix = _dot(merged.astype(BF16), wo_ref[...])
    out_ref[...] = _layer_norm(DN_ALPHA * x_ref[...] + mix, g_ref[...], b_ref[...])


def _merge(x2, ya, yb, ga, gb, wa, wb, wo, g, b):
    n = x2.shape[0]
    tm = min(TOK_TILE, n)
    tok = lambda w: pl.BlockSpec((tm, w), lambda i: (i, 0))
    full = lambda shape: pl.BlockSpec(shape, lambda i: (0,) * len(shape))
    return pl.pallas_call(
        _merge_kernel,
        grid=(n // tm,),
        in_specs=[tok(D_MODEL), tok(A_V), tok(B_Q), tok(D_MODEL), tok(D_MODEL),
                  full((A_V, D_MODEL)), full((B_Q, D_MODEL)), full((D_MODEL, D_MODEL)),
                  full((1, D_MODEL)), full((1, D_MODEL))],
        out_specs=tok(D_MODEL),
        out_shape=jax.ShapeDtypeStruct((n, D_MODEL), F32),
        compiler_params=_cparams(("parallel",)),
        name="merge_ln",
    )(x2, ya, yb, ga, gb, wa, wb, wo, g.reshape(1, D_MODEL), b.reshape(1, D_MODEL))


def _dense_ffn_kernel(x_ref, wg_ref, wu_ref, wd_ref, g_ref, b_ref, out_ref):
    x = x_ref[...]
    xb = x.astype(BF16)
    f = jnp.zeros(x.shape, F32)
    for c in range(D_FF // FF_CHUNK):
        cols = slice(c * FF_CHUNK, (c + 1) * FF_CHUNK)
        hdn = _silu(_dot(xb, wg_ref[:, cols])) * _dot(xb, wu_ref[:, cols])
        f = f + _dot(hdn.astype(BF16), wd_ref[cols, :])
    out_ref[...] = _layer_norm(DN_ALPHA * x + f, g_ref[...], b_ref[...])


def _dense_ffn(x2, wg, wu, wd, g, b):
    n = x2.shape[0]
    tm = min(TOK_TILE, n)
    tok = pl.BlockSpec((tm, D_MODEL), lambda i: (i, 0))
    full = lambda shape: pl.BlockSpec(shape, lambda i: (0,) * len(shape))
    return pl.pallas_call(
        _dense_ffn_kernel,
        grid=(n // tm,),
        in_specs=[tok, full((D_MODEL, D_FF)), full((D_MODEL, D_FF)), full((D_FF, D_MODEL)),
                  full((1, D_MODEL)), full((1, D_MODEL))],
        out_specs=tok,
        out_shape=jax.ShapeDtypeStruct((n, D_MODEL), F32),
        compiler_params=_cparams(("parallel",)),
        name="dense_ffn_ln",
    )(x2, wg, wu, wd, g.reshape(1, D_MODEL), b.reshape(1, D_MODEL))


def _router_kernel(x_ref, rw_ref, e_ref, g_ref):
    logits = lax.dot_general(rw_ref[...], x_ref[...], NT_DIMS, preferred_element_type=F32,
                             precision=lax.Precision.HIGHEST)
    eidx = lax.broadcasted_iota(jnp.int32, logits.shape, 0)
    m1 = jnp.max(logits, axis=0, keepdims=True)
    i1 = jnp.min(jnp.where(logits == m1, eidx, N_EXPERTS), axis=0, keepdims=True)
    rest = jnp.where(eidx == i1, -jnp.inf, logits)
    m2 = jnp.max(rest, axis=0, keepdims=True)
    i2 = jnp.min(jnp.where(rest == m2, eidx, N_EXPERTS), axis=0, keepdims=True)
    e2 = jnp.exp(m2 - m1)
    den = 1.0 + e2
    e_ref[0:1, :] = i1
    e_ref[1:2, :] = i2
    g_ref[0:1, :] = 1.0 / den
    g_ref[1:2, :] = e2 / den


def _router(x2, rw_t):
    n = x2.shape[0]
    tm = min(TOK_TILE, n)
    return pl.pallas_call(
        _router_kernel,
        grid=(n // tm,),
        in_specs=[pl.BlockSpec((tm, D_MODEL), lambda i: (i, 0)), pl.BlockSpec((N_EXPERTS, D_MODEL), lambda i: (0, 0))],
        out_specs=[pl.BlockSpec((TOP_K, tm), lambda i: (0, i)), pl.BlockSpec((TOP_K, tm), lambda i: (0, i))],
        out_shape=[jax.ShapeDtypeStruct((TOP_K, n), jnp.int32), jax.ShapeDtypeStruct((TOP_K, n), F32)],
        compiler_params=_cparams(("parallel",)),
        name="router",
    )(x2, rw_t)


def _row_copy(src, src_row, dst, dst_row, sem):
    return pltpu.make_async_copy(src.at[pl.ds(src_row, 1)], dst.at[pl.ds(dst_row, 1)], sem)


def _dispatch_kernel(d0_ref, d1_ref, x_ref, xs_init_hbm, xs_hbm, sem, *, tile):
    del xs_init_hbm

    def start(r, c):
        _row_copy(x_ref, r, xs_hbm, d0_ref[0, r], sem).start()
        _row_copy(x_ref, r, xs_hbm, d1_ref[0, r], sem).start()
        return c
    lax.fori_loop(0, tile, start, 0, unroll=DMA_UNROLL)

    def wait(r, c):
        _row_copy(x_ref, r, xs_hbm, 0, sem).wait()
        _row_copy(x_ref, r, xs_hbm, 0, sem).wait()
        return c
    lax.fori_loop(0, tile, wait, 0, unroll=DMA_UNROLL)


def _dispatch(x2, pos, total):
    n = x2.shape[0]
    tile = min(COMB_TILE, n)
    nb = n // tile
    smem = pl.BlockSpec((None, 1, tile), lambda i: (i, 0, 0), memory_space=pltpu.SMEM)
    return pl.pallas_call(
        functools.partial(_dispatch_kernel, tile=tile),
        grid=(nb,),
        in_specs=[smem, smem, pl.BlockSpec((tile, D_MODEL), lambda i: (i, 0)), pl.BlockSpec(memory_space=pl.ANY)],
        out_specs=pl.BlockSpec(memory_space=pl.ANY),
        out_shape=jax.ShapeDtypeStruct((total, D_MODEL), F32),
        scratch_shapes=[pltpu.SemaphoreType.DMA(())],
        input_output_aliases={3: 0},
        compiler_params=_cparams(("arbitrary",)),
        name="moe_dispatch",
    )(pos[:, 0].reshape(nb, 1, tile), pos[:, 1].reshape(nb, 1, tile), x2, jnp.zeros((total, D_MODEL), F32))


def _expert_kernel(blk_e_ref, nused_ref, xs_ref, wg_ref, wu_ref, wd_ref, y_ref, xb, acc):
    i = pl.program_id(0)
    f = pl.program_id(1)
    used = i < nused_ref[0]
    last = f == pl.num_programs(1) - 1

    @pl.when(jnp.logical_and(used, f == 0))
    def _():
        xb[...] = xs_ref[...].astype(BF16)
        acc[...] = jnp.zeros_like(acc)

    @pl.when(used)
    def _():
        x = xb[...]
        hdn = _silu(_dot(x, wg_ref[...])) * _dot(x, wu_ref[...])
        acc[...] += _dot(hdn.astype(BF16), wd_ref[...])

    @pl.when(jnp.logical_and(used, last))
    def _():
        y_ref[...] = acc[...]

    @pl.when(jnp.logical_and(jnp.logical_not(used), last))
    def _():
        y_ref[...] = jnp.zeros_like(y_ref)


def _experts(xs, blk_e, nused, wg, wu, wd):
    total = xs.shape[0]
    tile = MOE_TILE
    nblk = total // tile
    nf = D_FF_EXPERT // MOE_FT
    grid_spec = pltpu.PrefetchScalarGridSpec(
        num_scalar_prefetch=2,
        grid=(nblk, nf),
        in_specs=[pl.BlockSpec((tile, D_MODEL), lambda i, f, be, nu: (i, 0)),
                  pl.BlockSpec((None, D_MODEL, MOE_FT), lambda i, f, be, nu: (be[i], 0, f)),
                  pl.BlockSpec((None, D_MODEL, MOE_FT), lambda i, f, be, nu: (be[i], 0, f)),
                  pl.BlockSpec((None, MOE_FT, D_MODEL), lambda i, f, be, nu: (be[i], f, 0))],
        out_specs=pl.BlockSpec((tile, D_MODEL), lambda i, f, be, nu: (i, 0)),
        scratch_shapes=[pltpu.VMEM((tile, D_MODEL), BF16), pltpu.VMEM((tile, D_MODEL), F32)],
    )
    return pl.pallas_call(
        _expert_kernel,
        grid_spec=grid_spec,
        out_shape=jax.ShapeDtypeStruct((total, D_MODEL), F32),
        compiler_params=_cparams(("arbitrary", "arbitrary")),
        name="experts",
    )(blk_e, nused, xs, wg, wu, wd)


def _combine_kernel(p0_ref, p1_ref, y_hbm, x_ref, gate_ref, g_ref, b_ref, out_ref, buf0, buf1, sem, *, tile):
    def start(r, c):
        _row_copy(y_hbm, p0_ref[0, r], buf0, r, sem).start()
        _row_copy(y_hbm, p1_ref[0, r], buf1, r, sem).start()
        return c
    lax.fori_loop(0, tile, start, 0, unroll=DMA_UNROLL)

    def wait(r, c):
        _row_copy(y_hbm, 0, buf0, r, sem).wait()
        _row_copy(y_hbm, 0, buf1, r, sem).wait()
        return c
    lax.fori_loop(0, tile, wait, 0, unroll=DMA_UNROLL)
    f = buf0[...] * gate_ref[:, 0:1] + buf1[...] * gate_ref[:, 1:2]
    out_ref[...] = _layer_norm(DN_ALPHA * x_ref[...] + f, g_ref[...], b_ref[...])


def _combine(y, pos, gates, x2, g, b):
    n = x2.shape[0]
    tile = min(COMB_TILE, n)
    nb = n // tile
    smem = pl.BlockSpec((None, 1, tile), lambda i: (i, 0, 0), memory_space=pltpu.SMEM)
    full = pl.BlockSpec((1, D_MODEL), lambda i: (0, 0))
    return pl.pallas_call(
        functools.partial(_combine_kernel, tile=tile),
        grid=(nb,),
        in_specs=[smem, smem, pl.BlockSpec(memory_space=pl.ANY),
                  pl.BlockSpec((tile, D_MODEL), lambda i: (i, 0)),
                  pl.BlockSpec((tile, TOP_K), lambda i: (i, 0)), full, full],
        out_specs=pl.BlockSpec((tile, D_MODEL), lambda i: (i, 0)),
        out_shape=jax.ShapeDtypeStruct((n, D_MODEL), F32),
        scratch_shapes=[pltpu.VMEM((tile, D_MODEL), F32), pltpu.VMEM((tile, D_MODEL), F32),
                        pltpu.SemaphoreType.DMA(())],
        compiler_params=_cparams(("arbitrary",)),
        name="moe_combine_ln",
    )(pos[:, 0].reshape(nb, 1, tile), pos[:, 1].reshape(nb, 1, tile), y, x2, gates,
      g.reshape(1, D_MODEL), b.reshape(1, D_MODEL))


def _moe(x2, router_w, wg, wu, wd, g, b):
    n = x2.shape[0]
    top_e, gates = _router(x2, router_w.T)
    flat_e = top_e.T.reshape(-1)
    onehot = (flat_e[:, None] == jnp.arange(N_EXPERTS, dtype=jnp.int32)[None, :]).astype(jnp.int32)
    rank = jnp.sum((jnp.cumsum(onehot, axis=0) - onehot) * onehot, axis=1)
    counts = jnp.sum(onehot, axis=0)
    padded = ((counts + MOE_TILE - 1) // MOE_TILE) * MOE_TILE
    pad_end = jnp.cumsum(padded)
    pad_start = pad_end - padded
    pos = (pad_start[flat_e] + rank).astype(jnp.int32).reshape(n, TOP_K)
    total = n * TOP_K + N_EXPERTS * MOE_TILE
    nblk = total // MOE_TILE
    blk_e = jnp.minimum(jnp.searchsorted(pad_end, jnp.arange(nblk, dtype=jnp.int32) * MOE_TILE, side='right'),
                        N_EXPERTS - 1).astype(jnp.int32)
    nused = (pad_end[-1:] // MOE_TILE).astype(jnp.int32)
    xs = _dispatch(x2, pos, total)
    y = _experts(xs, blk_e, nused, wg, wu, wd)
    return _combine(y, pos, gates.T, x2, g, b)


def _pad_head_proj(w_uk, w_uv):
    H, DH, DC = DSA_HEADS, DSA_HEAD_DIM, DSA_LATENT
    eye = jnp.eye(H, dtype=w_uk.dtype)
    wuk_pad = jnp.einsum('hg,hdc->hgdc', eye, w_uk).reshape(H, H * DH, DC)
    wuv_pad = jnp.einsum('hg,hcd->hcgd', eye, w_uv).reshape(H, DC, H * DH)
    return wuk_pad.astype(BF16), wuv_pad.astype(BF16)


def kernel(x, w_in, mlstm_conv_w, mlstm_gate_bias, mlstm_norm_g, dsa_kv_norm_g, dsa_w_uk, dsa_w_uv, rel_bias,
           w_branch_a, w_branch_b, w_out, ln_g, ln_b, dense_w_gate, dense_w_up, dense_w_down, router_w,
           expert_w_gate, expert_w_up, expert_w_down):
    bsz, seq, d = x.shape
    n = bsz * seq
    x2 = x.reshape(n, d)
    tab = _near_bias_tables(rel_bias, *_dsa_tiles(seq))
    for l in range(DEPTH):
        p = dict(zip([nm for nm, _, _ in IN_PROJ_OUTS], _in_proj(x2, _regroup_in_proj(w_in[l]))))
        ya = _mlstm(p["qk"], p["av"], p["aif"], p["ao"], mlstm_conv_w[l], mlstm_gate_bias[l], mlstm_norm_g[l],
                    bsz, seq)
        ckv = _kv_norm(p["bc"], dsa_kv_norm_g[l])
        wuk_pad, wuv_pad = _pad_head_proj(dsa_w_uk[l], dsa_w_uv[l])
        yb = _dsa(p["bq"], p["iq"], p["ikw"], ckv, tab, wuk_pad, wuv_pad, bsz, seq)
        x2 = _merge(x2, ya, yb, p["ga"], p["gb"], w_branch_a[l].astype(BF16), w_branch_b[l].astype(BF16),
                    w_out[l].astype(BF16), ln_g[l, 0], ln_b[l, 0])
        j = l // 2
        if l % 2 == 0:
            x2 = _dense_ffn(x2, dense_w_gate[j].astype(BF16), dense_w_up[j].astype(BF16),
                            dense_w_down[j].astype(BF16), ln_g[l, 1], ln_b[l, 1])
        else:
            x2 = _moe(x2, router_w[j], expert_w_gate[j].astype(BF16), expert_w_up[j].astype(BF16),
                      expert_w_down[j].astype(BF16), ln_g[l, 1], ln_b[l, 1])
    return x2.reshape(bsz, seq, d)
```

```python
import functools
import math

import numpy as np
import jax
import jax.numpy as jnp
from jax import lax
from jax.experimental import pallas as pl
from jax.experimental.pallas import tpu as pltpu

D_MODEL = 1024
DEPTH = 4
MLSTM_HEADS = 4
MLSTM_QK_DIM = 64
MLSTM_V_DIM = 128
MLSTM_CONV = 4
DSA_HEADS = 8
DSA_HEAD_DIM = 64
DSA_LATENT = 128
IDX_HEADS = 8
IDX_DIM = 32
IDX_TOPK_MAX = 256
REL_BUCKETS = 32
REL_MAX_DIST = 128
D_FF = 2816
N_EXPERTS = 8
TOP_K = 2
D_FF_EXPERT = 3584
DN_ALPHA = (2 * DEPTH) ** 0.25
LN_EPS = 1e-5

A_QK = 2 * MLSTM_HEADS * MLSTM_QK_DIM
A_V = MLSTM_HEADS * MLSTM_V_DIM
A_GATE = 2 * MLSTM_HEADS
B_Q = DSA_HEADS * DSA_HEAD_DIM
I_Q = IDX_HEADS * IDX_DIM
PROJ_SIZES = (A_QK, A_V, A_V, A_GATE, B_Q, DSA_LATENT, I_Q, IDX_DIM, IDX_HEADS, D_MODEL, D_MODEL)

LANES = 128
SUBLANES = 8
VMEM_LIMIT = 56 * 1024 * 1024
TOK_TILE = 512
MLSTM_TILE = 256
DSA_TQ = 128
DSA_KN = 256
DSA_KF = 512
MOE_TILE = 512
MOE_FT = 512
COMB_TILE = 256
DMA_UNROLL = 8
FF_CHUNK = 256
INT_MIN = -(2 ** 31)
NEG_BIG = -1e30
M_FLOOR = -1e29
LOG2E = math.log2(math.e)
F32_MIN_NORMAL = 2.0 ** -126
BF16_MIN_NORMAL_BITS = 0x0080

BF16 = jnp.bfloat16
F32 = jnp.float32
NT_DIMS = (((1,), (1,)), ((), ()))


def _cparams(sem):
    return pltpu.CompilerParams(dimension_semantics=sem, vmem_limit_bytes=VMEM_LIMIT)


def _dot(a, b):
    return jnp.dot(a, b, preferred_element_type=F32)


def _dot_nt(a, b):
    return lax.dot_general(a, b, NT_DIMS, preferred_element_type=F32)


def _layer_norm(r, g, b):
    mu = jnp.mean(r, axis=-1, keepdims=True)
    d = r - mu
    var = jnp.mean(d * d, axis=-1, keepdims=True)
    return d * lax.rsqrt(var + LN_EPS) * g + b


def _silu(x):
    return x * jax.nn.sigmoid(x)


IN_PROJ_OUTS = (("qk", A_QK, F32), ("av", A_V, BF16), ("ao", A_V, BF16), ("bq", B_Q, BF16),
                ("iq", I_Q, BF16), ("ga", D_MODEL, BF16), ("gb", D_MODEL, BF16),
                ("aif", LANES, F32), ("bc", LANES, F32), ("ikw", LANES, F32))
IN_PROJ_COLS = sum(w for _, w, _ in IN_PROJ_OUTS)
IKW_W_OFF = IDX_DIM


def _regroup_in_proj(w):
    offs = np.concatenate([[0], np.cumsum(PROJ_SIZES)])
    a_qk, a_v, a_o, a_if, b_q, b_c, i_q, i_k, i_w, g_a, g_b = (w[:, offs[i]:offs[i + 1]] for i in range(11))
    pad = lambda a: jnp.pad(a, ((0, 0), (0, LANES - a.shape[1])))
    groups = [a_qk, a_v, a_o, b_q, i_q, g_a, g_b, pad(a_if), b_c, pad(jnp.concatenate([i_k, i_w], axis=1))]
    return jnp.concatenate(groups, axis=1).astype(BF16)


def _in_proj_kernel(x_ref, w_ref, *out_refs):
    xb = x_ref[...].astype(BF16)
    off = 0
    for (_, width, dtype), o_ref in zip(IN_PROJ_OUTS, out_refs):
        o_ref[...] = _dot(xb, w_ref[:, off:off + width]).astype(dtype)
        off += width


def _in_proj(x2, w):
    n = x2.shape[0]
    tm = min(TOK_TILE, n)
    return pl.pallas_call(
        _in_proj_kernel,
        grid=(n // tm,),
        in_specs=[pl.BlockSpec((tm, D_MODEL), lambda i: (i, 0)),
                  pl.BlockSpec((D_MODEL, IN_PROJ_COLS), lambda i: (0, 0))],
        out_specs=[pl.BlockSpec((tm, wd), lambda i: (i, 0)) for _, wd, _ in IN_PROJ_OUTS],
        out_shape=[jax.ShapeDtypeStruct((n, wd), dt) for _, wd, dt in IN_PROJ_OUTS],
        compiler_params=_cparams(("parallel",)),
        name="in_proj",
    )(x2, w)


def _log_sigmoid(x):
    return jnp.minimum(x, 0.0) - jnp.log1p(jnp.exp(-jnp.abs(x)))


def _mlstm_kernel(qk_ref, av_ref, aif_ref, ao_ref, convw_ref, gbias_ref, ng_ref, out_ref,
                  ext_ref, ct_ref, m_ref, *, tile):
    L = tile
    H, DK, DV = MLSTM_HEADS, MLSTM_QK_DIM, MLSTM_V_DIM

    @pl.when(pl.program_id(1) == 0)
    def _():
        ext_ref[0:SUBLANES, :] = jnp.zeros((SUBLANES, A_QK), F32)
        ct_ref[...] = jnp.zeros_like(ct_ref)
        m_ref[...] = jnp.zeros_like(m_ref)

    u = qk_ref[...]
    ext_ref[SUBLANES:SUBLANES + L, :] = u
    conv = jnp.zeros((L, A_QK), F32)
    for j in range(MLSTM_CONV):
        conv = conv + ext_ref[pl.ds(SUBLANES - (MLSTM_CONV - 1) + j, L), :] * convw_ref[j:j + 1, :]
    ext_ref[0:SUBLANES, :] = u[L - SUBLANES:L, :]
    qk = _silu(conv)

    g_col = aif_ref[...] + gbias_ref[...]
    g_row = g_col.T
    row = lax.broadcasted_iota(jnp.int32, (L, L), 0)
    col = lax.broadcasted_iota(jnp.int32, (L, L), 1)
    causal = row >= col
    tril = jnp.where(causal, 1.0, 0.0).astype(F32)
    triu = jnp.where(row <= col, 1.0, 0.0).astype(F32)
    b_col = jnp.dot(tril, _log_sigmoid(g_col), preferred_element_type=F32, precision=lax.Precision.HIGHEST)
    b_row = jnp.dot(_log_sigmoid(g_row), triu, preferred_element_type=F32, precision=lax.Precision.HIGHEST)

    ones_v = jnp.ones((L, DV), BF16)
    ones_t = jnp.ones((DV, L), F32)
    for h in range(H):
        bcol = b_col[:, H + h:H + h + 1]
        icol = g_col[:, h:h + 1]
        brow = b_row[H + h:H + h + 1, :]
        irow = g_row[h:h + 1, :]
        mprev = m_ref[h:h + 1, 0:1]
        dlog = jnp.where(causal, bcol - brow + irow, -jnp.inf)
        m_t = jnp.maximum(bcol + mprev, jnp.max(dlog, axis=1, keepdims=True))
        q = qk[:, h * DK:(h + 1) * DK].astype(BF16)
        k = qk[:, H * DK + h * DK:H * DK + (h + 1) * DK] * (DK ** -0.5)
        s_w = _dot_nt(q, k.astype(BF16)) * jnp.exp(dlog - m_t)
        inter = jnp.exp(bcol + mprev - m_t)
        v_h = av_ref[:, h * DV:(h + 1) * DV]
        v_ext = jnp.concatenate([v_h, ones_v], axis=1)
        ct_st = ct_ref[h]
        ne = _dot(s_w.astype(BF16), v_ext) + inter * _dot_nt(q, ct_st.astype(BF16))
        h_out = ne[:, :DV] / jnp.maximum(jnp.abs(ne[:, DV:]), jnp.exp(-m_t))
        blast = bcol[L - 1:L, :]
        glog = blast - bcol + icol
        m_new = jnp.maximum(blast + mprev, jnp.max(glog, axis=0, keepdims=True))
        decay = jnp.exp(blast + mprev - m_new)
        w_k = k * jnp.exp(glog - m_new)
        v_ext_t = jnp.concatenate([v_h.astype(F32).T, ones_t], axis=0).astype(BF16)
        ct_ref[h] = decay * ct_st + _dot(v_ext_t, w_k.astype(BF16))
        m_ref[h:h + 1, :] = jnp.broadcast_to(m_new, (1, LANES))
        mu = jnp.mean(h_out, axis=-1, keepdims=True)
        d = h_out - mu
        hn = d * lax.rsqrt(jnp.mean(d * d, axis=-1, keepdims=True) + LN_EPS)
        y = hn * ng_ref[:, h * DV:(h + 1) * DV] * jax.nn.sigmoid(ao_ref[:, h * DV:(h + 1) * DV].astype(F32))
        out_ref[:, h * DV:(h + 1) * DV] = y.astype(BF16)


def _mlstm(qk, av, aif, ao, conv_w, gate_bias, norm_g, bsz, seq):
    tile = min(MLSTM_TILE, seq)
    nc = seq // tile
    gbias = jnp.zeros((1, LANES), F32).at[0, :A_GATE].set(gate_bias.reshape(-1))
    tok = lambda w: pl.BlockSpec((tile, w), lambda b, c: (b * nc + c, 0))
    full = lambda shape: pl.BlockSpec(shape, lambda b, c: (0,) * len(shape))
    return pl.pallas_call(
        functools.partial(_mlstm_kernel, tile=tile),
        grid=(bsz, nc),
        in_specs=[tok(A_QK), tok(A_V), tok(LANES), tok(A_V),
                  full((MLSTM_CONV, A_QK)), full((1, LANES)), full((1, A_V))],
        out_specs=tok(A_V),
        out_shape=jax.ShapeDtypeStruct((bsz * seq, A_V), BF16),
        scratch_shapes=[pltpu.VMEM((tile + SUBLANES, A_QK), F32),
                        pltpu.VMEM((MLSTM_HEADS, 2 * MLSTM_V_DIM, MLSTM_QK_DIM), F32),
                        pltpu.VMEM((SUBLANES, LANES), F32)],
        compiler_params=_cparams(("parallel", "arbitrary")),
        name="mlstm",
    )(qk, av, aif, ao, conv_w, gbias, norm_g.reshape(1, A_V))


def _t5_bucket_np(dist):
    dist = np.maximum(dist, 0)
    exact = REL_BUCKETS // 2
    log_ratio = (np.log(np.maximum(dist, 1).astype(np.float32) / np.float32(exact))
                 / np.float32(math.log(REL_MAX_DIST / exact))).astype(np.float32)
    large = np.minimum(exact + (log_ratio * np.float32(REL_BUCKETS - exact)).astype(np.int32), REL_BUCKETS - 1)
    return np.where(dist < exact, dist, large)


def _near_bias_tables(rel_bias, tq, kn, kf):
    n_off = kf // tq + 1
    t = np.arange(tq)[None, :, None]
    s = np.arange(kn)[None, None, :]
    off = (np.arange(n_off) * tq)[:, None, None]
    dist = off + t - s
    bucket = _t5_bucket_np(dist)
    assert np.all(_t5_bucket_np(np.arange(tq + 1, 4 * REL_MAX_DIST)) == REL_BUCKETS - 1)
    tab = rel_bias[bucket] - rel_bias[REL_BUCKETS - 1]
    tab = jnp.where((dist >= 0)[..., None], tab, 0.0) * LOG2E
    return tab.transpose(0, 3, 1, 2).reshape(n_off, DSA_HEADS * tq, kn).astype(F32)


def _dsa_kernel(bq_ref, iq_ref, ikwq_ref, ikw_ref, ckv_ref, tab_ref, wuk_ref, wuv_ref, out_ref,
                key_ref, stage_ref, acc_ref, m_ref, ql_ref, wb_ref, *, tq, kn, kf, topk, packed, unroll):
    H = DSA_HEADS
    j = pl.program_id(1)
    t0 = j * tq
    n_score = (t0 + tq - 1) // kf + 1
    n_far = jnp.maximum(t0 - tq, 0) // kf
    near_lo = n_far * (kf // kn)
    near_hi = (t0 + tq - 1) // kn + 1

    bq = bq_ref[...]
    for h in range(H):
        ql_ref[h * tq:(h + 1) * tq, :] = (_dot(bq, wuk_ref[h]) * (DSA_HEAD_DIM ** -0.5 * LOG2E)).astype(BF16)
    w_idx = ikwq_ref[:, IKW_W_OFF:IKW_W_OFF + IDX_HEADS] * ((IDX_HEADS * IDX_DIM) ** -0.5)
    for h in range(IDX_HEADS):
        wb_ref[h] = jnp.broadcast_to(w_idx[:, h:h + 1], (tq, kf))
    iq = iq_ref[...]
    qis = [iq[:, h * IDX_DIM:(h + 1) * IDX_DIM] for h in range(IDX_HEADS)]
    tpos = t0 + lax.broadcasted_iota(jnp.int32, (tq, kf), 0)
    lpos = lax.broadcasted_iota(jnp.int32, (tq, kf), 1)

    ncol_f = kf // LANES
    tile_f = lambda a: jnp.concatenate([a] * ncol_f, axis=1)

    def score_chunk(c, masked):
        s0 = pl.multiple_of(c * kf, kf)
        kidx = ikw_ref[pl.ds(s0, kf), :][:, 0:IDX_DIM].astype(BF16)
        sc = jnp.zeros((tq, kf), F32)
        for h in range(IDX_HEADS):
            sc = sc + wb_ref[h] * jnp.maximum(_dot_nt(qis[h], kidx), 0.0)
        sc = jnp.where(jnp.abs(sc) < F32_MIN_NORMAL, 0.0, sc)
        bits = lax.bitcast_convert_type(sc, jnp.int32)
        key = bits ^ ((bits >> 31) & 0x7FFFFFFF)
        top = lax.bitcast_convert_type(bits & jnp.int32(-65536), F32)
        if masked:
            causal = s0 + lpos <= tpos
            key = jnp.where(causal, key, INT_MIN)
            top = jnp.where(causal, top, -jnp.inf)
        key_ref[:, pl.ds(s0, kf)] = key
        if packed:
            stage_ref[:, pl.ds(s0, kf)] = top.astype(BF16)

    def score_pair(i, carry):
        score_chunk(2 * i, False)
        score_chunk(2 * i + 1, False)
        return carry

    def score_one(c, carry):
        score_chunk(c, False)
        return carry

    if unroll:
        lax.fori_loop(0, (n_score - 1) // 2, score_pair, 0)

        @pl.when((n_score - 1) % 2 == 1)
        def _():
            score_chunk(n_score - 2, False)
    else:
        lax.fori_loop(0, n_score - 1, score_one, 0)

    score_chunk(n_score - 1, True)

    one_b = jnp.ones((tq, kf), BF16)
    zero_b = jnp.zeros((tq, kf), BF16)

    def count_stage(cand):
        cand_w = tile_f(cand.astype(BF16))

        def body(c, acc):
            s0 = pl.multiple_of(c * kf, kf)
            ind = jnp.where(stage_ref[:, pl.ds(s0, kf)] >= cand_w, one_b, zero_b)
            for g in range(ncol_f):
                acc = acc + ind[:, g * LANES:(g + 1) * LANES]
            return acc
        acc = lax.fori_loop(0, n_score, body, jnp.zeros((tq, LANES), BF16)).astype(F32)
        return jnp.sum(acc, axis=1, keepdims=True)

    def rewrite_stage(value_of_key):
        def body(c, carry):
            s0 = pl.multiple_of(c * kf, kf)
            stage_ref[:, pl.ds(s0, kf)] = value_of_key(key_ref[:, pl.ds(s0, kf)]).astype(BF16)
            return carry
        lax.fori_loop(0, n_score, body, 0)

    def search(nbits, start, to_cand):
        def step(i, t):
            cand = t + lax.shift_left(jnp.int32(1), nbits - 1 - i)
            return jnp.where(count_stage(to_cand(cand)) >= topk, cand, t)
        return lax.fori_loop(0, nbits, step, start)

    def top16_value(k):
        k = jnp.where(jnp.logical_and(k > 0, k < BF16_MIN_NORMAL_BITS), BF16_MIN_NORMAL_BITS, k)
        pat = (k ^ ((k >> 31) & 0x7FFF)) & 0xFFFF
        return lax.bitcast_convert_type(lax.shift_left(pat, 16), F32)

    def count_key(pred):
        def body(c, acc):
            s0 = pl.multiple_of(c * kf, kf)
            ind = jnp.where(pred(key_ref[:, pl.ds(s0, kf)]), 1.0, 0.0)
            for g in range(ncol_f):
                acc = acc + ind[:, g * LANES:(g + 1) * LANES]
            return acc
        acc = lax.fori_loop(0, n_score, body, jnp.zeros((tq, LANES), F32))
        return jnp.sum(acc, axis=1, keepdims=True)

    lowest = jnp.full((tq, LANES), -32768, jnp.int32)
    zero_i = jnp.zeros((tq, LANES), jnp.int32)
    if packed:
        t_a = jnp.where(count_stage(top16_value(zero_i)) >= topk, zero_i, lowest)
        t_a = search(15, t_a, top16_value)
        enough = t_a > lowest
        t_a = jnp.where(enough, t_a, zero_i)

        base_b = tile_f(lax.shift_left(t_a, 8))
        rewrite_stage(lambda key: ((key >> 8) - base_b).astype(F32))
        t_b = search(8, zero_i, lambda k: k.astype(F32))

        base_c = tile_f(lax.shift_left(t_a, 8) | t_b)
        rewrite_stage(lambda key: ((key >> 8) - base_c).astype(F32) * 256.0 + (key & 0xFF).astype(F32))
        t_c = search(8, zero_i, lambda k: k.astype(F32))

        thr = jnp.where(enough, lax.shift_left(t_a, 16) | lax.shift_left(t_b, 8) | t_c, INT_MIN + 1)
    else:
        count_ge = lambda cand: count_key(lambda k: k >= tile_f(cand))
        thr = jnp.where(count_ge(zero_i) >= topk, zero_i, INT_MIN)

        def bit_step(i, t):
            cand = t + lax.shift_left(jnp.int32(1), 30 - i)
            return jnp.where(count_ge(cand) >= topk, cand, t)

        thr = lax.fori_loop(0, 31, bit_step, thr)
        enough = thr > INT_MIN
        thr = jnp.maximum(thr, INT_MIN + 1)

    thr_f = tile_f(thr)
    surplus = jnp.where(enough[:, 0:1], count_key(lambda k: k >= thr_f) - topk, 0.0)

    @pl.when(jnp.max(surplus) > 0.0)
    def _():
        allowed = topk - count_key(lambda k: k > thr_f)
        r = lax.broadcasted_iota(jnp.int32, (kf, kf), 0)
        cidx = lax.broadcasted_iota(jnp.int32, (kf, kf), 1)
        upper = jnp.where(r <= cidx, 1.0, 0.0).astype(BF16)

        def body(c, seen):
            s0 = pl.multiple_of(c * kf, kf)
            key = key_ref[:, pl.ds(s0, kf)]
            tie = key == thr_f
            tie_f = jnp.where(tie, 1.0, 0.0)
            rank = _dot(tie_f.astype(BF16), upper) + seen
            key_ref[:, pl.ds(s0, kf)] = jnp.where(jnp.logical_and(tie, rank > allowed), thr_f - 1, key)
            return seen + jnp.sum(tie_f, axis=1, keepdims=True)
        lax.fori_loop(0, n_score, body, jnp.zeros((tq, 1), F32))

    m_ref[...] = jnp.full(m_ref.shape, M_FLOOR, F32)
    acc_ref[...] = jnp.zeros_like(acc_ref)

    def attend(s0, width, table_idx):
        ncol = width // LANES
        kv = ckv_ref[pl.ds(s0, width), :]
        kv_ext = jnp.concatenate([kv, jnp.ones((width, DSA_LATENT), BF16)], axis=1)
        thr_w = jnp.concatenate([thr] * ncol, axis=1)
        madd = jnp.where(key_ref[:, pl.ds(s0, width)] >= thr_w, 0.0, NEG_BIG)
        for h in range(H):
            rows = slice(h * tq, (h + 1) * tq)
            x = _dot_nt(ql_ref[rows, :], kv) + madd
            if table_idx is not None:
                x = x + tab_ref[table_idx, rows, :]
            m_old = m_ref[rows, :]
            m_new = jnp.maximum(m_old, jnp.max(x, axis=1, keepdims=True))
            alpha = jnp.exp2(m_old - m_new)
            p = jnp.exp2(x - jnp.concatenate([m_new] * ncol, axis=1))
            acc_ref[rows, :] = jnp.concatenate([alpha, alpha], axis=1) * acc_ref[rows, :] + _dot(p.astype(BF16), kv_ext)
            m_ref[rows, :] = m_new

    def far_pair(i, carry):
        attend(pl.multiple_of(2 * i * kf, kf), kf, None)
        attend(pl.multiple_of((2 * i + 1) * kf, kf), kf, None)
        return carry

    def near_chunk(c, carry):
        s0 = pl.multiple_of(c * kn, kn)
        attend(s0, kn, (t0 - s0) // tq)
        return carry

    def far_one(c, carry):
        attend(pl.multiple_of(c * kf, kf), kf, None)
        return carry

    if unroll:
        lax.fori_loop(0, n_far // 2, far_pair, 0)

        @pl.when(n_far % 2 == 1)
        def _():
            attend(pl.multiple_of((n_far - 1) * kf, kf), kf, None)
    else:
        lax.fori_loop(0, n_far, far_one, 0)

    lax.fori_loop(near_lo, near_hi, near_chunk, 0)

    y = jnp.zeros((tq, B_Q), F32)
    for h in range(H):
        a = acc_ref[h * tq:(h + 1) * tq, :]
        o = a[:, :DSA_LATENT] / a[:, DSA_LATENT:]
        y = y + _dot(o.astype(BF16), wuv_ref[h])
    out_ref[...] = y.astype(BF16)


def _dsa_tiles(seq):
    tq = min(DSA_TQ, seq)
    return tq, min(DSA_KN, seq), min(DSA_KF, seq)


def _dsa(bq, iq, ikw, ckv, tab, wuk_pad, wuv_pad, bsz, seq, packed=True, unroll=True):
    tq, kn, kf = _dsa_tiles(seq)
    nq = seq // tq
    topk = min(IDX_TOPK_MAX, seq // 4)
    n_off = kf // tq + 1
    qblk = lambda w: pl.BlockSpec((tq, w), lambda b, j: (b * nq + j, 0))
    seqblk = lambda w: pl.BlockSpec((seq, w), lambda b, j: (b, 0))
    full = lambda shape: pl.BlockSpec(shape, lambda b, j: (0,) * len(shape))
    return pl.pallas_call(
        functools.partial(_dsa_kernel, tq=tq, kn=kn, kf=kf, topk=topk, packed=packed, unroll=unroll),
        grid=(bsz, nq),
        in_specs=[qblk(B_Q), qblk(I_Q), qblk(LANES), seqblk(LANES), seqblk(DSA_LATENT),
                  full((n_off, DSA_HEADS * tq, kn)), full((DSA_HEADS, B_Q, DSA_LATENT)),
                  full((DSA_HEADS, DSA_LATENT, B_Q))],
        out_specs=qblk(B_Q),
        out_shape=jax.ShapeDtypeStruct((bsz * seq, B_Q), BF16),
        scratch_shapes=[pltpu.VMEM((tq, seq), jnp.int32),
                        pltpu.VMEM((tq, seq), BF16),
                        pltpu.VMEM((DSA_HEADS * tq, 2 * DSA_LATENT), F32),
                        pltpu.VMEM((DSA_HEADS * tq, LANES), F32),
                        pltpu.VMEM((DSA_HEADS * tq, DSA_LATENT), BF16),
                        pltpu.VMEM((IDX_HEADS, tq, kf), F32)],
        compiler_params=_cparams(("parallel", "arbitrary")),
        name="dsa",
    )(bq, iq, ikw, ikw, ckv, tab, wuk_pad, wuv_pad)


def _kv_norm_kernel(bc_ref, g_ref, out_ref):
    x = bc_ref[...]
    out_ref[...] = (x * lax.rsqrt(jnp.mean(x * x, axis=-1, keepdims=True) + LN_EPS) * g_ref[...]).astype(BF16)


def _kv_norm(bc, g):
    n = bc.shape[0]
    tm = min(4 * TOK_TILE, n)
    return pl.pallas_call(
        _kv_norm_kernel,
        grid=(n // tm,),
        in_specs=[pl.BlockSpec((tm, DSA_LATENT), lambda i: (i, 0)), pl.BlockSpec((1, DSA_LATENT), lambda i: (0, 0))],
        out_specs=pl.BlockSpec((tm, DSA_LATENT), lambda i: (i, 0)),
        out_shape=jax.ShapeDtypeStruct((n, DSA_LATENT), BF16),
        compiler_params=_cparams(("parallel",)),
        name="kv_norm",
    )(bc, g.reshape(1, DSA_LATENT))


def _merge_kernel(x_ref, ya_ref, yb_ref, ga_ref, gb_ref, wa_ref, wb_ref, wo_ref, g_ref, b_ref, out_ref):
    merged = (jax.nn.sigmoid(ga_ref[...].astype(F32)) * _dot(ya_ref[...], wa_ref[...])
              + jax.nn.sigmoid(gb_ref[...].astype(F32)) * _dot(yb_ref[...], wb_ref[...]))
    mix = _dot(merged.astype(BF16), wo_ref[...])
    out_ref[...] = _layer_norm(DN_ALPHA * x_ref[...] + mix, g_ref[...], b_ref[...])


def _merge(x2, ya, yb, ga, gb, wa, wb, wo, g, b):
    n = x2.shape[0]
    tm = min(TOK_TILE, n)
    tok = lambda w: pl.BlockSpec((tm, w), lambda i: (i, 0))
    full = lambda shape: pl.BlockSpec(shape, lambda i: (0,) * len(shape))
    return pl.pallas_call(
        _merge_kernel,
        grid=(n // tm,),
        in_specs=[tok(D_MODEL), tok(A_V), tok(B_Q), tok(D_MODEL), tok(D_MODEL),
                  full((A_V, D_MODEL)), full((B_Q, D_MODEL)), full((D_MODEL, D_MODEL)),
                  full((1, D_MODEL)), full((1, D_MODEL))],
        out_specs=tok(D_MODEL),
        out_shape=jax.ShapeDtypeStruct((n, D_MODEL), F32),
        compiler_params=_cparams(("parallel",)),
        name="merge_ln",
    )(x2, ya, yb, ga, gb, wa, wb, wo, g.reshape(1, D_MODEL), b.reshape(1, D_MODEL))


def _dense_ffn_kernel(x_ref, wg_ref, wu_ref, wd_ref, g_ref, b_ref, out_ref):
    x = x_ref[...]
    xb = x.astype(BF16)
    f = jnp.zeros(x.shape, F32)
    for c in range(D_FF // FF_CHUNK):
        cols = slice(c * FF_CHUNK, (c + 1) * FF_CHUNK)
        hdn = _silu(_dot(xb, wg_ref[:, cols])) * _dot(xb, wu_ref[:, cols])
        f = f + _dot(hdn.astype(BF16), wd_ref[cols, :])
    out_ref[...] = _layer_norm(DN_ALPHA * x + f, g_ref[...], b_ref[...])


def _dense_ffn(x2, wg, wu, wd, g, b):
    n = x2.shape[0]
    tm = min(TOK_TILE, n)
    tok = pl.BlockSpec((tm, D_MODEL), lambda i: (i, 0))
    full = lambda shape: pl.BlockSpec(shape, lambda i: (0,) * len(shape))
    return pl.pallas_call(
        _dense_ffn_kernel,
        grid=(n // tm,),
        in_specs=[tok, full((D_MODEL, D_FF)), full((D_MODEL, D_FF)), full((D_FF, D_MODEL)),
                  full((1, D_MODEL)), full((1, D_MODEL))],
        out_specs=tok,
        out_shape=jax.ShapeDtypeStruct((n, D_MODEL), F32),
        compiler_params=_cparams(("parallel",)),
        name="dense_ffn_ln",
    )(x2, wg, wu, wd, g.reshape(1, D_MODEL), b.reshape(1, D_MODEL))


def _router_kernel(x_ref, rw_ref, e_ref, g_ref):
    logits = lax.dot_general(rw_ref[...], x_ref[...], NT_DIMS, preferred_element_type=F32,
                             precision=lax.Precision.HIGHEST)
    eidx = lax.broadcasted_iota(jnp.int32, logits.shape, 0)
    m1 = jnp.max(logits, axis=0, keepdims=True)
    i1 = jnp.min(jnp.where(logits == m1, eidx, N_EXPERTS), axis=0, keepdims=True)
    rest = jnp.where(eidx == i1, -jnp.inf, logits)
    m2 = jnp.max(rest, axis=0, keepdims=True)
    i2 = jnp.min(jnp.where(rest == m2, eidx, N_EXPERTS), axis=0, keepdims=True)
    e2 = jnp.exp(m2 - m1)
    den = 1.0 + e2
    e_ref[0:1, :] = i1
    e_ref[1:2, :] = i2
    g_ref[0:1, :] = 1.0 / den
    g_ref[1:2, :] = e2 / den


def _router(x2, rw_t):
    n = x2.shape[0]
    tm = min(TOK_TILE, n)
    return pl.pallas_call(
        _router_kernel,
        grid=(n // tm,),
        in_specs=[pl.BlockSpec((tm, D_MODEL), lambda i: (i, 0)), pl.BlockSpec((N_EXPERTS, D_MODEL), lambda i: (0, 0))],
        out_specs=[pl.BlockSpec((TOP_K, tm), lambda i: (0, i)), pl.BlockSpec((TOP_K, tm), lambda i: (0, i))],
        out_shape=[jax.ShapeDtypeStruct((TOP_K, n), jnp.int32), jax.ShapeDtypeStruct((TOP_K, n), F32)],
        compiler_params=_cparams(("parallel",)),
        name="router",
    )(x2, rw_t)


def _row_copy(src, src_row, dst, dst_row, sem):
    return pltpu.make_async_copy(src.at[pl.ds(src_row, 1)], dst.at[pl.ds(dst_row, 1)], sem)


def _dispatch_kernel(d0_ref, d1_ref, x_ref, xs_init_hbm, xs_hbm, sem, *, tile):
    del xs_init_hbm

    def start(r, c):
        _row_copy(x_ref, r, xs_hbm, d0_ref[0, r], sem).start()
        _row_copy(x_ref, r, xs_hbm, d1_ref[0, r], sem).start()
        return c
    lax.fori_loop(0, tile, start, 0, unroll=DMA_UNROLL)

    def wait(r, c):
        _row_copy(x_ref, r, xs_hbm, 0, sem).wait()
        _row_copy(x_ref, r, xs_hbm, 0, sem).wait()
        return c
    lax.fori_loop(0, tile, wait, 0, unroll=DMA_UNROLL)


def _dispatch(x2, pos, total):
    n = x2.shape[0]
    tile = min(COMB_TILE, n)
    nb = n // tile
    smem = pl.BlockSpec((None, 1, tile), lambda i: (i, 0, 0), memory_space=pltpu.SMEM)
    return pl.pallas_call(
        functools.partial(_dispatch_kernel, tile=tile),
        grid=(nb,),
        in_specs=[smem, smem, pl.BlockSpec((tile, D_MODEL), lambda i: (i, 0)), pl.BlockSpec(memory_space=pl.ANY)],
        out_specs=pl.BlockSpec(memory_space=pl.ANY),
        out_shape=jax.ShapeDtypeStruct((total, D_MODEL), F32),
        scratch_shapes=[pltpu.SemaphoreType.DMA(())],
        input_output_aliases={3: 0},
        compiler_params=_cparams(("arbitrary",)),
        name="moe_dispatch",
    )(pos[:, 0].reshape(nb, 1, tile), pos[:, 1].reshape(nb, 1, tile), x2, jnp.zeros((total, D_MODEL), F32))


def _expert_kernel(blk_e_ref, nused_ref, xs_ref, wg_ref, wu_ref, wd_ref, y_ref, xb, acc):
    i = pl.program_id(0)
    f = pl.program_id(1)
    used = i < nused_ref[0]
    last = f == pl.num_programs(1) - 1

    @pl.when(jnp.logical_and(used, f == 0))
    def _():
        xb[...] = xs_ref[...].astype(BF16)
        acc[...] = jnp.zeros_like(acc)

    @pl.when(used)
    def _():
        x = xb[...]
        hdn = _silu(_dot(x, wg_ref[...])) * _dot(x, wu_ref[...])
        acc[...] += _dot(hdn.astype(BF16), wd_ref[...])

    @pl.when(jnp.logical_and(used, last))
    def _():
        y_ref[...] = acc[...]

    @pl.when(jnp.logical_and(jnp.logical_not(used), last))
    def _():
        y_ref[...] = jnp.zeros_like(y_ref)


def _experts(xs, blk_e, nused, wg, wu, wd):
    total = xs.shape[0]
    tile = MOE_TILE
    nblk = total // tile
    nf = D_FF_EXPERT // MOE_FT
    grid_spec = pltpu.PrefetchScalarGridSpec(
        num_scalar_prefetch=2,
        grid=(nblk, nf),
        in_specs=[pl.BlockSpec((tile, D_MODEL), lambda i, f, be, nu: (i, 0)),
                  pl.BlockSpec((None, D_MODEL, MOE_FT), lambda i, f, be, nu: (be[i], 0, f)),
                  pl.BlockSpec((None, D_MODEL, MOE_FT), lambda i, f, be, nu: (be[i], 0, f)),
                  pl.BlockSpec((None, MOE_FT, D_MODEL), lambda i, f, be, nu: (be[i], f, 0))],
        out_specs=pl.BlockSpec((tile, D_MODEL), lambda i, f, be, nu: (i, 0)),
        scratch_shapes=[pltpu.VMEM((tile, D_MODEL), BF16), pltpu.VMEM((tile, D_MODEL), F32)],
    )
    return pl.pallas_call(
        _expert_kernel,
        grid_spec=grid_spec,
        out_shape=jax.ShapeDtypeStruct((total, D_MODEL), F32),
        compiler_params=_cparams(("arbitrary", "arbitrary")),
        name="experts",
    )(blk_e, nused, xs, wg, wu, wd)


def _combine_kernel(p0_ref, p1_ref, y_hbm, x_ref, gate_ref, g_ref, b_ref, out_ref, buf0, buf1, sem, *, tile):
    def start(r, c):
        _row_copy(y_hbm, p0_ref[0, r], buf0, r, sem).start()
        _row_copy(y_hbm, p1_ref[0, r], buf1, r, sem).start()
        return c
    lax.fori_loop(0, tile, start, 0, unroll=DMA_UNROLL)

    def wait(r, c):
        _row_copy(y_hbm, 0, buf0, r, sem).wait()
        _row_copy(y_hbm, 0, buf1, r, sem).wait()
        return c
    lax.fori_loop(0, tile, wait, 0, unroll=DMA_UNROLL)
    f = buf0[...] * gate_ref[:, 0:1] + buf1[...] * gate_ref[:, 1:2]
    out_ref[...] = _layer_norm(DN_ALPHA * x_ref[...] + f, g_ref[...], b_ref[...])


def _combine(y, pos, gates, x2, g, b):
    n = x2.shape[0]
    tile = min(COMB_TILE, n)
    nb = n // tile
    smem = pl.BlockSpec((None, 1, tile), lambda i: (i, 0, 0), memory_space=pltpu.SMEM)
    full = pl.BlockSpec((1, D_MODEL), lambda i: (0, 0))
    return pl.pallas_call(
        functools.partial(_combine_kernel, tile=tile),
        grid=(nb,),
        in_specs=[smem, smem, pl.BlockSpec(memory_space=pl.ANY),
                  pl.BlockSpec((tile, D_MODEL), lambda i: (i, 0)),
                  pl.BlockSpec((tile, TOP_K), lambda i: (i, 0)), full, full],
        out_specs=pl.BlockSpec((tile, D_MODEL), lambda i: (i, 0)),
        out_shape=jax.ShapeDtypeStruct((n, D_MODEL), F32),
        scratch_shapes=[pltpu.VMEM((tile, D_MODEL), F32), pltpu.VMEM((tile, D_MODEL), F32),
                        pltpu.SemaphoreType.DMA(())],
        compiler_params=_cparams(("arbitrary",)),
        name="moe_combine_ln",
    )(pos[:, 0].reshape(nb, 1, tile), pos[:, 1].reshape(nb, 1, tile), y, x2, gates,
      g.reshape(1, D_MODEL), b.reshape(1, D_MODEL))


def _moe(x2, router_w, wg, wu, wd, g, b):
    n = x2.shape[0]
    top_e, gates = _router(x2, router_w.T)
    flat_e = top_e.T.reshape(-1)
    onehot = (flat_e[:, None] == jnp.arange(N_EXPERTS, dtype=jnp.int32)[None, :]).astype(jnp.int32)
    rank = jnp.sum((jnp.cumsum(onehot, axis=0) - onehot) * onehot, axis=1)
    counts = jnp.sum(onehot, axis=0)
    padded = ((counts + MOE_TILE - 1) // MOE_TILE) * MOE_TILE
    pad_end = jnp.cumsum(padded)
    pad_start = pad_end - padded
    pos = (pad_start[flat_e] + rank).astype(jnp.int32).reshape(n, TOP_K)
    total = n * TOP_K + N_EXPERTS * MOE_TILE
    nblk = total // MOE_TILE
    blk_e = jnp.minimum(jnp.searchsorted(pad_end, jnp.arange(nblk, dtype=jnp.int32) * MOE_TILE, side='right'),
                        N_EXPERTS - 1).astype(jnp.int32)
    nused = (pad_end[-1:] // MOE_TILE).astype(jnp.int32)
    xs = _dispatch(x2, pos, total)
    y = _experts(xs, blk_e, nused, wg, wu, wd)
    return _combine(y, pos, gates.T, x2, g, b)


def _pad_head_proj(w_uk, w_uv):
    H, DH, DC = DSA_HEADS, DSA_HEAD_DIM, DSA_LATENT
    eye = jnp.eye(H, dtype=w_uk.dtype)
    wuk_pad = jnp.einsum('hg,hdc->hgdc', eye, w_uk).reshape(H, H * DH, DC)
    wuv_pad = jnp.einsum('hg,hcd->hcgd', eye, w_uv).reshape(H, DC, H * DH)
    return wuk_pad.astype(BF16), wuv_pad.astype(BF16)


def kernel(x, w_in, mlstm_conv_w, mlstm_gate_bias, mlstm_norm_g, dsa_kv_norm_g, dsa_w_uk, dsa_w_uv, rel_bias,
           w_branch_a, w_branch_b, w_out, ln_g, ln_b, dense_w_gate, dense_w_up, dense_w_down, router_w,
           expert_w_gate, expert_w_up, expert_w_down):
    bsz, seq, d = x.shape
    n = bsz * seq
    x2 = x.reshape(n, d)
    tab = _near_bias_tables(rel_bias, *_dsa_tiles(seq))
    for l in range(DEPTH):
        p = dict(zip([nm for nm, _, _ in IN_PROJ_OUTS], _in_proj(x2, _regroup_in_proj(w_in[l]))))
        ya = _mlstm(p["qk"], p["av"], p["aif"], p["ao"], mlstm_conv_w[l], mlstm_gate_bias[l], mlstm_norm_g[l],
                    bsz, seq)
        ckv = _kv_norm(p["bc"], dsa_kv_norm_g[l])
        wuk_pad, wuv_pad = _pad_head_proj(dsa_w_uk[l], dsa_w_uv[l])
        yb = _dsa(p["bq"], p["iq"], p["ikw"], ckv, tab, wuk_pad, wuv_pad, bsz, seq,
                  packed=(l % 2 == 0), unroll=(l < 2))
        x2 = _merge(x2, ya, yb, p["ga"], p["gb"], w_branch_a[l].astype(BF16), w_branch_b[l].astype(BF16),
                    w_out[l].astype(BF16), ln_g[l, 0], ln_b[l, 0])
        j = l // 2
        if l % 2 == 0:
            x2 = _dense_ffn(x2, dense_w_gate[j].astype(BF16), dense_w_up[j].astype(BF16),
                            dense_w_down[j].astype(BF16), ln_g[l, 1], ln_b[l, 1])
        else:
            x2 = _moe(x2, router_w[j], expert_w_gate[j].astype(BF16), expert_w_up[j].astype(BF16),
                      expert_w_down[j].astype(BF16), ln_g[l, 1], ln_b[l, 1])
    return x2.reshape(bsz, seq, d)
```

```python
import functools
import math

import numpy as np
import jax
import jax.numpy as jnp
from jax import lax
from jax.experimental import pallas as pl
from jax.experimental.pallas import tpu as pltpu

D_MODEL = 1024
DEPTH = 4
MLSTM_HEADS = 4
MLSTM_QK_DIM = 64
MLSTM_V_DIM = 128
MLSTM_CONV = 4
DSA_HEADS = 8
DSA_HEAD_DIM = 64
DSA_LATENT = 128
IDX_HEADS = 8
IDX_DIM = 32
IDX_TOPK_MAX = 256
REL_BUCKETS = 32
REL_MAX_DIST = 128
D_FF = 2816
N_EXPERTS = 8
TOP_K = 2
D_FF_EXPERT = 3584
DN_ALPHA = (2 * DEPTH) ** 0.25
LN_EPS = 1e-5

A_QK = 2 * MLSTM_HEADS * MLSTM_QK_DIM
A_V = MLSTM_HEADS * MLSTM_V_DIM
A_GATE = 2 * MLSTM_HEADS
B_Q = DSA_HEADS * DSA_HEAD_DIM
I_Q = IDX_HEADS * IDX_DIM
PROJ_SIZES = (A_QK, A_V, A_V, A_GATE, B_Q, DSA_LATENT, I_Q, IDX_DIM, IDX_HEADS, D_MODEL, D_MODEL)

LANES = 128
SUBLANES = 8
VMEM_LIMIT = 56 * 1024 * 1024
TOK_TILE = 512
MLSTM_TILE = 256
DSA_TQ = 128
DSA_KN = 256
DSA_KF = 512
MOE_TILE = 512
MOE_FT = 512
COMB_TILE = 256
DMA_UNROLL = 8
FF_CHUNK = 256
INT_MIN = -(2 ** 31)
NEG_BIG = -1e30
M_FLOOR = -1e29
LOG2E = math.log2(math.e)
F32_MIN_NORMAL = 2.0 ** -126
VALUE_PASSES = 12

BF16 = jnp.bfloat16
F32 = jnp.float32
NT_DIMS = (((1,), (1,)), ((), ()))


def _cparams(sem):
    return pltpu.CompilerParams(dimension_semantics=sem, vmem_limit_bytes=VMEM_LIMIT)


def _dot(a, b):
    return jnp.dot(a, b, preferred_element_type=F32)


def _dot_nt(a, b):
    return lax.dot_general(a, b, NT_DIMS, preferred_element_type=F32)


def _layer_norm(r, g, b):
    mu = jnp.mean(r, axis=-1, keepdims=True)
    d = r - mu
    var = jnp.mean(d * d, axis=-1, keepdims=True)
    return d * lax.rsqrt(var + LN_EPS) * g + b


def _silu(x):
    return x * jax.nn.sigmoid(x)


IN_PROJ_OUTS = (("qk", A_QK, F32), ("av", A_V, BF16), ("ao", A_V, BF16), ("bq", B_Q, BF16),
                ("iq", I_Q, BF16), ("ga", D_MODEL, BF16), ("gb", D_MODEL, BF16),
                ("aif", LANES, F32), ("bc", LANES, F32), ("ikw", LANES, F32))
IN_PROJ_COLS = sum(w for _, w, _ in IN_PROJ_OUTS)
IKW_W_OFF = IDX_DIM


def _regroup_in_proj(w):
    offs = np.concatenate([[0], np.cumsum(PROJ_SIZES)])
    a_qk, a_v, a_o, a_if, b_q, b_c, i_q, i_k, i_w, g_a, g_b = (w[:, offs[i]:offs[i + 1]] for i in range(11))
    pad = lambda a: jnp.pad(a, ((0, 0), (0, LANES - a.shape[1])))
    groups = [a_qk, a_v, a_o, b_q, i_q, g_a, g_b, pad(a_if), b_c, pad(jnp.concatenate([i_k, i_w], axis=1))]
    return jnp.concatenate(groups, axis=1).astype(BF16)


def _in_proj_kernel(x_ref, w_ref, *out_refs):
    xb = x_ref[...].astype(BF16)
    off = 0
    for (_, width, dtype), o_ref in zip(IN_PROJ_OUTS, out_refs):
        o_ref[...] = _dot(xb, w_ref[:, off:off + width]).astype(dtype)
        off += width


def _in_proj(x2, w):
    n = x2.shape[0]
    tm = min(TOK_TILE, n)
    return pl.pallas_call(
        _in_proj_kernel,
        grid=(n // tm,),
        in_specs=[pl.BlockSpec((tm, D_MODEL), lambda i: (i, 0)),
                  pl.BlockSpec((D_MODEL, IN_PROJ_COLS), lambda i: (0, 0))],
        out_specs=[pl.BlockSpec((tm, wd), lambda i: (i, 0)) for _, wd, _ in IN_PROJ_OUTS],
        out_shape=[jax.ShapeDtypeStruct((n, wd), dt) for _, wd, dt in IN_PROJ_OUTS],
        compiler_params=_cparams(("parallel",)),
        name="in_proj",
    )(x2, w)


def _log_sigmoid(x):
    return jnp.minimum(x, 0.0) - jnp.log1p(jnp.exp(-jnp.abs(x)))


def _mlstm_kernel(qk_ref, av_ref, aif_ref, ao_ref, convw_ref, gbias_ref, ng_ref, out_ref,
                  ext_ref, ct_ref, m_ref, *, tile):
    L = tile
    H, DK, DV = MLSTM_HEADS, MLSTM_QK_DIM, MLSTM_V_DIM

    @pl.when(pl.program_id(1) == 0)
    def _():
        ext_ref[0:SUBLANES, :] = jnp.zeros((SUBLANES, A_QK), F32)
        ct_ref[...] = jnp.zeros_like(ct_ref)
        m_ref[...] = jnp.zeros_like(m_ref)

    u = qk_ref[...]
    ext_ref[SUBLANES:SUBLANES + L, :] = u
    conv = jnp.zeros((L, A_QK), F32)
    for j in range(MLSTM_CONV):
        conv = conv + ext_ref[pl.ds(SUBLANES - (MLSTM_CONV - 1) + j, L), :] * convw_ref[j:j + 1, :]
    ext_ref[0:SUBLANES, :] = u[L - SUBLANES:L, :]
    qk = _silu(conv)

    g_col = aif_ref[...] + gbias_ref[...]
    g_row = g_col.T
    row = lax.broadcasted_iota(jnp.int32, (L, L), 0)
    col = lax.broadcasted_iota(jnp.int32, (L, L), 1)
    causal = row >= col
    tril = jnp.where(causal, 1.0, 0.0).astype(F32)
    triu = jnp.where(row <= col, 1.0, 0.0).astype(F32)
    b_col = jnp.dot(tril, _log_sigmoid(g_col), preferred_element_type=F32, precision=lax.Precision.HIGHEST)
    b_row = jnp.dot(_log_sigmoid(g_row), triu, preferred_element_type=F32, precision=lax.Precision.HIGHEST)

    ones_v = jnp.ones((L, DV), BF16)
    ones_t = jnp.ones((DV, L), F32)
    for h in range(H):
        bcol = b_col[:, H + h:H + h + 1]
        icol = g_col[:, h:h + 1]
        brow = b_row[H + h:H + h + 1, :]
        irow = g_row[h:h + 1, :]
        mprev = m_ref[h:h + 1, 0:1]
        dlog = jnp.where(causal, bcol - brow + irow, -jnp.inf)
        m_t = jnp.maximum(bcol + mprev, jnp.max(dlog, axis=1, keepdims=True))
        q = qk[:, h * DK:(h + 1) * DK].astype(BF16)
        k = qk[:, H * DK + h * DK:H * DK + (h + 1) * DK] * (DK ** -0.5)
        s_w = _dot_nt(q, k.astype(BF16)) * jnp.exp(dlog - m_t)
        inter = jnp.exp(bcol + mprev - m_t)
        v_h = av_ref[:, h * DV:(h + 1) * DV]
        v_ext = jnp.concatenate([v_h, ones_v], axis=1)
        ct_st = ct_ref[h]
        ne = _dot(s_w.astype(BF16), v_ext) + inter * _dot_nt(q, ct_st.astype(BF16))
        h_out = ne[:, :DV] / jnp.maximum(jnp.abs(ne[:, DV:]), jnp.exp(-m_t))
        blast = bcol[L - 1:L, :]
        glog = blast - bcol + icol
        m_new = jnp.maximum(blast + mprev, jnp.max(glog, axis=0, keepdims=True))
        decay = jnp.exp(blast + mprev - m_new)
        w_k = k * jnp.exp(glog - m_new)
        v_ext_t = jnp.concatenate([v_h.astype(F32).T, ones_t], axis=0).astype(BF16)
        ct_ref[h] = decay * ct_st + _dot(v_ext_t, w_k.astype(BF16))
        m_ref[h:h + 1, :] = jnp.broadcast_to(m_new, (1, LANES))
        mu = jnp.mean(h_out, axis=-1, keepdims=True)
        d = h_out - mu
        hn = d * lax.rsqrt(jnp.mean(d * d, axis=-1, keepdims=True) + LN_EPS)
        y = hn * ng_ref[:, h * DV:(h + 1) * DV] * jax.nn.sigmoid(ao_ref[:, h * DV:(h + 1) * DV].astype(F32))
        out_ref[:, h * DV:(h + 1) * DV] = y.astype(BF16)


def _mlstm(qk, av, aif, ao, conv_w, gate_bias, norm_g, bsz, seq):
    tile = min(MLSTM_TILE, seq)
    nc = seq // tile
    gbias = jnp.zeros((1, LANES), F32).at[0, :A_GATE].set(gate_bias.reshape(-1))
    tok = lambda w: pl.BlockSpec((tile, w), lambda b, c: (b * nc + c, 0))
    full = lambda shape: pl.BlockSpec(shape, lambda b, c: (0,) * len(shape))
    return pl.pallas_call(
        functools.partial(_mlstm_kernel, tile=tile),
        grid=(bsz, nc),
        in_specs=[tok(A_QK), tok(A_V), tok(LANES), tok(A_V),
                  full((MLSTM_CONV, A_QK)), full((1, LANES)), full((1, A_V))],
        out_specs=tok(A_V),
        out_shape=jax.ShapeDtypeStruct((bsz * seq, A_V), BF16),
        scratch_shapes=[pltpu.VMEM((tile + SUBLANES, A_QK), F32),
                        pltpu.VMEM((MLSTM_HEADS, 2 * MLSTM_V_DIM, MLSTM_QK_DIM), F32),
                        pltpu.VMEM((SUBLANES, LANES), F32)],
        compiler_params=_cparams(("parallel", "arbitrary")),
        name="mlstm",
    )(qk, av, aif, ao, conv_w, gbias, norm_g.reshape(1, A_V))


def _t5_bucket_np(dist):
    dist = np.maximum(dist, 0)
    exact = REL_BUCKETS // 2
    log_ratio = (np.log(np.maximum(dist, 1).astype(np.float32) / np.float32(exact))
                 / np.float32(math.log(REL_MAX_DIST / exact))).astype(np.float32)
    large = np.minimum(exact + (log_ratio * np.float32(REL_BUCKETS - exact)).astype(np.int32), REL_BUCKETS - 1)
    return np.where(dist < exact, dist, large)


def _near_bias_tables(rel_bias, tq, kn, kf):
    n_off = kf // tq + 1
    t = np.arange(tq)[None, :, None]
    s = np.arange(kn)[None, None, :]
    off = (np.arange(n_off) * tq)[:, None, None]
    dist = off + t - s
    bucket = _t5_bucket_np(dist)
    assert np.all(_t5_bucket_np(np.arange(tq + 1, 4 * REL_MAX_DIST)) == REL_BUCKETS - 1)
    tab = rel_bias[bucket] - rel_bias[REL_BUCKETS - 1]
    tab = jnp.where((dist >= 0)[..., None], tab, 0.0) * LOG2E
    return tab.transpose(0, 3, 1, 2).reshape(n_off, DSA_HEADS * tq, kn).astype(F32)


def _dsa_kernel(bq_ref, iq_ref, ikwq_ref, ikw_ref, ckv_ref, tab_ref, wuk_ref, wuv_ref, out_ref,
                key_ref, smax_ref, smin_ref, npos_ref, nnon_ref, acc_ref, m_ref, ql_ref, wb_ref,
                *, tq, kn, kf, topk):
    H = DSA_HEADS
    j = pl.program_id(1)
    t0 = j * tq
    n_score = (t0 + tq - 1) // kf + 1
    n_far = jnp.maximum(t0 - tq, 0) // kf
    near_lo = n_far * (kf // kn)
    near_hi = (t0 + tq - 1) // kn + 1

    bq = bq_ref[...]
    for h in range(H):
        ql_ref[h * tq:(h + 1) * tq, :] = (_dot(bq, wuk_ref[h]) * (DSA_HEAD_DIM ** -0.5 * LOG2E)).astype(BF16)
    w_idx = ikwq_ref[:, IKW_W_OFF:IKW_W_OFF + IDX_HEADS] * ((IDX_HEADS * IDX_DIM) ** -0.5)
    for h in range(IDX_HEADS):
        wb_ref[h] = jnp.broadcast_to(w_idx[:, h:h + 1], (tq, kf))
    iq = iq_ref[...]
    qis = [iq[:, h * IDX_DIM:(h + 1) * IDX_DIM] for h in range(IDX_HEADS)]
    tpos = t0 + lax.broadcasted_iota(jnp.int32, (tq, kf), 0)
    lpos = lax.broadcasted_iota(jnp.int32, (tq, kf), 1)

    ncol_f = kf // LANES
    tile_f = lambda a: jnp.concatenate([a] * ncol_f, axis=1)

    smax_ref[...] = jnp.full(smax_ref.shape, -jnp.inf, F32)
    smin_ref[...] = jnp.full(smin_ref.shape, jnp.inf, F32)
    npos_ref[...] = jnp.zeros_like(npos_ref)
    nnon_ref[...] = jnp.zeros_like(nnon_ref)

    def score_chunk(c, masked):
        s0 = pl.multiple_of(c * kf, kf)
        kidx = ikw_ref[pl.ds(s0, kf), :][:, 0:IDX_DIM].astype(BF16)
        sc = jnp.zeros((tq, kf), F32)
        for h in range(IDX_HEADS):
            sc = sc + wb_ref[h] * jnp.maximum(_dot_nt(qis[h], kidx), 0.0)
        sc = jnp.where(jnp.abs(sc) < F32_MIN_NORMAL, 0.0, sc)
        bits = lax.bitcast_convert_type(sc, jnp.int32)
        key = bits ^ ((bits >> 31) & 0x7FFFFFFF)
        sc_lo = sc
        if masked:
            causal = s0 + lpos <= tpos
            key = jnp.where(causal, key, INT_MIN)
            sc = jnp.where(causal, sc, -jnp.inf)
            sc_lo = jnp.where(causal, sc_lo, jnp.inf)
        key_ref[:, pl.ds(s0, kf)] = key
        smax = smax_ref[...]
        smin = smin_ref[...]
        npos = npos_ref[...]
        nnon = nnon_ref[...]
        for g in range(ncol_f):
            cols = slice(g * LANES, (g + 1) * LANES)
            smax = jnp.maximum(smax, sc[:, cols])
            smin = jnp.minimum(smin, sc_lo[:, cols])
            npos = npos + jnp.where(sc[:, cols] > 0.0, 1.0, 0.0)
            nnon = nnon + jnp.where(sc[:, cols] >= 0.0, 1.0, 0.0)
        smax_ref[...] = smax
        smin_ref[...] = smin
        npos_ref[...] = npos
        nnon_ref[...] = nnon

    def score_pair(i, carry):
        score_chunk(2 * i, False)
        score_chunk(2 * i + 1, False)
        return carry

    lax.fori_loop(0, (n_score - 1) // 2, score_pair, 0)

    @pl.when((n_score - 1) % 2 == 1)
    def _():
        score_chunk(n_score - 2, False)

    score_chunk(n_score - 1, True)

    def count_ge(cand):
        cand_w = tile_f(cand)

        def body(c, acc):
            s0 = pl.multiple_of(c * kf, kf)
            ind = jnp.where(key_ref[:, pl.ds(s0, kf)] >= cand_w, 1.0, 0.0)
            for g in range(ncol_f):
                acc = acc + ind[:, g * LANES:(g + 1) * LANES]
            return acc
        acc = lax.fori_loop(0, n_score, body, jnp.zeros((tq, LANES), F32))
        return jnp.sum(acc, axis=1, keepdims=True)

    n_valid = (t0 + 1 + lax.broadcasted_iota(jnp.int32, (tq, LANES), 0)).astype(F32)
    enough = n_valid >= topk
    flip = lambda a: a ^ ((a >> 31) & 0x7FFFFFFF)
    to_key = lambda v: flip(lax.bitcast_convert_type(v, jnp.int32))
    to_val = lambda k: lax.bitcast_convert_type(flip(k), F32)
    row_all = lambda a: jnp.broadcast_to(a, (tq, LANES))
    lo0 = to_key(row_all(jnp.min(smin_ref[...], axis=1, keepdims=True)))
    hi0 = to_key(row_all(jnp.max(smax_ref[...], axis=1, keepdims=True))) + 1
    n_pos = row_all(jnp.sum(npos_ref[...], axis=1, keepdims=True))
    n_non = row_all(jnp.sum(nnon_ref[...], axis=1, keepdims=True))
    pos_thr = n_pos >= topk
    zero_thr = jnp.logical_and(n_non >= topk, jnp.logical_not(pos_thr))
    cnt0 = jnp.where(pos_thr, n_pos, jnp.where(zero_thr, n_non, n_valid))
    lo0 = jnp.where(pos_thr, 1, jnp.where(zero_thr, 0, lo0))
    hi0 = jnp.where(pos_thr, jnp.where(n_pos == topk, 2, hi0), jnp.where(zero_thr, 1, 0))

    def any_open(lo, hi):
        return jnp.max(jnp.where(jnp.logical_and(enough, hi > lo + 1), 1.0, 0.0))

    def bisect(state):
        lo, hi, cnt_lo, _, it = state
        mid_v = to_key(0.5 * to_val(lo) + 0.5 * to_val(hi))
        by_value = jnp.logical_and(it < VALUE_PASSES, jnp.logical_and(mid_v > lo, mid_v < hi))
        mid = jnp.where(by_value, mid_v, (lo >> 1) + (hi >> 1) + (lo & hi & 1))
        cnt = count_ge(mid)
        ge = cnt >= topk
        lo = jnp.where(ge, mid, lo)
        cnt_lo = jnp.where(ge, cnt, cnt_lo)
        hi = jnp.where(cnt == topk, mid + 1, jnp.where(ge, hi, mid))
        return lo, hi, cnt_lo, any_open(lo, hi), it + 1

    lo, _, cnt_lo, _, _ = lax.while_loop(lambda s: s[3] > 0.0, bisect,
                                         (lo0, hi0, cnt0, any_open(lo0, hi0), jnp.int32(0)))
    thr = jnp.where(enough, lo, INT_MIN + 1)

    def count_key(pred):
        def body(c, acc):
            s0 = pl.multiple_of(c * kf, kf)
            ind = jnp.where(pred(key_ref[:, pl.ds(s0, kf)]), 1.0, 0.0)
            for g in range(ncol_f):
                acc = acc + ind[:, g * LANES:(g + 1) * LANES]
            return acc
        acc = lax.fori_loop(0, n_score, body, jnp.zeros((tq, LANES), F32))
        return jnp.sum(acc, axis=1, keepdims=True)

    thr_f = tile_f(thr)
    surplus = jnp.where(enough, cnt_lo - topk, 0.0)

    @pl.when(jnp.max(surplus) > 0.0)
    def _():
        allowed = topk - count_key(lambda k: k > thr_f)
        r = lax.broadcasted_iota(jnp.int32, (kf, kf), 0)
        cidx = lax.broadcasted_iota(jnp.int32, (kf, kf), 1)
        upper = jnp.where(r <= cidx, 1.0, 0.0).astype(BF16)

        def body(c, seen):
            s0 = pl.multiple_of(c * kf, kf)
            key = key_ref[:, pl.ds(s0, kf)]
            tie = key == thr_f
            tie_f = jnp.where(tie, 1.0, 0.0)
            rank = _dot(tie_f.astype(BF16), upper) + seen
            key_ref[:, pl.ds(s0, kf)] = jnp.where(jnp.logical_and(tie, rank > allowed), thr_f - 1, key)
            return seen + jnp.sum(tie_f, axis=1, keepdims=True)
        lax.fori_loop(0, n_score, body, jnp.zeros((tq, 1), F32))

    m_ref[...] = jnp.full(m_ref.shape, M_FLOOR, F32)
    acc_ref[...] = jnp.zeros_like(acc_ref)

    def attend(s0, width, table_idx):
        ncol = width // LANES
        kv = ckv_ref[pl.ds(s0, width), :]
        kv_ext = jnp.concatenate([kv, jnp.ones((width, DSA_LATENT), BF16)], axis=1)
        thr_w = jnp.concatenate([thr] * ncol, axis=1)
        madd = jnp.where(key_ref[:, pl.ds(s0, width)] >= thr_w, 0.0, NEG_BIG)
        for h in range(H):
            rows = slice(h * tq, (h + 1) * tq)
            x = _dot_nt(ql_ref[rows, :], kv) + madd
            if table_idx is not None:
                x = x + tab_ref[table_idx, rows, :]
            m_old = m_ref[rows, :]
            m_new = jnp.maximum(m_old, jnp.max(x, axis=1, keepdims=True))
            alpha = jnp.exp2(m_old - m_new)
            p = jnp.exp2(x - jnp.concatenate([m_new] * ncol, axis=1))
            acc_ref[rows, :] = jnp.concatenate([alpha, alpha], axis=1) * acc_ref[rows, :] + _dot(p.astype(BF16), kv_ext)
            m_ref[rows, :] = m_new

    def far_pair(i, carry):
        attend(pl.multiple_of(2 * i * kf, kf), kf, None)
        attend(pl.multiple_of((2 * i + 1) * kf, kf), kf, None)
        return carry

    def near_chunk(c, carry):
        s0 = pl.multiple_of(c * kn, kn)
        attend(s0, kn, (t0 - s0) // tq)
        return carry

    lax.fori_loop(0, n_far // 2, far_pair, 0)

    @pl.when(n_far % 2 == 1)
    def _():
        attend(pl.multiple_of((n_far - 1) * kf, kf), kf, None)

    lax.fori_loop(near_lo, near_hi, near_chunk, 0)

    y = jnp.zeros((tq, B_Q), F32)
    for h in range(H):
        a = acc_ref[h * tq:(h + 1) * tq, :]
        o = a[:, :DSA_LATENT] / a[:, DSA_LATENT:]
        y = y + _dot(o.astype(BF16), wuv_ref[h])
    out_ref[...] = y.astype(BF16)


def _dsa_tiles(seq):
    tq = min(DSA_TQ, seq)
    return tq, min(DSA_KN, seq), min(DSA_KF, seq)


def _dsa(bq, iq, ikw, ckv, tab, wuk_pad, wuv_pad, bsz, seq):
    tq, kn, kf = _dsa_tiles(seq)
    nq = seq // tq
    topk = min(IDX_TOPK_MAX, seq // 4)
    n_off = kf // tq + 1
    qblk = lambda w: pl.BlockSpec((tq, w), lambda b, j: (b * nq + j, 0))
    seqblk = lambda w: pl.BlockSpec((seq, w), lambda b, j: (b, 0))
    full = lambda shape: pl.BlockSpec(shape, lambda b, j: (0,) * len(shape))
    return pl.pallas_call(
        functools.partial(_dsa_kernel, tq=tq, kn=kn, kf=kf, topk=topk),
        grid=(bsz, nq),
        in_specs=[qblk(B_Q), qblk(I_Q), qblk(LANES), seqblk(LANES), seqblk(DSA_LATENT),
                  full((n_off, DSA_HEADS * tq, kn)), full((DSA_HEADS, B_Q, DSA_LATENT)),
                  full((DSA_HEADS, DSA_LATENT, B_Q))],
        out_specs=qblk(B_Q),
        out_shape=jax.ShapeDtypeStruct((bsz * seq, B_Q), BF16),
        scratch_shapes=[pltpu.VMEM((tq, seq), jnp.int32),
                        pltpu.VMEM((tq, LANES), F32),
                        pltpu.VMEM((tq, LANES), F32),
                        pltpu.VMEM((tq, LANES), F32),
                        pltpu.VMEM((tq, LANES), F32),
                        pltpu.VMEM((DSA_HEADS * tq, 2 * DSA_LATENT), F32),
                        pltpu.VMEM((DSA_HEADS * tq, LANES), F32),
                        pltpu.VMEM((DSA_HEADS * tq, DSA_LATENT), BF16),
                        pltpu.VMEM((IDX_HEADS, tq, kf), F32)],
        compiler_params=_cparams(("parallel", "arbitrary")),
        name="dsa",
    )(bq, iq, ikw, ikw, ckv, tab, wuk_pad, wuv_pad)


def _kv_norm_kernel(bc_ref, g_ref, out_ref):
    x = bc_ref[...]
    out_ref[...] = (x * lax.rsqrt(jnp.mean(x * x, axis=-1, keepdims=True) + LN_EPS) * g_ref[...]).astype(BF16)


def _kv_norm(bc, g):
    n = bc.shape[0]
    tm = min(4 * TOK_TILE, n)
    return pl.pallas_call(
        _kv_norm_kernel,
        grid=(n // tm,),
        in_specs=[pl.BlockSpec((tm, DSA_LATENT), lambda i: (i, 0)), pl.BlockSpec((1, DSA_LATENT), lambda i: (0, 0))],
        out_specs=pl.BlockSpec((tm, DSA_LATENT), lambda i: (i, 0)),
        out_shape=jax.ShapeDtypeStruct((n, DSA_LATENT), BF16),
        compiler_params=_cparams(("parallel",)),
        name="kv_norm",
    )(bc, g.reshape(1, DSA_LATENT))


def _merge_kernel(x_ref, ya_ref, yb_ref, ga_ref, gb_ref, wa_ref, wb_ref, wo_ref, g_ref, b_ref, out_ref):
    merged = (jax.nn.sigmoid(ga_ref[...].astype(F32)) * _dot(ya_ref[...], wa_ref[...])
              + jax.nn.sigmoid(gb_ref[...].astype(F32)) * _dot(yb_ref[...], wb_ref[...]))
    mix = _dot(merged.astype(BF16), wo_ref[...])
    out_ref[...] = _layer_norm(DN_ALPHA * x_ref[...] + mix, g_ref[...], b_ref[...])


def _merge(x2, ya, yb, ga, gb, wa, wb, wo, g, b):
    n = x2.shape[0]
    tm = min(TOK_TILE, n)
    tok = lambda w: pl.BlockSpec((tm, w), lambda i: (i, 0))
    full = lambda shape: pl.BlockSpec(shape, lambda i: (0,) * len(shape))
    return pl.pallas_call(
        _merge_kernel,
        grid=(n // tm,),
        in_specs=[tok(D_MODEL), tok(A_V), tok(B_Q), tok(D_MODEL), tok(D_MODEL),
                  full((A_V, D_MODEL)), full((B_Q, D_MODEL)), full((D_MODEL, D_MODEL)),
                  full((1, D_MODEL)), full((1, D_MODEL))],
        out_specs=tok(D_MODEL),
        out_shape=jax.ShapeDtypeStruct((n, D_MODEL), F32),
        compiler_params=_cparams(("parallel",)),
        name="merge_ln",
    )(x2, ya, yb, ga, gb, wa, wb, wo, g.reshape(1, D_MODEL), b.reshape(1, D_MODEL))


def _dense_ffn_kernel(x_ref, wg_ref, wu_ref, wd_ref, g_ref, b_ref, out_ref):
    x = x_ref[...]
    xb = x.astype(BF16)
    f = jnp.zeros(x.shape, F32)
    for c in range(D_FF // FF_CHUNK):
        cols = slice(c * FF_CHUNK, (c + 1) * FF_CHUNK)
        hdn = _silu(_dot(xb, wg_ref[:, cols])) * _dot(xb, wu_ref[:, cols])
        f = f + _dot(hdn.astype(BF16), wd_ref[cols, :])
    out_ref[...] = _layer_norm(DN_ALPHA * x + f, g_ref[...], b_ref[...])


def _dense_ffn(x2, wg, wu, wd, g, b):
    n = x2.shape[0]
    tm = min(TOK_TILE, n)
    tok = pl.BlockSpec((tm, D_MODEL), lambda i: (i, 0))
    full = lambda shape: pl.BlockSpec(shape, lambda i: (0,) * len(shape))
    return pl.pallas_call(
        _dense_ffn_kernel,
        grid=(n // tm,),
        in_specs=[tok, full((D_MODEL, D_FF)), full((D_MODEL, D_FF)), full((D_FF, D_MODEL)),
                  full((1, D_MODEL)), full((1, D_MODEL))],
        out_specs=tok,
        out_shape=jax.ShapeDtypeStruct((n, D_MODEL), F32),
        compiler_params=_cparams(("parallel",)),
        name="dense_ffn_ln",
    )(x2, wg, wu, wd, g.reshape(1, D_MODEL), b.reshape(1, D_MODEL))


def _router_kernel(x_ref, rw_ref, e_ref, g_ref):
    logits = lax.dot_general(rw_ref[...], x_ref[...], NT_DIMS, preferred_element_type=F32,
                             precision=lax.Precision.HIGHEST)
    eidx = lax.broadcasted_iota(jnp.int32, logits.shape, 0)
    m1 = jnp.max(logits, axis=0, keepdims=True)
    i1 = jnp.min(jnp.where(logits == m1, eidx, N_EXPERTS), axis=0, keepdims=True)
    rest = jnp.where(eidx == i1, -jnp.inf, logits)
    m2 = jnp.max(rest, axis=0, keepdims=True)
    i2 = jnp.min(jnp.where(rest == m2, eidx, N_EXPERTS), axis=0, keepdims=True)
    e2 = jnp.exp(m2 - m1)
    den = 1.0 + e2
    e_ref[0:1, :] = i1
    e_ref[1:2, :] = i2
    g_ref[0:1, :] = 1.0 / den
    g_ref[1:2, :] = e2 / den


def _router(x2, rw_t):
    n = x2.shape[0]
    tm = min(TOK_TILE, n)
    return pl.pallas_call(
        _router_kernel,
        grid=(n // tm,),
        in_specs=[pl.BlockSpec((tm, D_MODEL), lambda i: (i, 0)), pl.BlockSpec((N_EXPERTS, D_MODEL), lambda i: (0, 0))],
        out_specs=[pl.BlockSpec((TOP_K, tm), lambda i: (0, i)), pl.BlockSpec((TOP_K, tm), lambda i: (0, i))],
        out_shape=[jax.ShapeDtypeStruct((TOP_K, n), jnp.int32), jax.ShapeDtypeStruct((TOP_K, n), F32)],
        compiler_params=_cparams(("parallel",)),
        name="router",
    )(x2, rw_t)


def _row_copy(src, src_row, dst, dst_row, sem):
    return pltpu.make_async_copy(src.at[pl.ds(src_row, 1)], dst.at[pl.ds(dst_row, 1)], sem)


def _dispatch_kernel(d0_ref, d1_ref, x_ref, xs_init_hbm, xs_hbm, sem, *, tile):
    del xs_init_hbm

    def start(r, c):
        _row_copy(x_ref, r, xs_hbm, d0_ref[0, r], sem).start()
        _row_copy(x_ref, r, xs_hbm, d1_ref[0, r], sem).start()
        return c
    lax.fori_loop(0, tile, start, 0, unroll=DMA_UNROLL)

    def wait(r, c):
        _row_copy(x_ref, r, xs_hbm, 0, sem).wait()
        _row_copy(x_ref, r, xs_hbm, 0, sem).wait()
        return c
    lax.fori_loop(0, tile, wait, 0, unroll=DMA_UNROLL)


def _dispatch(x2, pos, total):
    n = x2.shape[0]
    tile = min(COMB_TILE, n)
    nb = n // tile
    smem = pl.BlockSpec((None, 1, tile), lambda i: (i, 0, 0), memory_space=pltpu.SMEM)
    return pl.pallas_call(
        functools.partial(_dispatch_kernel, tile=tile),
        grid=(nb,),
        in_specs=[smem, smem, pl.BlockSpec((tile, D_MODEL), lambda i: (i, 0)), pl.BlockSpec(memory_space=pl.ANY)],
        out_specs=pl.BlockSpec(memory_space=pl.ANY),
        out_shape=jax.ShapeDtypeStruct((total, D_MODEL), F32),
        scratch_shapes=[pltpu.SemaphoreType.DMA(())],
        input_output_aliases={3: 0},
        compiler_params=_cparams(("arbitrary",)),
        name="moe_dispatch",
    )(pos[:, 0].reshape(nb, 1, tile), pos[:, 1].reshape(nb, 1, tile), x2, jnp.zeros((total, D_MODEL), F32))


def _expert_kernel(blk_e_ref, nused_ref, xs_ref, wg_ref, wu_ref, wd_ref, y_ref, xb, acc):
    i = pl.program_id(0)
    f = pl.program_id(1)
    used = i < nused_ref[0]
    last = f == pl.num_programs(1) - 1

    @pl.when(jnp.logical_and(used, f == 0))
    def _():
        xb[...] = xs_ref[...].astype(BF16)
        acc[...] = jnp.zeros_like(acc)

    @pl.when(used)
    def _():
        x = xb[...]
        hdn = _silu(_dot(x, wg_ref[...])) * _dot(x, wu_ref[...])
        acc[...] += _dot(hdn.astype(BF16), wd_ref[...])

    @pl.when(jnp.logical_and(used, last))
    def _():
        y_ref[...] = acc[...]

    @pl.when(jnp.logical_and(jnp.logical_not(used), last))
    def _():
        y_ref[...] = jnp.zeros_like(y_ref)


def _experts(xs, blk_e, nused, wg, wu, wd):
    total = xs.shape[0]
    tile = MOE_TILE
    nblk = total // tile
    nf = D_FF_EXPERT // MOE_FT
    grid_spec = pltpu.PrefetchScalarGridSpec(
        num_scalar_prefetch=2,
        grid=(nblk, nf),
        in_specs=[pl.BlockSpec((tile, D_MODEL), lambda i, f, be, nu: (i, 0)),
                  pl.BlockSpec((None, D_MODEL, MOE_FT), lambda i, f, be, nu: (be[i], 0, f)),
                  pl.BlockSpec((None, D_MODEL, MOE_FT), lambda i, f, be, nu: (be[i], 0, f)),
                  pl.BlockSpec((None, MOE_FT, D_MODEL), lambda i, f, be, nu: (be[i], f, 0))],
        out_specs=pl.BlockSpec((tile, D_MODEL), lambda i, f, be, nu: (i, 0)),
        scratch_shapes=[pltpu.VMEM((tile, D_MODEL), BF16), pltpu.VMEM((tile, D_MODEL), F32)],
    )
    return pl.pallas_call(
        _expert_kernel,
        grid_spec=grid_spec,
        out_shape=jax.ShapeDtypeStruct((total, D_MODEL), F32),
        compiler_params=_cparams(("arbitrary", "arbitrary")),
        name="experts",
    )(blk_e, nused, xs, wg, wu, wd)


def _combine_kernel(p0_ref, p1_ref, y_hbm, x_ref, gate_ref, g_ref, b_ref, out_ref, buf0, buf1, sem, *, tile):
    def start(r, c):
        _row_copy(y_hbm, p0_ref[0, r], buf0, r, sem).start()
        _row_copy(y_hbm, p1_ref[0, r], buf1, r, sem).start()
        return c
    lax.fori_loop(0, tile, start, 0, unroll=DMA_UNROLL)

    def wait(r, c):
        _row_copy(y_hbm, 0, buf0, r, sem).wait()
        _row_copy(y_hbm, 0, buf1, r, sem).wait()
        return c
    lax.fori_loop(0, tile, wait, 0, unroll=DMA_UNROLL)
    f = buf0[...] * gate_ref[:, 0:1] + buf1[...] * gate_ref[:, 1:2]
    out_ref[...] = _layer_norm(DN_ALPHA * x_ref[...] + f, g_ref[...], b_ref[...])


def _combine(y, pos, gates, x2, g, b):
    n = x2.shape[0]
    tile = min(COMB_TILE, n)
    nb = n // tile
    smem = pl.BlockSpec((None, 1, tile), lambda i: (i, 0, 0), memory_space=pltpu.SMEM)
    full = pl.BlockSpec((1, D_MODEL), lambda i: (0, 0))
    return pl.pallas_call(
        functools.partial(_combine_kernel, tile=tile),
        grid=(nb,),
        in_specs=[smem, smem, pl.BlockSpec(memory_space=pl.ANY),
                  pl.BlockSpec((tile, D_MODEL), lambda i: (i, 0)),
                  pl.BlockSpec((tile, TOP_K), lambda i: (i, 0)), full, full],
        out_specs=pl.BlockSpec((tile, D_MODEL), lambda i: (i, 0)),
        out_shape=jax.ShapeDtypeStruct((n, D_MODEL), F32),
        scratch_shapes=[pltpu.VMEM((tile, D_MODEL), F32), pltpu.VMEM((tile, D_MODEL), F32),
                        pltpu.SemaphoreType.DMA(())],
        compiler_params=_cparams(("arbitrary",)),
        name="moe_combine_ln",
    )(pos[:, 0].reshape(nb, 1, tile), pos[:, 1].reshape(nb, 1, tile), y, x2, gates,
      g.reshape(1, D_MODEL), b.reshape(1, D_MODEL))


def _moe(x2, router_w, wg, wu, wd, g, b):
    n = x2.shape[0]
    top_e, gates = _router(x2, router_w.T)
    flat_e = top_e.T.reshape(-1)
    onehot = (flat_e[:, None] == jnp.arange(N_EXPERTS, dtype=jnp.int32)[None, :]).astype(jnp.int32)
    rank = jnp.sum((jnp.cumsum(onehot, axis=0) - onehot) * onehot, axis=1)
    counts = jnp.sum(onehot, axis=0)
    padded = ((counts + MOE_TILE - 1) // MOE_TILE) * MOE_TILE
    pad_end = jnp.cumsum(padded)
    pad_start = pad_end - padded
    pos = (pad_start[flat_e] + rank).astype(jnp.int32).reshape(n, TOP_K)
    total = n * TOP_K + N_EXPERTS * MOE_TILE
    nblk = total // MOE_TILE
    blk_e = jnp.minimum(jnp.searchsorted(pad_end, jnp.arange(nblk, dtype=jnp.int32) * MOE_TILE, side='right'),
                        N_EXPERTS - 1).astype(jnp.int32)
    nused = (pad_end[-1:] // MOE_TILE).astype(jnp.int32)
    xs = _dispatch(x2, pos, total)
    y = _experts(xs, blk_e, nused, wg, wu, wd)
    return _combine(y, pos, gates.T, x2, g, b)


def _pad_head_proj(w_uk, w_uv):
    H, DH, DC = DSA_HEADS, DSA_HEAD_DIM, DSA_LATENT
    eye = jnp.eye(H, dtype=w_uk.dtype)
    wuk_pad = jnp.einsum('hg,hdc->hgdc', eye, w_uk).reshape(H, H * DH, DC)
    wuv_pad = jnp.einsum('hg,hcd->hcgd', eye, w_uv).reshape(H, DC, H * DH)
    return wuk_pad.astype(BF16), wuv_pad.astype(BF16)


def kernel(x, w_in, mlstm_conv_w, mlstm_gate_bias, mlstm_norm_g, dsa_kv_norm_g, dsa_w_uk, dsa_w_uv, rel_bias,
           w_branch_a, w_branch_b, w_out, ln_g, ln_b, dense_w_gate, dense_w_up, dense_w_down, router_w,
           expert_w_gate, expert_w_up, expert_w_down):
    bsz, seq, d = x.shape
    n = bsz * seq
    x2 = x.reshape(n, d)
    tab = _near_bias_tables(rel_bias, *_dsa_tiles(seq))
    for l in range(DEPTH):
        p = dict(zip([nm for nm, _, _ in IN_PROJ_OUTS], _in_proj(x2, _regroup_in_proj(w_in[l]))))
        ya = _mlstm(p["qk"], p["av"], p["aif"], p["ao"], mlstm_conv_w[l], mlstm_gate_bias[l], mlstm_norm_g[l],
                    bsz, seq)
        ckv = _kv_norm(p["bc"], dsa_kv_norm_g[l])
        wuk_pad, wuv_pad = _pad_head_proj(dsa_w_uk[l], dsa_w_uv[l])
        yb = _dsa(p["bq"], p["iq"], p["ikw"], ckv, tab, wuk_pad, wuv_pad, bsz, seq)
        x2 = _merge(x2, ya, yb, p["ga"], p["gb"], w_branch_a[l].astype(BF16), w_branch_b[l].astype(BF16),
                    w_out[l].astype(BF16), ln_g[l, 0], ln_b[l, 0])
        j = l // 2
        if l % 2 == 0:
            x2 = _dense_ffn(x2, dense_w_gate[j].astype(BF16), dense_w_up[j].astype(BF16),
                            dense_w_down[j].astype(BF16), ln_g[l, 1], ln_b[l, 1])
        else:
            x2 = _moe(x2, router_w[j], expert_w_gate[j].astype(BF16), expert_w_up[j].astype(BF16),
                      expert_w_down[j].astype(BF16), ln_g[l, 1], ln_b[l, 1])
    return x2.reshape(bsz, seq, d)
```

```python
import functools
import math

import numpy as np
import jax
import jax.numpy as jnp
from jax import lax
from jax.experimental import pallas as pl
from jax.experimental.pallas import tpu as pltpu

D_MODEL = 1024
DEPTH = 4
MLSTM_HEADS = 4
MLSTM_QK_DIM = 64
MLSTM_V_DIM = 128
MLSTM_CONV = 4
DSA_HEADS = 8
DSA_HEAD_DIM = 64
DSA_LATENT = 128
IDX_HEADS = 8
IDX_DIM = 32
IDX_TOPK_MAX = 256
REL_BUCKETS = 32
REL_MAX_DIST = 128
D_FF = 2816
N_EXPERTS = 8
TOP_K = 2
D_FF_EXPERT = 3584
DN_ALPHA = (2 * DEPTH) ** 0.25
LN_EPS = 1e-5

A_QK = 2 * MLSTM_HEADS * MLSTM_QK_DIM
A_V = MLSTM_HEADS * MLSTM_V_DIM
A_GATE = 2 * MLSTM_HEADS
B_Q = DSA_HEADS * DSA_HEAD_DIM
I_Q = IDX_HEADS * IDX_DIM
PROJ_SIZES = (A_QK, A_V, A_V, A_GATE, B_Q, DSA_LATENT, I_Q, IDX_DIM, IDX_HEADS, D_MODEL, D_MODEL)

LANES = 128
SUBLANES = 8
VMEM_LIMIT = 56 * 1024 * 1024
TOK_TILE = 512
MLSTM_TILE = 256
DSA_TQ = 128
DSA_KN = 256
DSA_KF = 512
MOE_TILE = 512
MOE_FT = 896
COMB_TILE = 256
DMA_UNROLL = 8
FF_CHUNK = 256
INT_MIN = -(2 ** 31)
NEG_BIG = -1e30
M_FLOOR = -1e29
LOG2E = math.log2(math.e)
F32_MIN_NORMAL = 2.0 ** -126
VALUE_PASSES = 12

BF16 = jnp.bfloat16
F32 = jnp.float32
NT_DIMS = (((1,), (1,)), ((), ()))


def _cparams(sem):
    return pltpu.CompilerParams(dimension_semantics=sem, vmem_limit_bytes=VMEM_LIMIT)


def _dot(a, b):
    return jnp.dot(a, b, preferred_element_type=F32)


def _dot_nt(a, b):
    return lax.dot_general(a, b, NT_DIMS, preferred_element_type=F32)


def _layer_norm(r, g, b):
    mu = jnp.mean(r, axis=-1, keepdims=True)
    d = r - mu
    var = jnp.mean(d * d, axis=-1, keepdims=True)
    return d * lax.rsqrt(var + LN_EPS) * g + b


def _silu(x):
    return x * jax.nn.sigmoid(x)


IN_PROJ_OUTS = (("qk", A_QK, F32), ("av", A_V, BF16), ("ao", A_V, BF16), ("bq", B_Q, BF16),
                ("iq", I_Q, BF16), ("ga", D_MODEL, BF16), ("gb", D_MODEL, BF16),
                ("aif", LANES, F32), ("bc", LANES, F32), ("ikw", LANES, F32))
IN_PROJ_COLS = sum(w for _, w, _ in IN_PROJ_OUTS)
IKW_W_OFF = IDX_DIM


def _regroup_in_proj(w):
    offs = np.concatenate([[0], np.cumsum(PROJ_SIZES)])
    a_qk, a_v, a_o, a_if, b_q, b_c, i_q, i_k, i_w, g_a, g_b = (w[:, offs[i]:offs[i + 1]] for i in range(11))
    pad = lambda a: jnp.pad(a, ((0, 0), (0, LANES - a.shape[1])))
    groups = [a_qk, a_v, a_o, b_q, i_q, g_a, g_b, pad(a_if), b_c, pad(jnp.concatenate([i_k, i_w], axis=1))]
    return jnp.concatenate(groups, axis=1).astype(BF16)


def _in_proj_kernel(x_ref, w_ref, *out_refs):
    xb = x_ref[...].astype(BF16)
    off = 0
    for (_, width, dtype), o_ref in zip(IN_PROJ_OUTS, out_refs):
        o_ref[...] = _dot(xb, w_ref[:, off:off + width]).astype(dtype)
        off += width


def _in_proj(x2, w):
    n = x2.shape[0]
    tm = min(TOK_TILE, n)
    return pl.pallas_call(
        _in_proj_kernel,
        grid=(n // tm,),
        in_specs=[pl.BlockSpec((tm, D_MODEL), lambda i: (i, 0)),
                  pl.BlockSpec((D_MODEL, IN_PROJ_COLS), lambda i: (0, 0))],
        out_specs=[pl.BlockSpec((tm, wd), lambda i: (i, 0)) for _, wd, _ in IN_PROJ_OUTS],
        out_shape=[jax.ShapeDtypeStruct((n, wd), dt) for _, wd, dt in IN_PROJ_OUTS],
        compiler_params=_cparams(("parallel",)),
        name="in_proj",
    )(x2, w)


def _log_sigmoid(x):
    return jnp.minimum(x, 0.0) - jnp.log1p(jnp.exp(-jnp.abs(x)))


def _mlstm_kernel(qk_ref, av_ref, aif_ref, ao_ref, convw_ref, gbias_ref, ng_ref, out_ref,
                  ext_ref, ct_ref, m_ref, *, tile):
    L = tile
    H, DK, DV = MLSTM_HEADS, MLSTM_QK_DIM, MLSTM_V_DIM

    @pl.when(pl.program_id(1) == 0)
    def _():
        ext_ref[0:SUBLANES, :] = jnp.zeros((SUBLANES, A_QK), F32)
        ct_ref[...] = jnp.zeros_like(ct_ref)
        m_ref[...] = jnp.zeros_like(m_ref)

    u = qk_ref[...]
    ext_ref[SUBLANES:SUBLANES + L, :] = u
    conv = jnp.zeros((L, A_QK), F32)
    for j in range(MLSTM_CONV):
        conv = conv + ext_ref[pl.ds(SUBLANES - (MLSTM_CONV - 1) + j, L), :] * convw_ref[j:j + 1, :]
    ext_ref[0:SUBLANES, :] = u[L - SUBLANES:L, :]
    qk = _silu(conv)

    g_col = aif_ref[...] + gbias_ref[...]
    g_row = g_col.T
    row = lax.broadcasted_iota(jnp.int32, (L, L), 0)
    col = lax.broadcasted_iota(jnp.int32, (L, L), 1)
    causal = row >= col
    tril = jnp.where(causal, 1.0, 0.0).astype(F32)
    triu = jnp.where(row <= col, 1.0, 0.0).astype(F32)
    b_col = jnp.dot(tril, _log_sigmoid(g_col), preferred_element_type=F32, precision=lax.Precision.HIGHEST)
    b_row = jnp.dot(_log_sigmoid(g_row), triu, preferred_element_type=F32, precision=lax.Precision.HIGHEST)

    ones_v = jnp.ones((L, DV), BF16)
    ones_t = jnp.ones((DV, L), F32)
    for h in range(H):
        bcol = b_col[:, H + h:H + h + 1]
        icol = g_col[:, h:h + 1]
        brow = b_row[H + h:H + h + 1, :]
        irow = g_row[h:h + 1, :]
        mprev = m_ref[h:h + 1, 0:1]
        dlog = jnp.where(causal, bcol - brow + irow, -jnp.inf)
        m_t = jnp.maximum(bcol + mprev, jnp.max(dlog, axis=1, keepdims=True))
        q = qk[:, h * DK:(h + 1) * DK].astype(BF16)
        k = qk[:, H * DK + h * DK:H * DK + (h + 1) * DK] * (DK ** -0.5)
        s_w = _dot_nt(q, k.astype(BF16)) * jnp.exp(dlog - m_t)
        inter = jnp.exp(bcol + mprev - m_t)
        v_h = av_ref[:, h * DV:(h + 1) * DV]
        v_ext = jnp.concatenate([v_h, ones_v], axis=1)
        ct_st = ct_ref[h]
        ne = _dot(s_w.astype(BF16), v_ext) + inter * _dot_nt(q, ct_st.astype(BF16))
        h_out = ne[:, :DV] / jnp.maximum(jnp.abs(ne[:, DV:]), jnp.exp(-m_t))
        blast = bcol[L - 1:L, :]
        glog = blast - bcol + icol
        m_new = jnp.maximum(blast + mprev, jnp.max(glog, axis=0, keepdims=True))
        decay = jnp.exp(blast + mprev - m_new)
        w_k = k * jnp.exp(glog - m_new)
        v_ext_t = jnp.concatenate([v_h.astype(F32).T, ones_t], axis=0).astype(BF16)
        ct_ref[h] = decay * ct_st + _dot(v_ext_t, w_k.astype(BF16))
        m_ref[h:h + 1, :] = jnp.broadcast_to(m_new, (1, LANES))
        mu = jnp.mean(h_out, axis=-1, keepdims=True)
        d = h_out - mu
        hn = d * lax.rsqrt(jnp.mean(d * d, axis=-1, keepdims=True) + LN_EPS)
        y = hn * ng_ref[:, h * DV:(h + 1) * DV] * jax.nn.sigmoid(ao_ref[:, h * DV:(h + 1) * DV].astype(F32))
        out_ref[:, h * DV:(h + 1) * DV] = y.astype(BF16)


def _mlstm(qk, av, aif, ao, conv_w, gate_bias, norm_g, bsz, seq):
    tile = min(MLSTM_TILE, seq)
    nc = seq // tile
    gbias = jnp.zeros((1, LANES), F32).at[0, :A_GATE].set(gate_bias.reshape(-1))
    tok = lambda w: pl.BlockSpec((tile, w), lambda b, c: (b * nc + c, 0))
    full = lambda shape: pl.BlockSpec(shape, lambda b, c: (0,) * len(shape))
    return pl.pallas_call(
        functools.partial(_mlstm_kernel, tile=tile),
        grid=(bsz, nc),
        in_specs=[tok(A_QK), tok(A_V), tok(LANES), tok(A_V),
                  full((MLSTM_CONV, A_QK)), full((1, LANES)), full((1, A_V))],
        out_specs=tok(A_V),
        out_shape=jax.ShapeDtypeStruct((bsz * seq, A_V), BF16),
        scratch_shapes=[pltpu.VMEM((tile + SUBLANES, A_QK), F32),
                        pltpu.VMEM((MLSTM_HEADS, 2 * MLSTM_V_DIM, MLSTM_QK_DIM), F32),
                        pltpu.VMEM((SUBLANES, LANES), F32)],
        compiler_params=_cparams(("parallel", "arbitrary")),
        name="mlstm",
    )(qk, av, aif, ao, conv_w, gbias, norm_g.reshape(1, A_V))


def _t5_bucket_np(dist):
    dist = np.maximum(dist, 0)
    exact = REL_BUCKETS // 2
    log_ratio = (np.log(np.maximum(dist, 1).astype(np.float32) / np.float32(exact))
                 / np.float32(math.log(REL_MAX_DIST / exact))).astype(np.float32)
    large = np.minimum(exact + (log_ratio * np.float32(REL_BUCKETS - exact)).astype(np.int32), REL_BUCKETS - 1)
    return np.where(dist < exact, dist, large)


def _near_bias_tables(rel_bias, tq, kn, kf):
    n_off = kf // tq + 1
    assert np.all(_t5_bucket_np(np.arange(tq + 1, 4 * REL_MAX_DIST)) == REL_BUCKETS - 1)
    d_max = (n_off - 1) * tq + tq - 1
    dist = np.arange(d_max, -kn, -1)
    by_dist = (rel_bias[_t5_bucket_np(dist)] - rel_bias[REL_BUCKETS - 1]) * LOG2E
    by_dist = jnp.where((dist >= 0)[:, None], by_dist, 0.0).T.astype(F32)
    rows = [[by_dist[:, d_max - (o * tq + t):d_max - (o * tq + t) + kn] for t in range(tq)] for o in range(n_off)]
    tab = jnp.stack([jnp.stack(r, axis=1) for r in rows], axis=0)
    return tab.reshape(n_off, DSA_HEADS * tq, kn)


def _dsa_kernel(bq_ref, iq_ref, ikwq_ref, ikw_ref, ckv_ref, tab_ref, wuk_ref, wuv_ref, out_ref,
                key_ref, smax_ref, smin_ref, npos_ref, nnon_ref, acc_ref, m_ref, ql_ref, wb_ref,
                *, tq, kn, kf, topk):
    H = DSA_HEADS
    j = pl.program_id(1)
    t0 = j * tq
    n_score = (t0 + tq - 1) // kf + 1
    n_far = jnp.maximum(t0 - tq, 0) // kf
    near_lo = n_far * (kf // kn)
    near_hi = (t0 + tq - 1) // kn + 1

    ql_all = (_dot(bq_ref[...], wuk_ref[...]) * (DSA_HEAD_DIM ** -0.5 * LOG2E)).astype(BF16)
    for h in range(H):
        ql_ref[h * tq:(h + 1) * tq, :] = ql_all[:, h * DSA_LATENT:(h + 1) * DSA_LATENT]
    w_idx = ikwq_ref[:, IKW_W_OFF:IKW_W_OFF + IDX_HEADS] * ((IDX_HEADS * IDX_DIM) ** -0.5)
    for h in range(IDX_HEADS):
        wb_ref[h] = jnp.broadcast_to(w_idx[:, h:h + 1], (tq, kf))
    iq = iq_ref[...]
    qis = [iq[:, h * IDX_DIM:(h + 1) * IDX_DIM] for h in range(IDX_HEADS)]
    tpos = t0 + lax.broadcasted_iota(jnp.int32, (tq, kf), 0)
    lpos = lax.broadcasted_iota(jnp.int32, (tq, kf), 1)

    ncol_f = kf // LANES
    tile_f = lambda a: jnp.concatenate([a] * ncol_f, axis=1)

    smax_ref[...] = jnp.full(smax_ref.shape, -jnp.inf, F32)
    smin_ref[...] = jnp.full(smin_ref.shape, jnp.inf, F32)
    npos_ref[...] = jnp.zeros_like(npos_ref)
    nnon_ref[...] = jnp.zeros_like(nnon_ref)

    def score_chunk(c, masked):
        s0 = pl.multiple_of(c * kf, kf)
        kidx = ikw_ref[pl.ds(s0, kf), :][:, 0:IDX_DIM].astype(BF16)
        sc = jnp.zeros((tq, kf), F32)
        for h in range(IDX_HEADS):
            sc = sc + wb_ref[h] * jnp.maximum(_dot_nt(qis[h], kidx), 0.0)
        sc = jnp.where(jnp.abs(sc) < F32_MIN_NORMAL, 0.0, sc)
        bits = lax.bitcast_convert_type(sc, jnp.int32)
        key = bits ^ ((bits >> 31) & 0x7FFFFFFF)
        sc_lo = sc
        if masked:
            causal = s0 + lpos <= tpos
            key = jnp.where(causal, key, INT_MIN)
            sc = jnp.where(causal, sc, -jnp.inf)
            sc_lo = jnp.where(causal, sc_lo, jnp.inf)
        key_ref[:, pl.ds(s0, kf)] = key
        smax = smax_ref[...]
        smin = smin_ref[...]
        npos = npos_ref[...]
        nnon = nnon_ref[...]
        for g in range(ncol_f):
            cols = slice(g * LANES, (g + 1) * LANES)
            smax = jnp.maximum(smax, sc[:, cols])
            smin = jnp.minimum(smin, sc_lo[:, cols])
            npos = npos + jnp.where(sc[:, cols] > 0.0, 1.0, 0.0)
            nnon = nnon + jnp.where(sc[:, cols] >= 0.0, 1.0, 0.0)
        smax_ref[...] = smax
        smin_ref[...] = smin
        npos_ref[...] = npos
        nnon_ref[...] = nnon

    def score_pair(i, carry):
        score_chunk(2 * i, False)
        score_chunk(2 * i + 1, False)
        return carry

    lax.fori_loop(0, (n_score - 1) // 2, score_pair, 0)

    @pl.when((n_score - 1) % 2 == 1)
    def _():
        score_chunk(n_score - 2, False)

    score_chunk(n_score - 1, True)

    def count_ge(cand):
        cand_w = tile_f(cand)

        def body(c, acc):
            s0 = pl.multiple_of(c * kf, kf)
            ind = jnp.where(key_ref[:, pl.ds(s0, kf)] >= cand_w, 1.0, 0.0)
            for g in range(ncol_f):
                acc = acc + ind[:, g * LANES:(g + 1) * LANES]
            return acc
        acc = lax.fori_loop(0, n_score, body, jnp.zeros((tq, LANES), F32))
        return jnp.sum(acc, axis=1, keepdims=True)

    n_valid = (t0 + 1 + lax.broadcasted_iota(jnp.int32, (tq, LANES), 0)).astype(F32)
    enough = n_valid >= topk
    flip = lambda a: a ^ ((a >> 31) & 0x7FFFFFFF)
    to_key = lambda v: flip(lax.bitcast_convert_type(v, jnp.int32))
    to_val = lambda k: lax.bitcast_convert_type(flip(k), F32)
    row_all = lambda a: jnp.broadcast_to(a, (tq, LANES))
    lo0 = to_key(row_all(jnp.min(smin_ref[...], axis=1, keepdims=True)))
    hi0 = to_key(row_all(jnp.max(smax_ref[...], axis=1, keepdims=True))) + 1
    n_pos = row_all(jnp.sum(npos_ref[...], axis=1, keepdims=True))
    n_non = row_all(jnp.sum(nnon_ref[...], axis=1, keepdims=True))
    pos_thr = n_pos >= topk
    zero_thr = jnp.logical_and(n_non >= topk, jnp.logical_not(pos_thr))
    cnt0 = jnp.where(pos_thr, n_pos, jnp.where(zero_thr, n_non, n_valid))
    lo0 = jnp.where(pos_thr, 1, jnp.where(zero_thr, 0, lo0))
    hi0 = jnp.where(pos_thr, jnp.where(n_pos == topk, 2, hi0), jnp.where(zero_thr, 1, 0))

    def probe(mid, lo, hi, cnt_lo):
        cnt = count_ge(mid)
        ge = cnt >= topk
        return (jnp.where(ge, mid, lo), jnp.where(cnt == topk, mid + 1, jnp.where(ge, hi, mid)),
                jnp.where(ge, cnt, cnt_lo))

    def by_value(i, state):
        lo, hi, cnt_lo = state
        mid = to_key(0.5 * to_val(lo) + 0.5 * to_val(hi))
        return probe(jnp.where(jnp.logical_and(mid > lo, mid < hi), mid, lo), lo, hi, cnt_lo)

    state = lax.fori_loop(0, VALUE_PASSES, by_value, (lo0, hi0, cnt0))

    def any_open(lo, hi):
        return jnp.max(jnp.where(jnp.logical_and(enough, hi > lo + 1), 1.0, 0.0))

    def by_key(state):
        lo, hi, cnt_lo, _ = state
        lo, hi, cnt_lo = probe((lo >> 1) + (hi >> 1) + (lo & hi & 1), lo, hi, cnt_lo)
        return lo, hi, cnt_lo, any_open(lo, hi)

    lo, _, cnt_lo, _ = lax.while_loop(lambda s: s[3] > 0.0, by_key, state + (any_open(state[0], state[1]),))
    thr = jnp.where(enough, lo, INT_MIN + 1)

    def count_key(pred):
        def body(c, acc):
            s0 = pl.multiple_of(c * kf, kf)
            ind = jnp.where(pred(key_ref[:, pl.ds(s0, kf)]), 1.0, 0.0)
            for g in range(ncol_f):
                acc = acc + ind[:, g * LANES:(g + 1) * LANES]
            return acc
        acc = lax.fori_loop(0, n_score, body, jnp.zeros((tq, LANES), F32))
        return jnp.sum(acc, axis=1, keepdims=True)

    thr_f = tile_f(thr)
    surplus = jnp.where(enough, cnt_lo - topk, 0.0)

    @pl.when(jnp.max(surplus) > 0.0)
    def _():
        allowed = topk - count_key(lambda k: k > thr_f)
        r = lax.broadcasted_iota(jnp.int32, (kf, kf), 0)
        cidx = lax.broadcasted_iota(jnp.int32, (kf, kf), 1)
        upper = jnp.where(r <= cidx, 1.0, 0.0).astype(BF16)

        def body(c, seen):
            s0 = pl.multiple_of(c * kf, kf)
            key = key_ref[:, pl.ds(s0, kf)]
            tie = key == thr_f
            tie_f = jnp.where(tie, 1.0, 0.0)
            rank = _dot(tie_f.astype(BF16), upper) + seen
            key_ref[:, pl.ds(s0, kf)] = jnp.where(jnp.logical_and(tie, rank > allowed), thr_f - 1, key)
            return seen + jnp.sum(tie_f, axis=1, keepdims=True)
        lax.fori_loop(0, n_score, body, jnp.zeros((tq, 1), F32))

    m_ref[...] = jnp.full(m_ref.shape, M_FLOOR, F32)
    acc_ref[...] = jnp.zeros_like(acc_ref)

    def attend(s0, width, table_idx):
        ncol = width // LANES
        kv = ckv_ref[pl.ds(s0, width), :]
        kv_ext = jnp.concatenate([kv, jnp.ones((width, DSA_LATENT), BF16)], axis=1)
        thr_w = jnp.concatenate([thr] * ncol, axis=1)
        madd = jnp.where(key_ref[:, pl.ds(s0, width)] >= thr_w, 0.0, NEG_BIG)
        for h in range(H):
            rows = slice(h * tq, (h + 1) * tq)
            x = _dot_nt(ql_ref[rows, :], kv) + madd
            if table_idx is not None:
                x = x + tab_ref[table_idx, rows, :]
            m_old = m_ref[rows, :]
            m_new = jnp.maximum(m_old, jnp.max(x, axis=1, keepdims=True))
            alpha = jnp.exp2(m_old - m_new)
            p = jnp.exp2(x - jnp.concatenate([m_new] * ncol, axis=1))
            acc_ref[rows, :] = jnp.concatenate([alpha, alpha], axis=1) * acc_ref[rows, :] + _dot(p.astype(BF16), kv_ext)
            m_ref[rows, :] = m_new

    def far_pair(i, carry):
        attend(pl.multiple_of(2 * i * kf, kf), kf, None)
        attend(pl.multiple_of((2 * i + 1) * kf, kf), kf, None)
        return carry

    def near_chunk(c, carry):
        s0 = pl.multiple_of(c * kn, kn)
        attend(s0, kn, (t0 - s0) // tq)
        return carry

    lax.fori_loop(0, n_far // 2, far_pair, 0)

    @pl.when(n_far % 2 == 1)
    def _():
        attend(pl.multiple_of((n_far - 1) * kf, kf), kf, None)

    lax.fori_loop(near_lo, near_hi, near_chunk, 0)

    outs = []
    for h in range(H):
        a = acc_ref[h * tq:(h + 1) * tq, :]
        outs.append((a[:, :DSA_LATENT] / a[:, DSA_LATENT:]).astype(BF16))
    out_ref[...] = _dot(jnp.concatenate(outs, axis=1), wuv_ref[...]).astype(BF16)


def _dsa_tiles(seq):
    tq = min(DSA_TQ, seq)
    return tq, min(DSA_KN, seq), min(DSA_KF, seq)


def _dsa(bq, iq, ikw, ckv, tab, wuk_pad, wuv_pad, bsz, seq):
    tq, kn, kf = _dsa_tiles(seq)
    nq = seq // tq
    topk = min(IDX_TOPK_MAX, seq // 4)
    n_off = kf // tq + 1
    qblk = lambda w: pl.BlockSpec((tq, w), lambda b, j: (b * nq + j, 0))
    seqblk = lambda w: pl.BlockSpec((seq, w), lambda b, j: (b, 0))
    full = lambda shape: pl.BlockSpec(shape, lambda b, j: (0,) * len(shape))
    return pl.pallas_call(
        functools.partial(_dsa_kernel, tq=tq, kn=kn, kf=kf, topk=topk),
        grid=(bsz, nq),
        in_specs=[qblk(B_Q), qblk(I_Q), qblk(LANES), seqblk(LANES), seqblk(DSA_LATENT),
                  full((n_off, DSA_HEADS * tq, kn)), full((B_Q, DSA_HEADS * DSA_LATENT)),
                  full((DSA_HEADS * DSA_LATENT, B_Q))],
        out_specs=qblk(B_Q),
        out_shape=jax.ShapeDtypeStruct((bsz * seq, B_Q), BF16),
        scratch_shapes=[pltpu.VMEM((tq, seq), jnp.int32),
                        pltpu.VMEM((tq, LANES), F32),
                        pltpu.VMEM((tq, LANES), F32),
                        pltpu.VMEM((tq, LANES), F32),
                        pltpu.VMEM((tq, LANES), F32),
                        pltpu.VMEM((DSA_HEADS * tq, 2 * DSA_LATENT), F32),
                        pltpu.VMEM((DSA_HEADS * tq, LANES), F32),
                        pltpu.VMEM((DSA_HEADS * tq, DSA_LATENT), BF16),
                        pltpu.VMEM((IDX_HEADS, tq, kf), F32)],
        compiler_params=_cparams(("parallel", "arbitrary")),
        name="dsa",
    )(bq, iq, ikw, ikw, ckv, tab, wuk_pad, wuv_pad)


def _kv_norm_kernel(bc_ref, g_ref, out_ref):
    x = bc_ref[...]
    out_ref[...] = (x * lax.rsqrt(jnp.mean(x * x, axis=-1, keepdims=True) + LN_EPS) * g_ref[...]).astype(BF16)


def _kv_norm(bc, g):
    n = bc.shape[0]
    tm = min(4 * TOK_TILE, n)
    return pl.pallas_call(
        _kv_norm_kernel,
        grid=(n // tm,),
        in_specs=[pl.BlockSpec((tm, DSA_LATENT), lambda i: (i, 0)), pl.BlockSpec((1, DSA_LATENT), lambda i: (0, 0))],
        out_specs=pl.BlockSpec((tm, DSA_LATENT), lambda i: (i, 0)),
        out_shape=jax.ShapeDtypeStruct((n, DSA_LATENT), BF16),
        compiler_params=_cparams(("parallel",)),
        name="kv_norm",
    )(bc, g.reshape(1, DSA_LATENT))


def _merge_kernel(x_ref, ya_ref, yb_ref, ga_ref, gb_ref, wa_ref, wb_ref, wo_ref, g_ref, b_ref, out_ref):
    merged = (jax.nn.sigmoid(ga_ref[...].astype(F32)) * _dot(ya_ref[...], wa_ref[...])
              + jax.nn.sigmoid(gb_ref[...].astype(F32)) * _dot(yb_ref[...], wb_ref[...]))
    mix = _dot(merged.astype(BF16), wo_ref[...])
    out_ref[...] = _layer_norm(DN_ALPHA * x_ref[...] + mix, g_ref[...], b_ref[...])


def _merge(x2, ya, yb, ga, gb, wa, wb, wo, g, b):
    n = x2.shape[0]
    tm = min(TOK_TILE, n)
    tok = lambda w: pl.BlockSpec((tm, w), lambda i: (i, 0))
    full = lambda shape: pl.BlockSpec(shape, lambda i: (0,) * len(shape))
    return pl.pallas_call(
        _merge_kernel,
        grid=(n // tm,),
        in_specs=[tok(D_MODEL), tok(A_V), tok(B_Q), tok(D_MODEL), tok(D_MODEL),
                  full((A_V, D_MODEL)), full((B_Q, D_MODEL)), full((D_MODEL, D_MODEL)),
                  full((1, D_MODEL)), full((1, D_MODEL))],
        out_specs=tok(D_MODEL),
        out_shape=jax.ShapeDtypeStruct((n, D_MODEL), F32),
        compiler_params=_cparams(("parallel",)),
        name="merge_ln",
    )(x2, ya, yb, ga, gb, wa, wb, wo, g.reshape(1, D_MODEL), b.reshape(1, D_MODEL))


def _dense_ffn_kernel(x_ref, wg_ref, wu_ref, wd_ref, g_ref, b_ref, out_ref):
    x = x_ref[...]
    xb = x.astype(BF16)
    f = jnp.zeros(x.shape, F32)
    for c in range(D_FF // FF_CHUNK):
        cols = slice(c * FF_CHUNK, (c + 1) * FF_CHUNK)
        hdn = _silu(_dot(xb, wg_ref[:, cols])) * _dot(xb, wu_ref[:, cols])
        f = f + _dot(hdn.astype(BF16), wd_ref[cols, :])
    out_ref[...] = _layer_norm(DN_ALPHA * x + f, g_ref[...], b_ref[...])


def _dense_ffn(x2, wg, wu, wd, g, b):
    n = x2.shape[0]
    tm = min(TOK_TILE, n)
    tok = pl.BlockSpec((tm, D_MODEL), lambda i: (i, 0))
    full = lambda shape: pl.BlockSpec(shape, lambda i: (0,) * len(shape))
    return pl.pallas_call(
        _dense_ffn_kernel,
        grid=(n // tm,),
        in_specs=[tok, full((D_MODEL, D_FF)), full((D_MODEL, D_FF)), full((D_FF, D_MODEL)),
                  full((1, D_MODEL)), full((1, D_MODEL))],
        out_specs=tok,
        out_shape=jax.ShapeDtypeStruct((n, D_MODEL), F32),
        compiler_params=_cparams(("parallel",)),
        name="dense_ffn_ln",
    )(x2, wg, wu, wd, g.reshape(1, D_MODEL), b.reshape(1, D_MODEL))


def _router_kernel(x_ref, rw_ref, e_ref, g_ref):
    logits = lax.dot_general(rw_ref[...], x_ref[...], NT_DIMS, preferred_element_type=F32,
                             precision=lax.Precision.HIGHEST)
    eidx = lax.broadcasted_iota(jnp.int32, logits.shape, 0)
    m1 = jnp.max(logits, axis=0, keepdims=True)
    i1 = jnp.min(jnp.where(logits == m1, eidx, N_EXPERTS), axis=0, keepdims=True)
    rest = jnp.where(eidx == i1, -jnp.inf, logits)
    m2 = jnp.max(rest, axis=0, keepdims=True)
    i2 = jnp.min(jnp.where(rest == m2, eidx, N_EXPERTS), axis=0, keepdims=True)
    e2 = jnp.exp(m2 - m1)
    den = 1.0 + e2
    e_ref[0:1, :] = i1
    e_ref[1:2, :] = i2
    g_ref[0:1, :] = 1.0 / den
    g_ref[1:2, :] = e2 / den


def _router(x2, rw_t):
    n = x2.shape[0]
    tm = min(TOK_TILE, n)
    return pl.pallas_call(
        _router_kernel,
        grid=(n // tm,),
        in_specs=[pl.BlockSpec((tm, D_MODEL), lambda i: (i, 0)), pl.BlockSpec((N_EXPERTS, D_MODEL), lambda i: (0, 0))],
        out_specs=[pl.BlockSpec((TOP_K, tm), lambda i: (0, i)), pl.BlockSpec((TOP_K, tm), lambda i: (0, i))],
        out_shape=[jax.ShapeDtypeStruct((TOP_K, n), jnp.int32), jax.ShapeDtypeStruct((TOP_K, n), F32)],
        compiler_params=_cparams(("parallel",)),
        name="router",
    )(x2, rw_t)


def _row_copy(src, src_row, dst, dst_row, sem):
    return pltpu.make_async_copy(src.at[pl.ds(src_row, 1)], dst.at[pl.ds(dst_row, 1)], sem)


def _dispatch_kernel(d0_ref, d1_ref, x_ref, xs_init_hbm, xs_hbm, sem, *, tile):
    del xs_init_hbm

    def start(r, c):
        _row_copy(x_ref, r, xs_hbm, d0_ref[0, r], sem).start()
        _row_copy(x_ref, r, xs_hbm, d1_ref[0, r], sem).start()
        return c
    lax.fori_loop(0, tile, start, 0, unroll=DMA_UNROLL)

    def wait(r, c):
        _row_copy(x_ref, r, xs_hbm, 0, sem).wait()
        _row_copy(x_ref, r, xs_hbm, 0, sem).wait()
        return c
    lax.fori_loop(0, tile, wait, 0, unroll=DMA_UNROLL)


def _dispatch(x2, pos, total):
    n = x2.shape[0]
    tile = min(COMB_TILE, n)
    nb = n // tile
    smem = pl.BlockSpec((None, 1, tile), lambda i: (i, 0, 0), memory_space=pltpu.SMEM)
    return pl.pallas_call(
        functools.partial(_dispatch_kernel, tile=tile),
        grid=(nb,),
        in_specs=[smem, smem, pl.BlockSpec((tile, D_MODEL), lambda i: (i, 0)), pl.BlockSpec(memory_space=pl.ANY)],
        out_specs=pl.BlockSpec(memory_space=pl.ANY),
        out_shape=jax.ShapeDtypeStruct((total, D_MODEL), F32),
        scratch_shapes=[pltpu.SemaphoreType.DMA(())],
        input_output_aliases={3: 0},
        compiler_params=_cparams(("arbitrary",)),
        name="moe_dispatch",
    )(pos[:, 0].reshape(nb, 1, tile), pos[:, 1].reshape(nb, 1, tile), x2, jnp.zeros((total, D_MODEL), F32))


def _expert_kernel(blk_e_ref, nused_ref, xs_ref, wg_ref, wu_ref, wd_ref, y_ref, xb, acc):
    i = pl.program_id(0)
    f = pl.program_id(1)
    used = i < nused_ref[0]
    last = f == pl.num_programs(1) - 1

    @pl.when(jnp.logical_and(used, f == 0))
    def _():
        xb[...] = xs_ref[...].astype(BF16)
        acc[...] = jnp.zeros_like(acc)

    @pl.when(used)
    def _():
        x = xb[...]
        hdn = _silu(_dot(x, wg_ref[...])) * _dot(x, wu_ref[...])
        acc[...] += _dot(hdn.astype(BF16), wd_ref[...])

    @pl.when(jnp.logical_and(used, last))
    def _():
        y_ref[...] = acc[...]

    @pl.when(jnp.logical_and(jnp.logical_not(used), last))
    def _():
        y_ref[...] = jnp.zeros_like(y_ref)


def _experts(xs, blk_e, nused, wg, wu, wd):
    total = xs.shape[0]
    tile = MOE_TILE
    nblk = total // tile
    nf = D_FF_EXPERT // MOE_FT
    grid_spec = pltpu.PrefetchScalarGridSpec(
        num_scalar_prefetch=2,
        grid=(nblk, nf),
        in_specs=[pl.BlockSpec((tile, D_MODEL), lambda i, f, be, nu: (i, 0)),
                  pl.BlockSpec((None, D_MODEL, MOE_FT), lambda i, f, be, nu: (be[i], 0, f)),
                  pl.BlockSpec((None, D_MODEL, MOE_FT), lambda i, f, be, nu: (be[i], 0, f)),
                  pl.BlockSpec((None, MOE_FT, D_MODEL), lambda i, f, be, nu: (be[i], f, 0))],
        out_specs=pl.BlockSpec((tile, D_MODEL), lambda i, f, be, nu: (i, 0)),
        scratch_shapes=[pltpu.VMEM((tile, D_MODEL), BF16), pltpu.VMEM((tile, D_MODEL), F32)],
    )
    return pl.pallas_call(
        _expert_kernel,
        grid_spec=grid_spec,
        out_shape=jax.ShapeDtypeStruct((total, D_MODEL), F32),
        compiler_params=_cparams(("arbitrary", "arbitrary")),
        name="experts",
    )(blk_e, nused, xs, wg, wu, wd)


def _combine_kernel(p0_ref, p1_ref, y_hbm, x_ref, gate_ref, g_ref, b_ref, out_ref, buf0, buf1, sem, *, tile):
    def start(r, c):
        _row_copy(y_hbm, p0_ref[0, r], buf0, r, sem).start()
        _row_copy(y_hbm, p1_ref[0, r], buf1, r, sem).start()
        return c
    lax.fori_loop(0, tile, start, 0, unroll=DMA_UNROLL)

    def wait(r, c):
        _row_copy(y_hbm, 0, buf0, r, sem).wait()
        _row_copy(y_hbm, 0, buf1, r, sem).wait()
        return c
    lax.fori_loop(0, tile, wait, 0, unroll=DMA_UNROLL)
    f = buf0[...] * gate_ref[:, 0:1] + buf1[...] * gate_ref[:, 1:2]
    out_ref[...] = _layer_norm(DN_ALPHA * x_ref[...] + f, g_ref[...], b_ref[...])


def _combine(y, pos, gates, x2, g, b):
    n = x2.shape[0]
    tile = min(COMB_TILE, n)
    nb = n // tile
    smem = pl.BlockSpec((None, 1, tile), lambda i: (i, 0, 0), memory_space=pltpu.SMEM)
    full = pl.BlockSpec((1, D_MODEL), lambda i: (0, 0))
    return pl.pallas_call(
        functools.partial(_combine_kernel, tile=tile),
        grid=(nb,),
        in_specs=[smem, smem, pl.BlockSpec(memory_space=pl.ANY),
                  pl.BlockSpec((tile, D_MODEL), lambda i: (i, 0)),
                  pl.BlockSpec((tile, TOP_K), lambda i: (i, 0)), full, full],
        out_specs=pl.BlockSpec((tile, D_MODEL), lambda i: (i, 0)),
        out_shape=jax.ShapeDtypeStruct((n, D_MODEL), F32),
        scratch_shapes=[pltpu.VMEM((tile, D_MODEL), F32), pltpu.VMEM((tile, D_MODEL), F32),
                        pltpu.SemaphoreType.DMA(())],
        compiler_params=_cparams(("arbitrary",)),
        name="moe_combine_ln",
    )(pos[:, 0].reshape(nb, 1, tile), pos[:, 1].reshape(nb, 1, tile), y, x2, gates,
      g.reshape(1, D_MODEL), b.reshape(1, D_MODEL))


def _moe(x2, router_w, wg, wu, wd, g, b):
    n = x2.shape[0]
    top_e, gates = _router(x2, router_w.T)
    flat_e = top_e.T.reshape(-1)
    onehot = (flat_e[:, None] == jnp.arange(N_EXPERTS, dtype=jnp.int32)[None, :]).astype(jnp.int32)
    rank = jnp.sum((jnp.cumsum(onehot, axis=0) - onehot) * onehot, axis=1)
    counts = jnp.sum(onehot, axis=0)
    padded = ((counts + MOE_TILE - 1) // MOE_TILE) * MOE_TILE
    pad_end = jnp.cumsum(padded)
    pad_start = pad_end - padded
    pos = (pad_start[flat_e] + rank).astype(jnp.int32).reshape(n, TOP_K)
    total = n * TOP_K + N_EXPERTS * MOE_TILE
    nblk = total // MOE_TILE
    blk_e = jnp.minimum(jnp.searchsorted(pad_end, jnp.arange(nblk, dtype=jnp.int32) * MOE_TILE, side='right'),
                        N_EXPERTS - 1).astype(jnp.int32)
    nused = (pad_end[-1:] // MOE_TILE).astype(jnp.int32)
    xs = _dispatch(x2, pos, total)
    y = _experts(xs, blk_e, nused, wg, wu, wd)
    return _combine(y, pos, gates.T, x2, g, b)


def _pad_head_proj(w_uk, w_uv):
    H, DH, DC = DSA_HEADS, DSA_HEAD_DIM, DSA_LATENT
    eye = jnp.eye(H, dtype=w_uk.dtype)
    wuk_bd = jnp.einsum('hg,hdc->hdgc', eye, w_uk).reshape(H * DH, H * DC)
    wuv_bd = jnp.einsum('hg,hcd->hcgd', eye, w_uv).reshape(H * DC, H * DH)
    return wuk_bd.astype(BF16), wuv_bd.astype(BF16)


def kernel(x, w_in, mlstm_conv_w, mlstm_gate_bias, mlstm_norm_g, dsa_kv_norm_g, dsa_w_uk, dsa_w_uv, rel_bias,
           w_branch_a, w_branch_b, w_out, ln_g, ln_b, dense_w_gate, dense_w_up, dense_w_down, router_w,
           expert_w_gate, expert_w_up, expert_w_down):
    bsz, seq, d = x.shape
    n = bsz * seq
    x2 = x.reshape(n, d)
    tab = _near_bias_tables(rel_bias, *_dsa_tiles(seq))
    for l in range(DEPTH):
        p = dict(zip([nm for nm, _, _ in IN_PROJ_OUTS], _in_proj(x2, _regroup_in_proj(w_in[l]))))
        ya = _mlstm(p["qk"], p["av"], p["aif"], p["ao"], mlstm_conv_w[l], mlstm_gate_bias[l], mlstm_norm_g[l],
                    bsz, seq)
        ckv = _kv_norm(p["bc"], dsa_kv_norm_g[l])
        wuk_pad, wuv_pad = _pad_head_proj(dsa_w_uk[l], dsa_w_uv[l])
        yb = _dsa(p["bq"], p["iq"], p["ikw"], ckv, tab, wuk_pad, wuv_pad, bsz, seq)
        x2 = _merge(x2, ya, yb, p["ga"], p["gb"], w_branch_a[l].astype(BF16), w_branch_b[l].astype(BF16),
                    w_out[l].astype(BF16), ln_g[l, 0], ln_b[l, 0])
        j = l // 2
        if l % 2 == 0:
            x2 = _dense_ffn(x2, dense_w_gate[j].astype(BF16), dense_w_up[j].astype(BF16),
                            dense_w_down[j].astype(BF16), ln_g[l, 1], ln_b[l, 1])
        else:
            x2 = _moe(x2, router_w[j], expert_w_gate[j].astype(BF16), expert_w_up[j].astype(BF16),
                      expert_w_down[j].astype(BF16), ln_g[l, 1], ln_b[l, 1])
    return x2.reshape(bsz, seq, d)
```

```python
import functools
import math

import numpy as np
import jax
import jax.numpy as jnp
from jax import lax
from jax.experimental import pallas as pl
from jax.experimental.pallas import tpu as pltpu

D_MODEL = 1024
DEPTH = 4
MLSTM_HEADS = 4
MLSTM_QK_DIM = 64
MLSTM_V_DIM = 128
MLSTM_CONV = 4
DSA_HEADS = 8
DSA_HEAD_DIM = 64
DSA_LATENT = 128
IDX_HEADS = 8
IDX_DIM = 32
IDX_TOPK_MAX = 256
REL_BUCKETS = 32
REL_MAX_DIST = 128
D_FF = 2816
N_EXPERTS = 8
TOP_K = 2
D_FF_EXPERT = 3584
DN_ALPHA = (2 * DEPTH) ** 0.25
LN_EPS = 1e-5

A_QK = 2 * MLSTM_HEADS * MLSTM_QK_DIM
A_V = MLSTM_HEADS * MLSTM_V_DIM
A_GATE = 2 * MLSTM_HEADS
B_Q = DSA_HEADS * DSA_HEAD_DIM
I_Q = IDX_HEADS * IDX_DIM
PROJ_SIZES = (A_QK, A_V, A_V, A_GATE, B_Q, DSA_LATENT, I_Q, IDX_DIM, IDX_HEADS, D_MODEL, D_MODEL)

LANES = 128
SUBLANES = 8
VMEM_LIMIT = 56 * 1024 * 1024
TOK_TILE = 512
MLSTM_TILE = 256
DSA_TQ = 128
DSA_KN = 256
DSA_KF = 512
MOE_TILE = 512
MOE_FT = 896
COMB_TILE = 256
DMA_UNROLL = 8
FF_CHUNK = 256
INT_MIN = -(2 ** 31)
NEG_BIG = -1e30
M_FLOOR = -1e29
LOG2E = math.log2(math.e)
F32_MIN_NORMAL = 2.0 ** -126
VALUE_PASSES = 12

BF16 = jnp.bfloat16
F32 = jnp.float32
NT_DIMS = (((1,), (1,)), ((), ()))


def _cparams(sem):
    return pltpu.CompilerParams(dimension_semantics=sem, vmem_limit_bytes=VMEM_LIMIT)


def _dot(a, b):
    return jnp.dot(a, b, preferred_element_type=F32)


def _dot_nt(a, b):
    return lax.dot_general(a, b, NT_DIMS, preferred_element_type=F32)


def _layer_norm(r, g, b):
    mu = jnp.mean(r, axis=-1, keepdims=True)
    d = r - mu
    var = jnp.mean(d * d, axis=-1, keepdims=True)
    return d * lax.rsqrt(var + LN_EPS) * g + b


def _silu(x):
    return x * jax.nn.sigmoid(x)


IN_PROJ_OUTS = (("qk", A_QK, F32), ("av", A_V, BF16), ("ao", A_V, BF16), ("bq", B_Q, BF16),
                ("iq", I_Q, BF16), ("ga", D_MODEL, BF16), ("gb", D_MODEL, BF16),
                ("aif", LANES, F32), ("bc", LANES, F32), ("ikw", LANES, F32))
IN_PROJ_COLS = sum(w for _, w, _ in IN_PROJ_OUTS)
IKW_W_OFF = IDX_DIM


def _regroup_in_proj(w):
    offs = np.concatenate([[0], np.cumsum(PROJ_SIZES)])
    a_qk, a_v, a_o, a_if, b_q, b_c, i_q, i_k, i_w, g_a, g_b = (w[:, offs[i]:offs[i + 1]] for i in range(11))
    pad = lambda a: jnp.pad(a, ((0, 0), (0, LANES - a.shape[1])))
    groups = [a_qk, a_v, a_o, b_q, i_q, g_a, g_b, pad(a_if), b_c, pad(jnp.concatenate([i_k, i_w], axis=1))]
    return jnp.concatenate(groups, axis=1).astype(BF16)


def _in_proj_kernel(x_ref, w_ref, *out_refs):
    xb = x_ref[...].astype(BF16)
    off = 0
    for (_, width, dtype), o_ref in zip(IN_PROJ_OUTS, out_refs):
        o_ref[...] = _dot(xb, w_ref[:, off:off + width]).astype(dtype)
        off += width


def _in_proj(x2, w):
    n = x2.shape[0]
    tm = min(TOK_TILE, n)
    return pl.pallas_call(
        _in_proj_kernel,
        grid=(n // tm,),
        in_specs=[pl.BlockSpec((tm, D_MODEL), lambda i: (i, 0)),
                  pl.BlockSpec((D_MODEL, IN_PROJ_COLS), lambda i: (0, 0))],
        out_specs=[pl.BlockSpec((tm, wd), lambda i: (i, 0)) for _, wd, _ in IN_PROJ_OUTS],
        out_shape=[jax.ShapeDtypeStruct((n, wd), dt) for _, wd, dt in IN_PROJ_OUTS],
        compiler_params=_cparams(("parallel",)),
        name="in_proj",
    )(x2, w)


def _log_sigmoid(x):
    return jnp.minimum(x, 0.0) - jnp.log1p(jnp.exp(-jnp.abs(x)))


def _mlstm_kernel(qk_ref, av_ref, aif_ref, ao_ref, convw_ref, gbias_ref, ng_ref, out_ref,
                  ext_ref, ct_ref, m_ref, *, tile):
    L = tile
    H, DK, DV = MLSTM_HEADS, MLSTM_QK_DIM, MLSTM_V_DIM

    @pl.when(pl.program_id(1) == 0)
    def _():
        ext_ref[0:SUBLANES, :] = jnp.zeros((SUBLANES, A_QK), F32)
        ct_ref[...] = jnp.zeros_like(ct_ref)
        m_ref[...] = jnp.zeros_like(m_ref)

    u = qk_ref[...]
    ext_ref[SUBLANES:SUBLANES + L, :] = u
    conv = jnp.zeros((L, A_QK), F32)
    for j in range(MLSTM_CONV):
        conv = conv + ext_ref[pl.ds(SUBLANES - (MLSTM_CONV - 1) + j, L), :] * convw_ref[j:j + 1, :]
    ext_ref[0:SUBLANES, :] = u[L - SUBLANES:L, :]
    qk = _silu(conv)

    g_col = aif_ref[...] + gbias_ref[...]
    g_row = g_col.T
    row = lax.broadcasted_iota(jnp.int32, (L, L), 0)
    col = lax.broadcasted_iota(jnp.int32, (L, L), 1)
    causal = row >= col
    tril = jnp.where(causal, 1.0, 0.0).astype(F32)
    triu = jnp.where(row <= col, 1.0, 0.0).astype(F32)
    b_col = jnp.dot(tril, _log_sigmoid(g_col), preferred_element_type=F32, precision=lax.Precision.HIGHEST)
    b_row = jnp.dot(_log_sigmoid(g_row), triu, preferred_element_type=F32, precision=lax.Precision.HIGHEST)

    ones_v = jnp.ones((L, DV), BF16)
    ones_t = jnp.ones((DV, L), F32)
    for h in range(H):
        bcol = b_col[:, H + h:H + h + 1]
        icol = g_col[:, h:h + 1]
        brow = b_row[H + h:H + h + 1, :]
        irow = g_row[h:h + 1, :]
        mprev = m_ref[h:h + 1, 0:1]
        dlog = jnp.where(causal, bcol - brow + irow, -jnp.inf)
        m_t = jnp.maximum(bcol + mprev, jnp.max(dlog, axis=1, keepdims=True))
        q = qk[:, h * DK:(h + 1) * DK].astype(BF16)
        k = qk[:, H * DK + h * DK:H * DK + (h + 1) * DK] * (DK ** -0.5)
        s_w = _dot_nt(q, k.astype(BF16)) * jnp.exp(dlog - m_t)
        inter = jnp.exp(bcol + mprev - m_t)
        v_h = av_ref[:, h * DV:(h + 1) * DV]
        v_ext = jnp.concatenate([v_h, ones_v], axis=1)
        ct_st = ct_ref[h]
        ne = _dot(s_w.astype(BF16), v_ext) + inter * _dot_nt(q, ct_st.astype(BF16))
        h_out = ne[:, :DV] / jnp.maximum(jnp.abs(ne[:, DV:]), jnp.exp(-m_t))
        blast = bcol[L - 1:L, :]
        glog = blast - bcol + icol
        m_new = jnp.maximum(blast + mprev, jnp.max(glog, axis=0, keepdims=True))
        decay = jnp.exp(blast + mprev - m_new)
        w_k = k * jnp.exp(glog - m_new)
        v_ext_t = jnp.concatenate([v_h.astype(F32).T, ones_t], axis=0).astype(BF16)
        ct_ref[h] = decay * ct_st + _dot(v_ext_t, w_k.astype(BF16))
        m_ref[h:h + 1, :] = jnp.broadcast_to(m_new, (1, LANES))
        mu = jnp.mean(h_out, axis=-1, keepdims=True)
        d = h_out - mu
        hn = d * lax.rsqrt(jnp.mean(d * d, axis=-1, keepdims=True) + LN_EPS)
        y = hn * ng_ref[:, h * DV:(h + 1) * DV] * jax.nn.sigmoid(ao_ref[:, h * DV:(h + 1) * DV].astype(F32))
        out_ref[:, h * DV:(h + 1) * DV] = y.astype(BF16)


def _mlstm(qk, av, aif, ao, conv_w, gate_bias, norm_g, bsz, seq):
    tile = min(MLSTM_TILE, seq)
    nc = seq // tile
    gbias = jnp.zeros((1, LANES), F32).at[0, :A_GATE].set(gate_bias.reshape(-1))
    tok = lambda w: pl.BlockSpec((tile, w), lambda b, c: (b * nc + c, 0))
    full = lambda shape: pl.BlockSpec(shape, lambda b, c: (0,) * len(shape))
    return pl.pallas_call(
        functools.partial(_mlstm_kernel, tile=tile),
        grid=(bsz, nc),
        in_specs=[tok(A_QK), tok(A_V), tok(LANES), tok(A_V),
                  full((MLSTM_CONV, A_QK)), full((1, LANES)), full((1, A_V))],
        out_specs=tok(A_V),
        out_shape=jax.ShapeDtypeStruct((bsz * seq, A_V), BF16),
        scratch_shapes=[pltpu.VMEM((tile + SUBLANES, A_QK), F32),
                        pltpu.VMEM((MLSTM_HEADS, 2 * MLSTM_V_DIM, MLSTM_QK_DIM), F32),
                        pltpu.VMEM((SUBLANES, LANES), F32)],
        compiler_params=_cparams(("parallel", "arbitrary")),
        name="mlstm",
    )(qk, av, aif, ao, conv_w, gbias, norm_g.reshape(1, A_V))


def _t5_bucket_np(dist):
    dist = np.maximum(dist, 0)
    exact = REL_BUCKETS // 2
    log_ratio = (np.log(np.maximum(dist, 1).astype(np.float32) / np.float32(exact))
                 / np.float32(math.log(REL_MAX_DIST / exact))).astype(np.float32)
    large = np.minimum(exact + (log_ratio * np.float32(REL_BUCKETS - exact)).astype(np.int32), REL_BUCKETS - 1)
    return np.where(dist < exact, dist, large)


def _near_bias_tables(rel_bias, tq, kn, kf):
    n_off = kf // tq + 1
    assert np.all(_t5_bucket_np(np.arange(tq + 1, 4 * REL_MAX_DIST)) == REL_BUCKETS - 1)
    d_max = (n_off - 1) * tq + tq - 1
    dist = np.arange(d_max, -kn, -1)
    by_dist = (rel_bias[_t5_bucket_np(dist)] - rel_bias[REL_BUCKETS - 1]) * LOG2E
    by_dist = jnp.where((dist >= 0)[:, None], by_dist, 0.0).T.astype(F32)
    rows = [[by_dist[:, d_max - (o * tq + t):d_max - (o * tq + t) + kn] for t in range(tq)] for o in range(n_off)]
    tab = jnp.stack([jnp.stack(r, axis=1) for r in rows], axis=0)
    return tab.reshape(n_off, DSA_HEADS * tq, kn)


def _dsa_kernel(bq_ref, iq_ref, ikwq_ref, ikw_ref, ckv_ref, tab_ref, wuk_ref, wuv_ref, out_ref,
                key_ref, keyt_ref, smax_ref, smin_ref, npos_ref, nnon_ref, acc_ref, m_ref, ql_ref, wb_ref,
                *, tq, kn, kf, topk):
    H = DSA_HEADS
    j = pl.program_id(1)
    t0 = j * tq
    n_score = (t0 + tq - 1) // kf + 1
    n_far = jnp.maximum(t0 - tq, 0) // kf
    near_lo = n_far * (kf // kn)
    near_hi = (t0 + tq - 1) // kn + 1

    ql_all = (_dot(bq_ref[...], wuk_ref[...]) * (DSA_HEAD_DIM ** -0.5 * LOG2E)).astype(BF16)
    for h in range(H):
        ql_ref[h * tq:(h + 1) * tq, :] = ql_all[:, h * DSA_LATENT:(h + 1) * DSA_LATENT]
    w_idx = ikwq_ref[:, IKW_W_OFF:IKW_W_OFF + IDX_HEADS] * ((IDX_HEADS * IDX_DIM) ** -0.5)
    for h in range(IDX_HEADS):
        wb_ref[h] = jnp.broadcast_to(w_idx[:, h:h + 1], (tq, kf))
    iq = iq_ref[...]
    qis = [iq[:, h * IDX_DIM:(h + 1) * IDX_DIM] for h in range(IDX_HEADS)]
    tpos = t0 + lax.broadcasted_iota(jnp.int32, (tq, kf), 0)
    lpos = lax.broadcasted_iota(jnp.int32, (tq, kf), 1)

    ncol_f = kf // LANES
    tile_f = lambda a: jnp.concatenate([a] * ncol_f, axis=1)

    flip = lambda a: a ^ ((a >> 31) & 0x7FFFFFFF)
    to_key = lambda v: flip(lax.bitcast_convert_type(v, jnp.int32))
    to_val = lambda k: lax.bitcast_convert_type(flip(k), F32)
    fold = lambda a: a.reshape(kf // SUBLANES, SUBLANES, tq)

    smax_ref[...] = jnp.full(smax_ref.shape, -jnp.inf, F32)
    smin_ref[...] = jnp.full(smin_ref.shape, jnp.inf, F32)
    npos_ref[...] = jnp.zeros_like(npos_ref)
    nnon_ref[...] = jnp.zeros_like(nnon_ref)

    def score_chunk(c, masked):
        s0 = pl.multiple_of(c * kf, kf)
        kidx = ikw_ref[pl.ds(s0, kf), :][:, 0:IDX_DIM].astype(BF16)
        sc = jnp.zeros((tq, kf), F32)
        for h in range(IDX_HEADS):
            sc = sc + wb_ref[h] * jnp.maximum(_dot_nt(qis[h], kidx), 0.0)
        sc = jnp.where(jnp.abs(sc) < F32_MIN_NORMAL, 0.0, sc)
        key = to_key(sc)
        sc_t = sc.T
        sc_t_lo = sc_t
        if masked:
            causal = s0 + lpos <= tpos
            key = jnp.where(causal, key, INT_MIN)
            sc_t = jnp.where(causal, sc, -jnp.inf).T
            sc_t_lo = jnp.where(causal, sc, jnp.inf).T
        key_ref[:, pl.ds(s0, kf)] = key
        keyt_ref[pl.ds(s0, kf), :] = to_key(sc_t)
        smax_ref[...] = jnp.maximum(smax_ref[...], jnp.max(fold(sc_t), axis=0))
        smin_ref[...] = jnp.minimum(smin_ref[...], jnp.min(fold(sc_t_lo), axis=0))
        npos_ref[...] = npos_ref[...] + jnp.sum(fold(jnp.where(sc_t > 0.0, 1.0, 0.0)), axis=0)
        nnon_ref[...] = nnon_ref[...] + jnp.sum(fold(jnp.where(sc_t >= 0.0, 1.0, 0.0)), axis=0)

    def score_run(first, count):
        for u in range(count):
            score_chunk(first + u, False)

    def score_quad(i, carry):
        score_run(4 * i, 4)
        return carry

    n_plain = n_score - 1
    lax.fori_loop(0, n_plain // 4, score_quad, 0)

    @pl.when(n_plain % 4 >= 2)
    def _():
        score_run((n_plain // 4) * 4, 2)

    @pl.when(n_plain % 2 == 1)
    def _():
        score_run(n_plain - 1, 1)

    score_chunk(n_score - 1, True)

    per_query = lambda a, op: jnp.broadcast_to(op(a, axis=0, keepdims=True), (SUBLANES, tq))
    n_acc = 8

    def count_ge(cand):
        def body(c, acc):
            s0 = pl.multiple_of(c * kf, kf)
            ind = jnp.where(fold(keyt_ref[pl.ds(s0, kf), :]) >= cand[None], 1.0, 0.0)
            return acc + jnp.sum(ind.reshape(kf // SUBLANES // n_acc, n_acc, SUBLANES, tq), axis=0)
        acc = lax.fori_loop(0, n_score, body, jnp.zeros((n_acc, SUBLANES, tq), F32))
        return per_query(jnp.sum(acc, axis=0), jnp.sum)

    n_valid = (t0 + 1 + lax.broadcasted_iota(jnp.int32, (SUBLANES, tq), 1)).astype(F32)
    enough = n_valid >= topk
    lo0 = to_key(per_query(smin_ref[...], jnp.min))
    hi0 = to_key(per_query(smax_ref[...], jnp.max)) + 1
    n_pos = per_query(npos_ref[...], jnp.sum)
    n_non = per_query(nnon_ref[...], jnp.sum)
    pos_thr = n_pos >= topk
    zero_thr = jnp.logical_and(n_non >= topk, jnp.logical_not(pos_thr))
    cnt0 = jnp.where(pos_thr, n_pos, jnp.where(zero_thr, n_non, n_valid))
    lo0 = jnp.where(pos_thr, 1, jnp.where(zero_thr, 0, lo0))
    hi0 = jnp.where(pos_thr, jnp.where(n_pos == topk, 2, hi0), jnp.where(zero_thr, 1, 0))

    def probe(mid, lo, hi, cnt_lo):
        cnt = count_ge(mid)
        ge = cnt >= topk
        return (jnp.where(ge, mid, lo), jnp.where(cnt == topk, mid + 1, jnp.where(ge, hi, mid)),
                jnp.where(ge, cnt, cnt_lo))

    def by_value(i, state):
        lo, hi, cnt_lo = state
        mid = to_key(0.5 * to_val(lo) + 0.5 * to_val(hi))
        return probe(jnp.where(jnp.logical_and(mid > lo, mid < hi), mid, lo), lo, hi, cnt_lo)

    state = lax.fori_loop(0, VALUE_PASSES, by_value, (lo0, hi0, cnt0))

    def any_open(lo, hi):
        return jnp.max(jnp.where(jnp.logical_and(enough, hi > lo + 1), 1.0, 0.0))

    def by_key(state):
        lo, hi, cnt_lo, _ = state
        lo, hi, cnt_lo = probe((lo >> 1) + (hi >> 1) + (lo & hi & 1), lo, hi, cnt_lo)
        return lo, hi, cnt_lo, any_open(lo, hi)

    lo, _, cnt_lo, _ = lax.while_loop(lambda s: s[3] > 0.0, by_key, state + (any_open(state[0], state[1]),))
    thr_q = jnp.where(enough, lo, INT_MIN + 1)
    thr = jnp.broadcast_to(thr_q[0:1, :], (LANES, tq)).T

    def count_key(pred):
        def body(c, acc):
            s0 = pl.multiple_of(c * kf, kf)
            ind = jnp.where(pred(key_ref[:, pl.ds(s0, kf)]), 1.0, 0.0)
            for g in range(ncol_f):
                acc = acc + ind[:, g * LANES:(g + 1) * LANES]
            return acc
        acc = lax.fori_loop(0, n_score, body, jnp.zeros((tq, LANES), F32))
        return jnp.sum(acc, axis=1, keepdims=True)

    thr_f = tile_f(thr)
    surplus = jnp.where(enough, cnt_lo - topk, 0.0)

    @pl.when(jnp.max(surplus) > 0.0)
    def _():
        allowed = topk - count_key(lambda k: k > thr_f)
        r = lax.broadcasted_iota(jnp.int32, (kf, kf), 0)
        cidx = lax.broadcasted_iota(jnp.int32, (kf, kf), 1)
        upper = jnp.where(r <= cidx, 1.0, 0.0).astype(BF16)

        def body(c, seen):
            s0 = pl.multiple_of(c * kf, kf)
            key = key_ref[:, pl.ds(s0, kf)]
            tie = key == thr_f
            tie_f = jnp.where(tie, 1.0, 0.0)
            rank = _dot(tie_f.astype(BF16), upper) + seen
            key_ref[:, pl.ds(s0, kf)] = jnp.where(jnp.logical_and(tie, rank > allowed), thr_f - 1, key)
            return seen + jnp.sum(tie_f, axis=1, keepdims=True)
        lax.fori_loop(0, n_score, body, jnp.zeros((tq, 1), F32))

    m_ref[...] = jnp.full(m_ref.shape, M_FLOOR, F32)
    acc_ref[...] = jnp.zeros_like(acc_ref)

    def attend(s0, width, table_idx):
        ncol = width // LANES
        kv = ckv_ref[pl.ds(s0, width), :]
        kv_ext = jnp.concatenate([kv, jnp.ones((width, DSA_LATENT), BF16)], axis=1)
        thr_w = jnp.concatenate([thr] * ncol, axis=1)
        madd = jnp.where(key_ref[:, pl.ds(s0, width)] >= thr_w, 0.0, NEG_BIG)
        for h in range(H):
            rows = slice(h * tq, (h + 1) * tq)
            x = _dot_nt(ql_ref[rows, :], kv) + madd
            if table_idx is not None:
                x = x + tab_ref[table_idx, rows, :]
            m_old = m_ref[rows, :]
            m_new = jnp.maximum(m_old, jnp.max(x, axis=1, keepdims=True))
            alpha = jnp.exp2(m_old - m_new)
            p = jnp.exp2(x - jnp.concatenate([m_new] * ncol, axis=1))
            acc_ref[rows, :] = jnp.concatenate([alpha, alpha], axis=1) * acc_ref[rows, :] + _dot(p.astype(BF16), kv_ext)
            m_ref[rows, :] = m_new

    def far_run(first, count):
        for u in range(count):
            attend(pl.multiple_of((first + u) * kf, kf), kf, None)

    def far_quad(i, carry):
        far_run(4 * i, 4)
        return carry

    def near_chunk(c, carry):
        s0 = pl.multiple_of(c * kn, kn)
        attend(s0, kn, (t0 - s0) // tq)
        return carry

    lax.fori_loop(0, n_far // 4, far_quad, 0)

    @pl.when(n_far % 4 >= 2)
    def _():
        far_run((n_far // 4) * 4, 2)

    @pl.when(n_far % 2 == 1)
    def _():
        far_run(n_far - 1, 1)

    lax.fori_loop(near_lo, near_hi, near_chunk, 0)

    outs = []
    for h in range(H):
        a = acc_ref[h * tq:(h + 1) * tq, :]
        outs.append((a[:, :DSA_LATENT] / a[:, DSA_LATENT:]).astype(BF16))
    out_ref[...] = _dot(jnp.concatenate(outs, axis=1), wuv_ref[...]).astype(BF16)


def _dsa_tiles(seq):
    tq = min(DSA_TQ, seq)
    return tq, min(DSA_KN, seq), min(DSA_KF, seq)


def _dsa(bq, iq, ikw, ckv, tab, wuk_pad, wuv_pad, bsz, seq):
    tq, kn, kf = _dsa_tiles(seq)
    nq = seq // tq
    topk = min(IDX_TOPK_MAX, seq // 4)
    n_off = kf // tq + 1
    qblk = lambda w: pl.BlockSpec((tq, w), lambda b, j: (b * nq + j, 0))
    seqblk = lambda w: pl.BlockSpec((seq, w), lambda b, j: (b, 0))
    full = lambda shape: pl.BlockSpec(shape, lambda b, j: (0,) * len(shape))
    return pl.pallas_call(
        functools.partial(_dsa_kernel, tq=tq, kn=kn, kf=kf, topk=topk),
        grid=(bsz, nq),
        in_specs=[qblk(B_Q), qblk(I_Q), qblk(LANES), seqblk(LANES), seqblk(DSA_LATENT),
                  full((n_off, DSA_HEADS * tq, kn)), full((B_Q, DSA_HEADS * DSA_LATENT)),
                  full((DSA_HEADS * DSA_LATENT, B_Q))],
        out_specs=qblk(B_Q),
        out_shape=jax.ShapeDtypeStruct((bsz * seq, B_Q), BF16),
        scratch_shapes=[pltpu.VMEM((tq, seq), jnp.int32),
                        pltpu.VMEM((seq, tq), jnp.int32),
                        pltpu.VMEM((SUBLANES, tq), F32),
                        pltpu.VMEM((SUBLANES, tq), F32),
                        pltpu.VMEM((SUBLANES, tq), F32),
                        pltpu.VMEM((SUBLANES, tq), F32),
                        pltpu.VMEM((DSA_HEADS * tq, 2 * DSA_LATENT), F32),
                        pltpu.VMEM((DSA_HEADS * tq, LANES), F32),
                        pltpu.VMEM((DSA_HEADS * tq, DSA_LATENT), BF16),
                        pltpu.VMEM((IDX_HEADS, tq, kf), F32)],
        compiler_params=_cparams(("parallel", "arbitrary")),
        name="dsa",
    )(bq, iq, ikw, ikw, ckv, tab, wuk_pad, wuv_pad)


def _kv_norm_kernel(bc_ref, g_ref, out_ref):
    x = bc_ref[...]
    out_ref[...] = (x * lax.rsqrt(jnp.mean(x * x, axis=-1, keepdims=True) + LN_EPS) * g_ref[...]).astype(BF16)


def _kv_norm(bc, g):
    n = bc.shape[0]
    tm = min(4 * TOK_TILE, n)
    return pl.pallas_call(
        _kv_norm_kernel,
        grid=(n // tm,),
        in_specs=[pl.BlockSpec((tm, DSA_LATENT), lambda i: (i, 0)), pl.BlockSpec((1, DSA_LATENT), lambda i: (0, 0))],
        out_specs=pl.BlockSpec((tm, DSA_LATENT), lambda i: (i, 0)),
        out_shape=jax.ShapeDtypeStruct((n, DSA_LATENT), BF16),
        compiler_params=_cparams(("parallel",)),
        name="kv_norm",
    )(bc, g.reshape(1, DSA_LATENT))


def _merge_kernel(x_ref, ya_ref, yb_ref, ga_ref, gb_ref, wa_ref, wb_ref, wo_ref, g_ref, b_ref, out_ref):
    merged = (jax.nn.sigmoid(ga_ref[...].astype(F32)) * _dot(ya_ref[...], wa_ref[...])
              + jax.nn.sigmoid(gb_ref[...].astype(F32)) * _dot(yb_ref[...], wb_ref[...]))
    mix = _dot(merged.astype(BF16), wo_ref[...])
    out_ref[...] = _layer_norm(DN_ALPHA * x_ref[...] + mix, g_ref[...], b_ref[...])


def _merge(x2, ya, yb, ga, gb, wa, wb, wo, g, b):
    n = x2.shape[0]
    tm = min(TOK_TILE, n)
    tok = lambda w: pl.BlockSpec((tm, w), lambda i: (i, 0))
    full = lambda shape: pl.BlockSpec(shape, lambda i: (0,) * len(shape))
    return pl.pallas_call(
        _merge_kernel,
        grid=(n // tm,),
        in_specs=[tok(D_MODEL), tok(A_V), tok(B_Q), tok(D_MODEL), tok(D_MODEL),
                  full((A_V, D_MODEL)), full((B_Q, D_MODEL)), full((D_MODEL, D_MODEL)),
                  full((1, D_MODEL)), full((1, D_MODEL))],
        out_specs=tok(D_MODEL),
        out_shape=jax.ShapeDtypeStruct((n, D_MODEL), F32),
        compiler_params=_cparams(("parallel",)),
        name="merge_ln",
    )(x2, ya, yb, ga, gb, wa, wb, wo, g.reshape(1, D_MODEL), b.reshape(1, D_MODEL))


def _dense_ffn_kernel(x_ref, wg_ref, wu_ref, wd_ref, g_ref, b_ref, out_ref):
    x = x_ref[...]
    xb = x.astype(BF16)
    f = jnp.zeros(x.shape, F32)
    for c in range(D_FF // FF_CHUNK):
        cols = slice(c * FF_CHUNK, (c + 1) * FF_CHUNK)
        hdn = _silu(_dot(xb, wg_ref[:, cols])) * _dot(xb, wu_ref[:, cols])
        f = f + _dot(hdn.astype(BF16), wd_ref[cols, :])
    out_ref[...] = _layer_norm(DN_ALPHA * x + f, g_ref[...], b_ref[...])


def _dense_ffn(x2, wg, wu, wd, g, b):
    n = x2.shape[0]
    tm = min(TOK_TILE, n)
    tok = pl.BlockSpec((tm, D_MODEL), lambda i: (i, 0))
    full = lambda shape: pl.BlockSpec(shape, lambda i: (0,) * len(shape))
    return pl.pallas_call(
        _dense_ffn_kernel,
        grid=(n // tm,),
        in_specs=[tok, full((D_MODEL, D_FF)), full((D_MODEL, D_FF)), full((D_FF, D_MODEL)),
                  full((1, D_MODEL)), full((1, D_MODEL))],
        out_specs=tok,
        out_shape=jax.ShapeDtypeStruct((n, D_MODEL), F32),
        compiler_params=_cparams(("parallel",)),
        name="dense_ffn_ln",
    )(x2, wg, wu, wd, g.reshape(1, D_MODEL), b.reshape(1, D_MODEL))


def _router_kernel(x_ref, rw_ref, e_ref, g_ref):
    logits = lax.dot_general(rw_ref[...], x_ref[...], NT_DIMS, preferred_element_type=F32,
                             precision=lax.Precision.HIGHEST)
    eidx = lax.broadcasted_iota(jnp.int32, logits.shape, 0)
    m1 = jnp.max(logits, axis=0, keepdims=True)
    i1 = jnp.min(jnp.where(logits == m1, eidx, N_EXPERTS), axis=0, keepdims=True)
    rest = jnp.where(eidx == i1, -jnp.inf, logits)
    m2 = jnp.max(rest, axis=0, keepdims=True)
    i2 = jnp.min(jnp.where(rest == m2, eidx, N_EXPERTS), axis=0, keepdims=True)
    e2 = jnp.exp(m2 - m1)
    den = 1.0 + e2
    e_ref[0:1, :] = i1
    e_ref[1:2, :] = i2
    g_ref[0:1, :] = 1.0 / den
    g_ref[1:2, :] = e2 / den


def _router(x2, rw_t):
    n = x2.shape[0]
    tm = min(TOK_TILE, n)
    return pl.pallas_call(
        _router_kernel,
        grid=(n // tm,),
        in_specs=[pl.BlockSpec((tm, D_MODEL), lambda i: (i, 0)), pl.BlockSpec((N_EXPERTS, D_MODEL), lambda i: (0, 0))],
        out_specs=[pl.BlockSpec((TOP_K, tm), lambda i: (0, i)), pl.BlockSpec((TOP_K, tm), lambda i: (0, i))],
        out_shape=[jax.ShapeDtypeStruct((TOP_K, n), jnp.int32), jax.ShapeDtypeStruct((TOP_K, n), F32)],
        compiler_params=_cparams(("parallel",)),
        name="router",
    )(x2, rw_t)


def _row_copy(src, src_row, dst, dst_row, sem):
    return pltpu.make_async_copy(src.at[pl.ds(src_row, 1)], dst.at[pl.ds(dst_row, 1)], sem)


def _dispatch_kernel(d0_ref, d1_ref, x_ref, xs_init_hbm, xs_hbm, sem, *, tile):
    del xs_init_hbm

    def start(r, c):
        _row_copy(x_ref, r, xs_hbm, d0_ref[0, r], sem).start()
        _row_copy(x_ref, r, xs_hbm, d1_ref[0, r], sem).start()
        return c
    lax.fori_loop(0, tile, start, 0, unroll=DMA_UNROLL)

    def wait(r, c):
        _row_copy(x_ref, r, xs_hbm, 0, sem).wait()
        _row_copy(x_ref, r, xs_hbm, 0, sem).wait()
        return c
    lax.fori_loop(0, tile, wait, 0, unroll=DMA_UNROLL)


def _dispatch(x2, pos, total):
    n = x2.shape[0]
    tile = min(COMB_TILE, n)
    nb = n // tile
    smem = pl.BlockSpec((None, 1, tile), lambda i: (i, 0, 0), memory_space=pltpu.SMEM)
    return pl.pallas_call(
        functools.partial(_dispatch_kernel, tile=tile),
        grid=(nb,),
        in_specs=[smem, smem, pl.BlockSpec((tile, D_MODEL), lambda i: (i, 0)), pl.BlockSpec(memory_space=pl.ANY)],
        out_specs=pl.BlockSpec(memory_space=pl.ANY),
        out_shape=jax.ShapeDtypeStruct((total, D_MODEL), F32),
        scratch_shapes=[pltpu.SemaphoreType.DMA(())],
        input_output_aliases={3: 0},
        compiler_params=_cparams(("arbitrary",)),
        name="moe_dispatch",
    )(pos[:, 0].reshape(nb, 1, tile), pos[:, 1].reshape(nb, 1, tile), x2, jnp.zeros((total, D_MODEL), F32))


def _expert_kernel(blk_e_ref, nused_ref, xs_ref, wg_ref, wu_ref, wd_ref, y_ref, xb, acc):
    i = pl.program_id(0)
    f = pl.program_id(1)
    used = i < nused_ref[0]
    last = f == pl.num_programs(1) - 1

    @pl.when(jnp.logical_and(used, f == 0))
    def _():
        xb[...] = xs_ref[...].astype(BF16)
        acc[...] = jnp.zeros_like(acc)

    @pl.when(used)
    def _():
        x = xb[...]
        hdn = _silu(_dot(x, wg_ref[...])) * _dot(x, wu_ref[...])
        acc[...] += _dot(hdn.astype(BF16), wd_ref[...])

    @pl.when(jnp.logical_and(used, last))
    def _():
        y_ref[...] = acc[...]

    @pl.when(jnp.logical_and(jnp.logical_not(used), last))
    def _():
        y_ref[...] = jnp.zeros_like(y_ref)


def _experts(xs, blk_e, nused, wg, wu, wd):
    total = xs.shape[0]
    tile = MOE_TILE
    nblk = total // tile
    nf = D_FF_EXPERT // MOE_FT
    grid_spec = pltpu.PrefetchScalarGridSpec(
        num_scalar_prefetch=2,
        grid=(nblk, nf),
        in_specs=[pl.BlockSpec((tile, D_MODEL), lambda i, f, be, nu: (i, 0)),
                  pl.BlockSpec((None, D_MODEL, MOE_FT), lambda i, f, be, nu: (be[i], 0, f)),
                  pl.BlockSpec((None, D_MODEL, MOE_FT), lambda i, f, be, nu: (be[i], 0, f)),
                  pl.BlockSpec((None, MOE_FT, D_MODEL), lambda i, f, be, nu: (be[i], f, 0))],
        out_specs=pl.BlockSpec((tile, D_MODEL), lambda i, f, be, nu: (i, 0)),
        scratch_shapes=[pltpu.VMEM((tile, D_MODEL), BF16), pltpu.VMEM((tile, D_MODEL), F32)],
    )
    return pl.pallas_call(
        _expert_kernel,
        grid_spec=grid_spec,
        out_shape=jax.ShapeDtypeStruct((total, D_MODEL), F32),
        compiler_params=_cparams(("arbitrary", "arbitrary")),
        name="experts",
    )(blk_e, nused, xs, wg, wu, wd)


def _combine_kernel(p0_ref, p1_ref, y_hbm, x_ref, gate_ref, g_ref, b_ref, out_ref, buf0, buf1, sem, *, tile):
    def start(r, c):
        _row_copy(y_hbm, p0_ref[0, r], buf0, r, sem).start()
        _row_copy(y_hbm, p1_ref[0, r], buf1, r, sem).start()
        return c
    lax.fori_loop(0, tile, start, 0, unroll=DMA_UNROLL)

    def wait(r, c):
        _row_copy(y_hbm, 0, buf0, r, sem).wait()
        _row_copy(y_hbm, 0, buf1, r, sem).wait()
        return c
    lax.fori_loop(0, tile, wait, 0, unroll=DMA_UNROLL)
    f = buf0[...] * gate_ref[:, 0:1] + buf1[...] * gate_ref[:, 1:2]
    out_ref[...] = _layer_norm(DN_ALPHA * x_ref[...] + f, g_ref[...], b_ref[...])


def _combine(y, pos, gates, x2, g, b):
    n = x2.shape[0]
    tile = min(COMB_TILE, n)
    nb = n // tile
    smem = pl.BlockSpec((None, 1, tile), lambda i: (i, 0, 0), memory_space=pltpu.SMEM)
    full = pl.BlockSpec((1, D_MODEL), lambda i: (0, 0))
    return pl.pallas_call(
        functools.partial(_combine_kernel, tile=tile),
        grid=(nb,),
        in_specs=[smem, smem, pl.BlockSpec(memory_space=pl.ANY),
                  pl.BlockSpec((tile, D_MODEL), lambda i: (i, 0)),
                  pl.BlockSpec((tile, TOP_K), lambda i: (i, 0)), full, full],
        out_specs=pl.BlockSpec((tile, D_MODEL), lambda i: (i, 0)),
        out_shape=jax.ShapeDtypeStruct((n, D_MODEL), F32),
        scratch_shapes=[pltpu.VMEM((tile, D_MODEL), F32), pltpu.VMEM((tile, D_MODEL), F32),
                        pltpu.SemaphoreType.DMA(())],
        compiler_params=_cparams(("arbitrary",)),
        name="moe_combine_ln",
    )(pos[:, 0].reshape(nb, 1, tile), pos[:, 1].reshape(nb, 1, tile), y, x2, gates,
      g.reshape(1, D_MODEL), b.reshape(1, D_MODEL))


def _moe(x2, router_w, wg, wu, wd, g, b):
    n = x2.shape[0]
    top_e, gates = _router(x2, router_w.T)
    flat_e = top_e.T.reshape(-1)
    onehot = (flat_e[:, None] == jnp.arange(N_EXPERTS, dtype=jnp.int32)[None, :]).astype(jnp.int32)
    rank = jnp.sum((jnp.cumsum(onehot, axis=0) - onehot) * onehot, axis=1)
    counts = jnp.sum(onehot, axis=0)
    padded = ((counts + MOE_TILE - 1) // MOE_TILE) * MOE_TILE
    pad_end = jnp.cumsum(padded)
    pad_start = pad_end - padded
    pos = (pad_start[flat_e] + rank).astype(jnp.int32).reshape(n, TOP_K)
    total = n * TOP_K + N_EXPERTS * MOE_TILE
    nblk = total // MOE_TILE
    blk_e = jnp.minimum(jnp.searchsorted(pad_end, jnp.arange(nblk, dtype=jnp.int32) * MOE_TILE, side='right'),
                        N_EXPERTS - 1).astype(jnp.int32)
    nused = (pad_end[-1:] // MOE_TILE).astype(jnp.int32)
    xs = _dispatch(x2, pos, total)
    y = _experts(xs, blk_e, nused, wg, wu, wd)
    return _combine(y, pos, gates.T, x2, g, b)


def _pad_head_proj(w_uk, w_uv):
    H, DH, DC = DSA_HEADS, DSA_HEAD_DIM, DSA_LATENT
    eye = jnp.eye(H, dtype=w_uk.dtype)
    wuk_bd = jnp.einsum('hg,hdc->hdgc', eye, w_uk).reshape(H * DH, H * DC)
    wuv_bd = jnp.einsum('hg,hcd->hcgd', eye, w_uv).reshape(H * DC, H * DH)
    return wuk_bd.astype(BF16), wuv_bd.astype(BF16)


def kernel(x, w_in, mlstm_conv_w, mlstm_gate_bias, mlstm_norm_g, dsa_kv_norm_g, dsa_w_uk, dsa_w_uv, rel_bias,
           w_branch_a, w_branch_b, w_out, ln_g, ln_b, dense_w_gate, dense_w_up, dense_w_down, router_w,
           expert_w_gate, expert_w_up, expert_w_down):
    bsz, seq, d = x.shape
    n = bsz * seq
    x2 = x.reshape(n, d)
    tab = _near_bias_tables(rel_bias, *_dsa_tiles(seq))
    for l in range(DEPTH):
        p = dict(zip([nm for nm, _, _ in IN_PROJ_OUTS], _in_proj(x2, _regroup_in_proj(w_in[l]))))
        ya = _mlstm(p["qk"], p["av"], p["aif"], p["ao"], mlstm_conv_w[l], mlstm_gate_bias[l], mlstm_norm_g[l],
                    bsz, seq)
        ckv = _kv_norm(p["bc"], dsa_kv_norm_g[l])
        wuk_pad, wuv_pad = _pad_head_proj(dsa_w_uk[l], dsa_w_uv[l])
        yb = _dsa(p["bq"], p["iq"], p["ikw"], ckv, tab, wuk_pad, wuv_pad, bsz, seq)
        x2 = _merge(x2, ya, yb, p["ga"], p["gb"], w_branch_a[l].astype(BF16), w_branch_b[l].astype(BF16),
                    w_out[l].astype(BF16), ln_g[l, 0], ln_b[l, 0])
        j = l // 2
        if l % 2 == 0:
            x2 = _dense_ffn(x2, dense_w_gate[j].astype(BF16), dense_w_up[j].astype(BF16),
                            dense_w_down[j].astype(BF16), ln_g[l, 1], ln_b[l, 1])
        else:
            x2 = _moe(x2, router_w[j], expert_w_gate[j].astype(BF16), expert_w_up[j].astype(BF16),
                      expert_w_down[j].astype(BF16), ln_g[l, 1], ln_b[l, 1])
    return x2.reshape(bsz, seq, d)
```

```python
import functools
import math

import numpy as np
import jax
import jax.numpy as jnp
from jax import lax
from jax.experimental import pallas as pl
from jax.experimental.pallas import tpu as pltpu

D_MODEL = 1024
DEPTH = 4
MLSTM_HEADS = 4
MLSTM_QK_DIM = 64
MLSTM_V_DIM = 128
MLSTM_CONV = 4
DSA_HEADS = 8
DSA_HEAD_DIM = 64
DSA_LATENT = 128
IDX_HEADS = 8
IDX_DIM = 32
IDX_TOPK_MAX = 256
REL_BUCKETS = 32
REL_MAX_DIST = 128
D_FF = 2816
N_EXPERTS = 8
TOP_K = 2
D_FF_EXPERT = 3584
DN_ALPHA = (2 * DEPTH) ** 0.25
LN_EPS = 1e-5

A_QK = 2 * MLSTM_HEADS * MLSTM_QK_DIM
A_V = MLSTM_HEADS * MLSTM_V_DIM
A_GATE = 2 * MLSTM_HEADS
B_Q = DSA_HEADS * DSA_HEAD_DIM
I_Q = IDX_HEADS * IDX_DIM
PROJ_SIZES = (A_QK, A_V, A_V, A_GATE, B_Q, DSA_LATENT, I_Q, IDX_DIM, IDX_HEADS, D_MODEL, D_MODEL)

LANES = 128
SUBLANES = 8
VMEM_LIMIT = 56 * 1024 * 1024
TOK_TILE = 512
MLSTM_TILE = 256
DSA_TQ = 128
DSA_KN = 256
DSA_KF = 512
MOE_TILE = 512
MOE_FT = 896
COMB_TILE = 256
DMA_UNROLL = 8
FF_CHUNK = 256
INT_MIN = -(2 ** 31)
NEG_BIG = -1e30
M_FLOOR = -1e29
LOG2E = math.log2(math.e)
F32_MIN_NORMAL = 2.0 ** -126
VALUE_PASSES = 16

BF16 = jnp.bfloat16
F32 = jnp.float32
NT_DIMS = (((1,), (1,)), ((), ()))


def _cparams(sem):
    return pltpu.CompilerParams(dimension_semantics=sem, vmem_limit_bytes=VMEM_LIMIT)


def _dot(a, b):
    return jnp.dot(a, b, preferred_element_type=F32)


def _dot_nt(a, b):
    return lax.dot_general(a, b, NT_DIMS, preferred_element_type=F32)


def _layer_norm(r, g, b):
    mu = jnp.mean(r, axis=-1, keepdims=True)
    d = r - mu
    var = jnp.mean(d * d, axis=-1, keepdims=True)
    return d * lax.rsqrt(var + LN_EPS) * g + b


def _silu(x):
    return x * jax.nn.sigmoid(x)


IN_PROJ_OUTS = (("qk", A_QK, F32), ("av", A_V, BF16), ("ao", A_V, BF16), ("bq", B_Q, BF16),
                ("iq", I_Q, BF16), ("ga", D_MODEL, BF16), ("gb", D_MODEL, BF16),
                ("aif", LANES, F32), ("bc", LANES, F32), ("ikw", LANES, F32))
IN_PROJ_COLS = sum(w for _, w, _ in IN_PROJ_OUTS)
IKW_W_OFF = IDX_DIM


def _regroup_in_proj(w):
    offs = np.concatenate([[0], np.cumsum(PROJ_SIZES)])
    a_qk, a_v, a_o, a_if, b_q, b_c, i_q, i_k, i_w, g_a, g_b = (w[:, offs[i]:offs[i + 1]] for i in range(11))
    pad = lambda a: jnp.pad(a, ((0, 0), (0, LANES - a.shape[1])))
    groups = [a_qk, a_v, a_o, b_q, i_q, g_a, g_b, pad(a_if), b_c, pad(jnp.concatenate([i_k, i_w], axis=1))]
    return jnp.concatenate(groups, axis=1).astype(BF16)


def _in_proj_kernel(x_ref, w_ref, *out_refs):
    xb = x_ref[...].astype(BF16)
    off = 0
    for (_, width, dtype), o_ref in zip(IN_PROJ_OUTS, out_refs):
        o_ref[...] = _dot(xb, w_ref[:, off:off + width]).astype(dtype)
        off += width


def _in_proj(x2, w):
    n = x2.shape[0]
    tm = min(TOK_TILE, n)
    return pl.pallas_call(
        _in_proj_kernel,
        grid=(n // tm,),
        in_specs=[pl.BlockSpec((tm, D_MODEL), lambda i: (i, 0)),
                  pl.BlockSpec((D_MODEL, IN_PROJ_COLS), lambda i: (0, 0))],
        out_specs=[pl.BlockSpec((tm, wd), lambda i: (i, 0)) for _, wd, _ in IN_PROJ_OUTS],
        out_shape=[jax.ShapeDtypeStruct((n, wd), dt) for _, wd, dt in IN_PROJ_OUTS],
        compiler_params=_cparams(("parallel",)),
        name="in_proj",
    )(x2, w)


def _log_sigmoid(x):
    return jnp.minimum(x, 0.0) - jnp.log1p(jnp.exp(-jnp.abs(x)))


def _mlstm_kernel(qk_ref, av_ref, aif_ref, ao_ref, convw_ref, gbias_ref, ng_ref, out_ref,
                  ext_ref, ct_ref, m_ref, *, tile):
    L = tile
    H, DK, DV = MLSTM_HEADS, MLSTM_QK_DIM, MLSTM_V_DIM

    @pl.when(pl.program_id(1) == 0)
    def _():
        ext_ref[0:SUBLANES, :] = jnp.zeros((SUBLANES, A_QK), F32)
        ct_ref[...] = jnp.zeros_like(ct_ref)
        m_ref[...] = jnp.zeros_like(m_ref)

    u = qk_ref[...]
    ext_ref[SUBLANES:SUBLANES + L, :] = u
    conv = jnp.zeros((L, A_QK), F32)
    for j in range(MLSTM_CONV):
        conv = conv + ext_ref[pl.ds(SUBLANES - (MLSTM_CONV - 1) + j, L), :] * convw_ref[j:j + 1, :]
    ext_ref[0:SUBLANES, :] = u[L - SUBLANES:L, :]
    qk = _silu(conv)

    g_col = aif_ref[...] + gbias_ref[...]
    g_row = g_col.T
    row = lax.broadcasted_iota(jnp.int32, (L, L), 0)
    col = lax.broadcasted_iota(jnp.int32, (L, L), 1)
    causal = row >= col
    tril = jnp.where(causal, 1.0, 0.0).astype(F32)
    triu = jnp.where(row <= col, 1.0, 0.0).astype(F32)
    b_col = jnp.dot(tril, _log_sigmoid(g_col), preferred_element_type=F32, precision=lax.Precision.HIGHEST)
    b_row = jnp.dot(_log_sigmoid(g_row), triu, preferred_element_type=F32, precision=lax.Precision.HIGHEST)

    ones_v = jnp.ones((L, DV), BF16)
    ones_t = jnp.ones((DV, L), F32)
    for h in range(H):
        bcol = b_col[:, H + h:H + h + 1]
        icol = g_col[:, h:h + 1]
        brow = b_row[H + h:H + h + 1, :]
        irow = g_row[h:h + 1, :]
        mprev = m_ref[h:h + 1, 0:1]
        dlog = jnp.where(causal, bcol - brow + irow, -jnp.inf)
        m_t = jnp.maximum(bcol + mprev, jnp.max(dlog, axis=1, keepdims=True))
        q = qk[:, h * DK:(h + 1) * DK].astype(BF16)
        k = qk[:, H * DK + h * DK:H * DK + (h + 1) * DK] * (DK ** -0.5)
        s_w = _dot_nt(q, k.astype(BF16)) * jnp.exp(dlog - m_t)
        inter = jnp.exp(bcol + mprev - m_t)
        v_h = av_ref[:, h * DV:(h + 1) * DV]
        v_ext = jnp.concatenate([v_h, ones_v], axis=1)
        ct_st = ct_ref[h]
        ne = _dot(s_w.astype(BF16), v_ext) + inter * _dot_nt(q, ct_st.astype(BF16))
        h_out = ne[:, :DV] / jnp.maximum(jnp.abs(ne[:, DV:]), jnp.exp(-m_t))
        blast = bcol[L - 1:L, :]
        glog = blast - bcol + icol
        m_new = jnp.maximum(blast + mprev, jnp.max(glog, axis=0, keepdims=True))
        decay = jnp.exp(blast + mprev - m_new)
        w_k = k * jnp.exp(glog - m_new)
        v_ext_t = jnp.concatenate([v_h.astype(F32).T, ones_t], axis=0).astype(BF16)
        ct_ref[h] = decay * ct_st + _dot(v_ext_t, w_k.astype(BF16))
        m_ref[h:h + 1, :] = jnp.broadcast_to(m_new, (1, LANES))
        mu = jnp.mean(h_out, axis=-1, keepdims=True)
        d = h_out - mu
        hn = d * lax.rsqrt(jnp.mean(d * d, axis=-1, keepdims=True) + LN_EPS)
        y = hn * ng_ref[:, h * DV:(h + 1) * DV] * jax.nn.sigmoid(ao_ref[:, h * DV:(h + 1) * DV].astype(F32))
        out_ref[:, h * DV:(h + 1) * DV] = y.astype(BF16)


def _mlstm(qk, av, aif, ao, conv_w, gate_bias, norm_g, bsz, seq):
    tile = min(MLSTM_TILE, seq)
    nc = seq // tile
    gbias = jnp.zeros((1, LANES), F32).at[0, :A_GATE].set(gate_bias.reshape(-1))
    tok = lambda w: pl.BlockSpec((tile, w), lambda b, c: (b * nc + c, 0))
    full = lambda shape: pl.BlockSpec(shape, lambda b, c: (0,) * len(shape))
    return pl.pallas_call(
        functools.partial(_mlstm_kernel, tile=tile),
        grid=(bsz, nc),
        in_specs=[tok(A_QK), tok(A_V), tok(LANES), tok(A_V),
                  full((MLSTM_CONV, A_QK)), full((1, LANES)), full((1, A_V))],
        out_specs=tok(A_V),
        out_shape=jax.ShapeDtypeStruct((bsz * seq, A_V), BF16),
        scratch_shapes=[pltpu.VMEM((tile + SUBLANES, A_QK), F32),
                        pltpu.VMEM((MLSTM_HEADS, 2 * MLSTM_V_DIM, MLSTM_QK_DIM), F32),
                        pltpu.VMEM((SUBLANES, LANES), F32)],
        compiler_params=_cparams(("parallel", "arbitrary")),
        name="mlstm",
    )(qk, av, aif, ao, conv_w, gbias, norm_g.reshape(1, A_V))


def _t5_bucket_np(dist):
    dist = np.maximum(dist, 0)
    exact = REL_BUCKETS // 2
    log_ratio = (np.log(np.maximum(dist, 1).astype(np.float32) / np.float32(exact))
                 / np.float32(math.log(REL_MAX_DIST / exact))).astype(np.float32)
    large = np.minimum(exact + (log_ratio * np.float32(REL_BUCKETS - exact)).astype(np.int32), REL_BUCKETS - 1)
    return np.where(dist < exact, dist, large)


def _near_bias_tables(rel_bias, tq, kn, kf):
    n_off = kf // tq + 1
    assert np.all(_t5_bucket_np(np.arange(tq + 1, 4 * REL_MAX_DIST)) == REL_BUCKETS - 1)
    d_max = (n_off - 1) * tq + tq - 1
    dist = np.arange(d_max, -kn, -1)
    by_dist = (rel_bias[_t5_bucket_np(dist)] - rel_bias[REL_BUCKETS - 1]) * LOG2E
    by_dist = jnp.where((dist >= 0)[:, None], by_dist, 0.0).T.astype(F32)
    rows = [[by_dist[:, d_max - (o * tq + t):d_max - (o * tq + t) + kn] for t in range(tq)] for o in range(n_off)]
    tab = jnp.stack([jnp.stack(r, axis=1) for r in rows], axis=0)
    return tab.reshape(n_off, DSA_HEADS * tq, kn)


def _dsa_kernel(bq_ref, iq_ref, ikwq_ref, ikw_ref, ckv_ref, tab_ref, wuk_ref, wuv_ref, out_ref,
                key_ref, keyt_ref, smax_ref, smin_ref, npos_ref, nnon_ref, acc_ref, m_ref, ql_ref, wb_ref,
                *, tq, kn, kf, topk):
    H = DSA_HEADS
    j = pl.program_id(1)
    t0 = j * tq
    n_score = (t0 + tq - 1) // kf + 1
    n_far = jnp.maximum(t0 - tq, 0) // kf
    near_lo = n_far * (kf // kn)
    near_hi = (t0 + tq - 1) // kn + 1

    ql_all = (_dot(bq_ref[...], wuk_ref[...]) * (DSA_HEAD_DIM ** -0.5 * LOG2E)).astype(BF16)
    for h in range(H):
        ql_ref[h * tq:(h + 1) * tq, :] = ql_all[:, h * DSA_LATENT:(h + 1) * DSA_LATENT]
    w_idx = ikwq_ref[:, IKW_W_OFF:IKW_W_OFF + IDX_HEADS] * ((IDX_HEADS * IDX_DIM) ** -0.5)
    for h in range(IDX_HEADS):
        wb_ref[h] = jnp.broadcast_to(w_idx[:, h:h + 1], (tq, kf))
    iq = iq_ref[...]
    qis = [iq[:, h * IDX_DIM:(h + 1) * IDX_DIM] for h in range(IDX_HEADS)]
    tpos = t0 + lax.broadcasted_iota(jnp.int32, (tq, kf), 0)
    lpos = lax.broadcasted_iota(jnp.int32, (tq, kf), 1)

    ncol_f = kf // LANES
    tile_f = lambda a: jnp.concatenate([a] * ncol_f, axis=1)

    flip = lambda a: a ^ ((a >> 31) & 0x7FFFFFFF)
    to_key = lambda v: flip(lax.bitcast_convert_type(v, jnp.int32))
    to_val = lambda k: lax.bitcast_convert_type(flip(k), F32)
    fold = lambda a: a.reshape(kf // SUBLANES, SUBLANES, tq)

    smax_ref[...] = jnp.full(smax_ref.shape, -jnp.inf, F32)
    smin_ref[...] = jnp.full(smin_ref.shape, jnp.inf, F32)
    npos_ref[...] = jnp.zeros_like(npos_ref)
    nnon_ref[...] = jnp.zeros_like(nnon_ref)

    def score_chunk(c, masked):
        s0 = pl.multiple_of(c * kf, kf)
        kidx = ikw_ref[pl.ds(s0, kf), :][:, 0:IDX_DIM].astype(BF16)
        sc = jnp.zeros((tq, kf), F32)
        for h in range(IDX_HEADS):
            sc = sc + wb_ref[h] * jnp.maximum(_dot_nt(qis[h], kidx), 0.0)
        sc = jnp.where(jnp.abs(sc) < F32_MIN_NORMAL, 0.0, sc)
        key = to_key(sc)
        sc_t = sc.T
        sc_t_lo = sc_t
        if masked:
            causal = s0 + lpos <= tpos
            key = jnp.where(causal, key, INT_MIN)
            sc_t = jnp.where(causal, sc, -jnp.inf).T
            sc_t_lo = jnp.where(causal, sc, jnp.inf).T
        key_ref[:, pl.ds(s0, kf)] = key
        keyt_ref[pl.ds(s0, kf), :] = to_key(sc_t)
        smax_ref[...] = jnp.maximum(smax_ref[...], jnp.max(fold(sc_t), axis=0))
        smin_ref[...] = jnp.minimum(smin_ref[...], jnp.min(fold(sc_t_lo), axis=0))
        npos_ref[...] = npos_ref[...] + jnp.sum(fold(jnp.where(sc_t > 0.0, 1.0, 0.0)), axis=0)
        nnon_ref[...] = nnon_ref[...] + jnp.sum(fold(jnp.where(sc_t >= 0.0, 1.0, 0.0)), axis=0)

    def score_run(first, count):
        for u in range(count):
            score_chunk(first + u, False)

    def score_quad(i, carry):
        score_run(4 * i, 4)
        return carry

    n_plain = n_score - 1
    lax.fori_loop(0, n_plain // 4, score_quad, 0)

    @pl.when(n_plain % 4 >= 2)
    def _():
        score_run((n_plain // 4) * 4, 2)

    @pl.when(n_plain % 2 == 1)
    def _():
        score_run(n_plain - 1, 1)

    score_chunk(n_score - 1, True)

    per_query = lambda a, op: jnp.broadcast_to(op(a, axis=0, keepdims=True), (SUBLANES, tq))
    n_acc = 8

    def count_ge(cand):
        def body(c, acc):
            s0 = pl.multiple_of(c * kf, kf)
            ind = jnp.where(fold(keyt_ref[pl.ds(s0, kf), :]) >= cand[None], 1.0, 0.0)
            return acc + jnp.sum(ind.reshape(kf // SUBLANES // n_acc, n_acc, SUBLANES, tq), axis=0)
        acc = lax.fori_loop(0, n_score, body, jnp.zeros((n_acc, SUBLANES, tq), F32))
        return per_query(jnp.sum(acc, axis=0), jnp.sum)

    n_valid = (t0 + 1 + lax.broadcasted_iota(jnp.int32, (SUBLANES, tq), 1)).astype(F32)
    enough = n_valid >= topk
    lo0 = to_key(per_query(smin_ref[...], jnp.min))
    hi0 = to_key(per_query(smax_ref[...], jnp.max)) + 1
    n_pos = per_query(npos_ref[...], jnp.sum)
    n_non = per_query(nnon_ref[...], jnp.sum)
    pos_thr = n_pos >= topk
    zero_thr = jnp.logical_and(n_non >= topk, jnp.logical_not(pos_thr))
    cnt0 = jnp.where(pos_thr, n_pos, jnp.where(zero_thr, n_non, n_valid))
    lo0 = jnp.where(pos_thr, 1, jnp.where(zero_thr, 0, lo0))
    hi0 = jnp.where(pos_thr, jnp.where(n_pos == topk, 2, hi0), jnp.where(zero_thr, 1, 0))

    def probe(mid, lo, hi, cnt_lo):
        cnt = count_ge(mid)
        ge = cnt >= topk
        return (jnp.where(ge, mid, lo), jnp.where(cnt == topk, mid + 1, jnp.where(ge, hi, mid)),
                jnp.where(ge, cnt, cnt_lo))

    def by_value(i, state):
        lo, hi, cnt_lo = state
        mid = to_key(0.5 * to_val(lo) + 0.5 * to_val(hi))
        return probe(jnp.where(jnp.logical_and(mid > lo, mid < hi), mid, lo), lo, hi, cnt_lo)

    state = lax.fori_loop(0, VALUE_PASSES, by_value, (lo0, hi0, cnt0))

    def any_open(lo, hi):
        return jnp.max(jnp.where(jnp.logical_and(enough, hi > lo + 1), 1.0, 0.0))

    def by_key(state):
        lo, hi, cnt_lo, _ = state
        lo, hi, cnt_lo = probe((lo >> 1) + (hi >> 1) + (lo & hi & 1), lo, hi, cnt_lo)
        return lo, hi, cnt_lo, any_open(lo, hi)

    lo, _, cnt_lo, _ = lax.while_loop(lambda s: s[3] > 0.0, by_key, state + (any_open(state[0], state[1]),))
    thr_q = jnp.where(enough, lo, INT_MIN + 1)
    thr = jnp.broadcast_to(thr_q[0:1, :], (LANES, tq)).T

    def count_key(pred):
        def body(c, acc):
            s0 = pl.multiple_of(c * kf, kf)
            ind = jnp.where(pred(key_ref[:, pl.ds(s0, kf)]), 1.0, 0.0)
            for g in range(ncol_f):
                acc = acc + ind[:, g * LANES:(g + 1) * LANES]
            return acc
        acc = lax.fori_loop(0, n_score, body, jnp.zeros((tq, LANES), F32))
        return jnp.sum(acc, axis=1, keepdims=True)

    thr_f = tile_f(thr)
    surplus = jnp.where(enough, cnt_lo - topk, 0.0)

    @pl.when(jnp.max(surplus) > 0.0)
    def _():
        allowed = topk - count_key(lambda k: k > thr_f)
        r = lax.broadcasted_iota(jnp.int32, (kf, kf), 0)
        cidx = lax.broadcasted_iota(jnp.int32, (kf, kf), 1)
        upper = jnp.where(r <= cidx, 1.0, 0.0).astype(BF16)

        def body(c, seen):
            s0 = pl.multiple_of(c * kf, kf)
            key = key_ref[:, pl.ds(s0, kf)]
            tie = key == thr_f
            tie_f = jnp.where(tie, 1.0, 0.0)
            rank = _dot(tie_f.astype(BF16), upper) + seen
            key_ref[:, pl.ds(s0, kf)] = jnp.where(jnp.logical_and(tie, rank > allowed), thr_f - 1, key)
            return seen + jnp.sum(tie_f, axis=1, keepdims=True)
        lax.fori_loop(0, n_score, body, jnp.zeros((tq, 1), F32))

    m_ref[...] = jnp.full(m_ref.shape, M_FLOOR, F32)
    acc_ref[...] = jnp.zeros_like(acc_ref)

    def attend(s0, width, table_idx):
        ncol = width // LANES
        kv = ckv_ref[pl.ds(s0, width), :]
        kv_ext = jnp.concatenate([kv, jnp.ones((width, DSA_LATENT), BF16)], axis=1)
        thr_w = jnp.concatenate([thr] * ncol, axis=1)
        madd = jnp.where(key_ref[:, pl.ds(s0, width)] >= thr_w, 0.0, NEG_BIG)
        for h in range(H):
            rows = slice(h * tq, (h + 1) * tq)
            x = _dot_nt(ql_ref[rows, :], kv) + madd
            if table_idx is not None:
                x = x + tab_ref[table_idx, rows, :]
            m_old = m_ref[rows, :]
            m_new = jnp.maximum(m_old, jnp.max(x, axis=1, keepdims=True))
            alpha = jnp.exp2(m_old - m_new)
            p = jnp.exp2(x - jnp.concatenate([m_new] * ncol, axis=1))
            acc_ref[rows, :] = jnp.concatenate([alpha, alpha], axis=1) * acc_ref[rows, :] + _dot(p.astype(BF16), kv_ext)
            m_ref[rows, :] = m_new

    def far_run(first, count):
        for u in range(count):
            attend(pl.multiple_of((first + u) * kf, kf), kf, None)

    def far_quad(i, carry):
        far_run(4 * i, 4)
        return carry

    def near_chunk(c, carry):
        s0 = pl.multiple_of(c * kn, kn)
        attend(s0, kn, (t0 - s0) // tq)
        return carry

    lax.fori_loop(0, n_far // 4, far_quad, 0)

    @pl.when(n_far % 4 >= 2)
    def _():
        far_run((n_far // 4) * 4, 2)

    @pl.when(n_far % 2 == 1)
    def _():
        far_run(n_far - 1, 1)

    lax.fori_loop(near_lo, near_hi, near_chunk, 0)

    outs = []
    for h in range(H):
        a = acc_ref[h * tq:(h + 1) * tq, :]
        outs.append((a[:, :DSA_LATENT] / a[:, DSA_LATENT:]).astype(BF16))
    out_ref[...] = _dot(jnp.concatenate(outs, axis=1), wuv_ref[...]).astype(BF16)


def _dsa_tiles(seq):
    tq = min(DSA_TQ, seq)
    return tq, min(DSA_KN, seq), min(DSA_KF, seq)


def _dsa(bq, iq, ikw, ckv, tab, wuk_pad, wuv_pad, bsz, seq):
    tq, kn, kf = _dsa_tiles(seq)
    nq = seq // tq
    topk = min(IDX_TOPK_MAX, seq // 4)
    n_off = kf // tq + 1
    qblk = lambda w: pl.BlockSpec((tq, w), lambda b, j: (b * nq + j, 0))
    seqblk = lambda w: pl.BlockSpec((seq, w), lambda b, j: (b, 0))
    full = lambda shape: pl.BlockSpec(shape, lambda b, j: (0,) * len(shape))
    return pl.pallas_call(
        functools.partial(_dsa_kernel, tq=tq, kn=kn, kf=kf, topk=topk),
        grid=(bsz, nq),
        in_specs=[qblk(B_Q), qblk(I_Q), qblk(LANES), seqblk(LANES), seqblk(DSA_LATENT),
                  full((n_off, DSA_HEADS * tq, kn)), full((B_Q, DSA_HEADS * DSA_LATENT)),
                  full((DSA_HEADS * DSA_LATENT, B_Q))],
        out_specs=qblk(B_Q),
        out_shape=jax.ShapeDtypeStruct((bsz * seq, B_Q), BF16),
        scratch_shapes=[pltpu.VMEM((tq, seq), jnp.int32),
                        pltpu.VMEM((seq, tq), jnp.int32),
                        pltpu.VMEM((SUBLANES, tq), F32),
                        pltpu.VMEM((SUBLANES, tq), F32),
                        pltpu.VMEM((SUBLANES, tq), F32),
                        pltpu.VMEM((SUBLANES, tq), F32),
                        pltpu.VMEM((DSA_HEADS * tq, 2 * DSA_LATENT), F32),
                        pltpu.VMEM((DSA_HEADS * tq, LANES), F32),
                        pltpu.VMEM((DSA_HEADS * tq, DSA_LATENT), BF16),
                        pltpu.VMEM((IDX_HEADS, tq, kf), F32)],
        compiler_params=_cparams(("parallel", "arbitrary")),
        name="dsa",
    )(bq, iq, ikw, ikw, ckv, tab, wuk_pad, wuv_pad)


def _kv_norm_kernel(bc_ref, g_ref, out_ref):
    x = bc_ref[...]
    out_ref[...] = (x * lax.rsqrt(jnp.mean(x * x, axis=-1, keepdims=True) + LN_EPS) * g_ref[...]).astype(BF16)


def _kv_norm(bc, g):
    n = bc.shape[0]
    tm = min(4 * TOK_TILE, n)
    return pl.pallas_call(
        _kv_norm_kernel,
        grid=(n // tm,),
        in_specs=[pl.BlockSpec((tm, DSA_LATENT), lambda i: (i, 0)), pl.BlockSpec((1, DSA_LATENT), lambda i: (0, 0))],
        out_specs=pl.BlockSpec((tm, DSA_LATENT), lambda i: (i, 0)),
        out_shape=jax.ShapeDtypeStruct((n, DSA_LATENT), BF16),
        compiler_params=_cparams(("parallel",)),
        name="kv_norm",
    )(bc, g.reshape(1, DSA_LATENT))


def _merge_kernel(x_ref, ya_ref, yb_ref, ga_ref, gb_ref, wa_ref, wb_ref, wo_ref, g_ref, b_ref, out_ref):
    merged = (jax.nn.sigmoid(ga_ref[...].astype(F32)) * _dot(ya_ref[...], wa_ref[...])
              + jax.nn.sigmoid(gb_ref[...].astype(F32)) * _dot(yb_ref[...], wb_ref[...]))
    mix = _dot(merged.astype(BF16), wo_ref[...])
    out_ref[...] = _layer_norm(DN_ALPHA * x_ref[...] + mix, g_ref[...], b_ref[...])


def _merge(x2, ya, yb, ga, gb, wa, wb, wo, g, b):
    n = x2.shape[0]
    tm = min(TOK_TILE, n)
    tok = lambda w: pl.BlockSpec((tm, w), lambda i: (i, 0))
    full = lambda shape: pl.BlockSpec(shape, lambda i: (0,) * len(shape))
    return pl.pallas_call(
        _merge_kernel,
        grid=(n // tm,),
        in_specs=[tok(D_MODEL), tok(A_V), tok(B_Q), tok(D_MODEL), tok(D_MODEL),
                  full((A_V, D_MODEL)), full((B_Q, D_MODEL)), full((D_MODEL, D_MODEL)),
                  full((1, D_MODEL)), full((1, D_MODEL))],
        out_specs=tok(D_MODEL),
        out_shape=jax.ShapeDtypeStruct((n, D_MODEL), F32),
        compiler_params=_cparams(("parallel",)),
        name="merge_ln",
    )(x2, ya, yb, ga, gb, wa, wb, wo, g.reshape(1, D_MODEL), b.reshape(1, D_MODEL))


def _dense_ffn_kernel(x_ref, wg_ref, wu_ref, wd_ref, g_ref, b_ref, out_ref):
    x = x_ref[...]
    xb = x.astype(BF16)
    f = jnp.zeros(x.shape, F32)
    for c in range(D_FF // FF_CHUNK):
        cols = slice(c * FF_CHUNK, (c + 1) * FF_CHUNK)
        hdn = _silu(_dot(xb, wg_ref[:, cols])) * _dot(xb, wu_ref[:, cols])
        f = f + _dot(hdn.astype(BF16), wd_ref[cols, :])
    out_ref[...] = _layer_norm(DN_ALPHA * x + f, g_ref[...], b_ref[...])


def _dense_ffn(x2, wg, wu, wd, g, b):
    n = x2.shape[0]
    tm = min(TOK_TILE, n)
    tok = pl.BlockSpec((tm, D_MODEL), lambda i: (i, 0))
    full = lambda shape: pl.BlockSpec(shape, lambda i: (0,) * len(shape))
    return pl.pallas_call(
        _dense_ffn_kernel,
        grid=(n // tm,),
        in_specs=[tok, full((D_MODEL, D_FF)), full((D_MODEL, D_FF)), full((D_FF, D_MODEL)),
                  full((1, D_MODEL)), full((1, D_MODEL))],
        out_specs=tok,
        out_shape=jax.ShapeDtypeStruct((n, D_MODEL), F32),
        compiler_params=_cparams(("parallel",)),
        name="dense_ffn_ln",
    )(x2, wg, wu, wd, g.reshape(1, D_MODEL), b.reshape(1, D_MODEL))


def _router_kernel(x_ref, rw_ref, e_ref, g_ref):
    logits = lax.dot_general(rw_ref[...], x_ref[...], NT_DIMS, preferred_element_type=F32,
                             precision=lax.Precision.HIGHEST)
    eidx = lax.broadcasted_iota(jnp.int32, logits.shape, 0)
    m1 = jnp.max(logits, axis=0, keepdims=True)
    i1 = jnp.min(jnp.where(logits == m1, eidx, N_EXPERTS), axis=0, keepdims=True)
    rest = jnp.where(eidx == i1, -jnp.inf, logits)
    m2 = jnp.max(rest, axis=0, keepdims=True)
    i2 = jnp.min(jnp.where(rest == m2, eidx, N_EXPERTS), axis=0, keepdims=True)
    e2 = jnp.exp(m2 - m1)
    den = 1.0 + e2
    e_ref[0:1, :] = i1
    e_ref[1:2, :] = i2
    g_ref[0:1, :] = 1.0 / den
    g_ref[1:2, :] = e2 / den


def _router(x2, rw_t):
    n = x2.shape[0]
    tm = min(TOK_TILE, n)
    return pl.pallas_call(
        _router_kernel,
        grid=(n // tm,),
        in_specs=[pl.BlockSpec((tm, D_MODEL), lambda i: (i, 0)), pl.BlockSpec((N_EXPERTS, D_MODEL), lambda i: (0, 0))],
        out_specs=[pl.BlockSpec((TOP_K, tm), lambda i: (0, i)), pl.BlockSpec((TOP_K, tm), lambda i: (0, i))],
        out_shape=[jax.ShapeDtypeStruct((TOP_K, n), jnp.int32), jax.ShapeDtypeStruct((TOP_K, n), F32)],
        compiler_params=_cparams(("parallel",)),
        name="router",
    )(x2, rw_t)


def _row_copy(src, src_row, dst, dst_row, sem):
    return pltpu.make_async_copy(src.at[pl.ds(src_row, 1)], dst.at[pl.ds(dst_row, 1)], sem)


def _dispatch_kernel(d0_ref, d1_ref, x_ref, xs_init_hbm, xs_hbm, sem, *, tile):
    del xs_init_hbm

    def start(r, c):
        _row_copy(x_ref, r, xs_hbm, d0_ref[0, r], sem).start(priority=0)
        _row_copy(x_ref, r, xs_hbm, d1_ref[0, r], sem).start(priority=1)
        return c
    lax.fori_loop(0, tile, start, 0, unroll=DMA_UNROLL)

    def wait(r, c):
        _row_copy(x_ref, r, xs_hbm, 0, sem).wait()
        _row_copy(x_ref, r, xs_hbm, 0, sem).wait()
        return c
    lax.fori_loop(0, tile, wait, 0, unroll=DMA_UNROLL)


def _dispatch(x2, pos, total):
    n = x2.shape[0]
    tile = min(COMB_TILE, n)
    nb = n // tile
    smem = pl.BlockSpec((None, 1, tile), lambda i: (i, 0, 0), memory_space=pltpu.SMEM)
    return pl.pallas_call(
        functools.partial(_dispatch_kernel, tile=tile),
        grid=(nb,),
        in_specs=[smem, smem, pl.BlockSpec((tile, D_MODEL), lambda i: (i, 0)), pl.BlockSpec(memory_space=pl.ANY)],
        out_specs=pl.BlockSpec(memory_space=pl.ANY),
        out_shape=jax.ShapeDtypeStruct((total, D_MODEL), F32),
        scratch_shapes=[pltpu.SemaphoreType.DMA(())],
        input_output_aliases={3: 0},
        compiler_params=_cparams(("arbitrary",)),
        name="moe_dispatch",
    )(pos[:, 0].reshape(nb, 1, tile), pos[:, 1].reshape(nb, 1, tile), x2, jnp.zeros((total, D_MODEL), F32))


def _expert_kernel(blk_e_ref, nused_ref, xs_ref, wg_ref, wu_ref, wd_ref, y_ref, xb, acc):
    i = pl.program_id(0)
    f = pl.program_id(1)
    used = i < nused_ref[0]
    last = f == pl.num_programs(1) - 1

    @pl.when(jnp.logical_and(used, f == 0))
    def _():
        xb[...] = xs_ref[...].astype(BF16)
        acc[...] = jnp.zeros_like(acc)

    @pl.when(used)
    def _():
        x = xb[...]
        hdn = _silu(_dot(x, wg_ref[...])) * _dot(x, wu_ref[...])
        acc[...] += _dot(hdn.astype(BF16), wd_ref[...])

    @pl.when(jnp.logical_and(used, last))
    def _():
        y_ref[...] = acc[...]

    @pl.when(jnp.logical_and(jnp.logical_not(used), last))
    def _():
        y_ref[...] = jnp.zeros_like(y_ref)


def _experts(xs, blk_e, nused, wg, wu, wd):
    total = xs.shape[0]
    tile = MOE_TILE
    nblk = total // tile
    nf = D_FF_EXPERT // MOE_FT
    grid_spec = pltpu.PrefetchScalarGridSpec(
        num_scalar_prefetch=2,
        grid=(nblk, nf),
        in_specs=[pl.BlockSpec((tile, D_MODEL), lambda i, f, be, nu: (i, 0)),
                  pl.BlockSpec((None, D_MODEL, MOE_FT), lambda i, f, be, nu: (be[i], 0, f)),
                  pl.BlockSpec((None, D_MODEL, MOE_FT), lambda i, f, be, nu: (be[i], 0, f)),
                  pl.BlockSpec((None, MOE_FT, D_MODEL), lambda i, f, be, nu: (be[i], f, 0))],
        out_specs=pl.BlockSpec((tile, D_MODEL), lambda i, f, be, nu: (i, 0)),
        scratch_shapes=[pltpu.VMEM((tile, D_MODEL), BF16), pltpu.VMEM((tile, D_MODEL), F32)],
    )
    return pl.pallas_call(
        _expert_kernel,
        grid_spec=grid_spec,
        out_shape=jax.ShapeDtypeStruct((total, D_MODEL), F32),
        compiler_params=_cparams(("arbitrary", "arbitrary")),
        name="experts",
    )(blk_e, nused, xs, wg, wu, wd)


def _combine_kernel(p0_ref, p1_ref, y_hbm, x_ref, gate_ref, g_ref, b_ref, out_ref, buf0, buf1, sem, *, tile):
    def start(r, c):
        _row_copy(y_hbm, p0_ref[0, r], buf0, r, sem).start(priority=0)
        _row_copy(y_hbm, p1_ref[0, r], buf1, r, sem).start(priority=1)
        return c
    lax.fori_loop(0, tile, start, 0, unroll=DMA_UNROLL)

    def wait(r, c):
        _row_copy(y_hbm, 0, buf0, r, sem).wait()
        _row_copy(y_hbm, 0, buf1, r, sem).wait()
        return c
    lax.fori_loop(0, tile, wait, 0, unroll=DMA_UNROLL)
    f = buf0[...] * gate_ref[:, 0:1] + buf1[...] * gate_ref[:, 1:2]
    out_ref[...] = _layer_norm(DN_ALPHA * x_ref[...] + f, g_ref[...], b_ref[...])


def _combine(y, pos, gates, x2, g, b):
    n = x2.shape[0]
    tile = min(COMB_TILE, n)
    nb = n // tile
    smem = pl.BlockSpec((None, 1, tile), lambda i: (i, 0, 0), memory_space=pltpu.SMEM)
    full = pl.BlockSpec((1, D_MODEL), lambda i: (0, 0))
    return pl.pallas_call(
        functools.partial(_combine_kernel, tile=tile),
        grid=(nb,),
        in_specs=[smem, smem, pl.BlockSpec(memory_space=pl.ANY),
                  pl.BlockSpec((tile, D_MODEL), lambda i: (i, 0)),
                  pl.BlockSpec((tile, TOP_K), lambda i: (i, 0)), full, full],
        out_specs=pl.BlockSpec((tile, D_MODEL), lambda i: (i, 0)),
        out_shape=jax.ShapeDtypeStruct((n, D_MODEL), F32),
        scratch_shapes=[pltpu.VMEM((tile, D_MODEL), F32), pltpu.VMEM((tile, D_MODEL), F32),
                        pltpu.SemaphoreType.DMA(())],
        compiler_params=_cparams(("arbitrary",)),
        name="moe_combine_ln",
    )(pos[:, 0].reshape(nb, 1, tile), pos[:, 1].reshape(nb, 1, tile), y, x2, gates,
      g.reshape(1, D_MODEL), b.reshape(1, D_MODEL))


def _moe(x2, router_w, wg, wu, wd, g, b):
    n = x2.shape[0]
    top_e, gates = _router(x2, router_w.T)
    flat_e = top_e.T.reshape(-1)
    onehot = (flat_e[:, None] == jnp.arange(N_EXPERTS, dtype=jnp.int32)[None, :]).astype(jnp.int32)
    rank = jnp.sum((jnp.cumsum(onehot, axis=0) - onehot) * onehot, axis=1)
    counts = jnp.sum(onehot, axis=0)
    padded = ((counts + MOE_TILE - 1) // MOE_TILE) * MOE_TILE
    pad_end = jnp.cumsum(padded)
    pad_start = pad_end - padded
    pos = (pad_start[flat_e] + rank).astype(jnp.int32).reshape(n, TOP_K)
    total = n * TOP_K + N_EXPERTS * MOE_TILE
    nblk = total // MOE_TILE
    blk_e = jnp.minimum(jnp.searchsorted(pad_end, jnp.arange(nblk, dtype=jnp.int32) * MOE_TILE, side='right'),
                        N_EXPERTS - 1).astype(jnp.int32)
    nused = (pad_end[-1:] // MOE_TILE).astype(jnp.int32)
    xs = _dispatch(x2, pos, total)
    y = _experts(xs, blk_e, nused, wg, wu, wd)
    return _combine(y, pos, gates.T, x2, g, b)


def _pad_head_proj(w_uk, w_uv):
    H, DH, DC = DSA_HEADS, DSA_HEAD_DIM, DSA_LATENT
    eye = jnp.eye(H, dtype=w_uk.dtype)
    wuk_bd = jnp.einsum('hg,hdc->hdgc', eye, w_uk).reshape(H * DH, H * DC)
    wuv_bd = jnp.einsum('hg,hcd->hcgd', eye, w_uv).reshape(H * DC, H * DH)
    return wuk_bd.astype(BF16), wuv_bd.astype(BF16)


def kernel(x, w_in, mlstm_conv_w, mlstm_gate_bias, mlstm_norm_g, dsa_kv_norm_g, dsa_w_uk, dsa_w_uv, rel_bias,
           w_branch_a, w_branch_b, w_out, ln_g, ln_b, dense_w_gate, dense_w_up, dense_w_down, router_w,
           expert_w_gate, expert_w_up, expert_w_down):
    bsz, seq, d = x.shape
    n = bsz * seq
    x2 = x.reshape(n, d)
    tab = _near_bias_tables(rel_bias, *_dsa_tiles(seq))
    for l in range(DEPTH):
        p = dict(zip([nm for nm, _, _ in IN_PROJ_OUTS], _in_proj(x2, _regroup_in_proj(w_in[l]))))
        ya = _mlstm(p["qk"], p["av"], p["aif"], p["ao"], mlstm_conv_w[l], mlstm_gate_bias[l], mlstm_norm_g[l],
                    bsz, seq)
        ckv = _kv_norm(p["bc"], dsa_kv_norm_g[l])
        wuk_pad, wuv_pad = _pad_head_proj(dsa_w_uk[l], dsa_w_uv[l])
        yb = _dsa(p["bq"], p["iq"], p["ikw"], ckv, tab, wuk_pad, wuv_pad, bsz, seq)
        x2 = _merge(x2, ya, yb, p["ga"], p["gb"], w_branch_a[l].astype(BF16), w_branch_b[l].astype(BF16),
                    w_out[l].astype(BF16), ln_g[l, 0], ln_b[l, 0])
        j = l // 2
        if l % 2 == 0:
            x2 = _dense_ffn(x2, dense_w_gate[j].astype(BF16), dense_w_up[j].astype(BF16),
                            dense_w_down[j].astype(BF16), ln_g[l, 1], ln_b[l, 1])
        else:
            x2 = _moe(x2, router_w[j], expert_w_gate[j].astype(BF16), expert_w_up[j].astype(BF16),
                      expert_w_down[j].astype(BF16), ln_g[l, 1], ln_b[l, 1])
    return x2.reshape(bsz, seq, d)
```

```python
import functools
import math

import numpy as np
import jax
import jax.numpy as jnp
from jax import lax
from jax.experimental import pallas as pl
from jax.experimental.pallas import tpu as pltpu

D_MODEL = 1024
DEPTH = 4
MLSTM_HEADS = 4
MLSTM_QK_DIM = 64
MLSTM_V_DIM = 128
MLSTM_CONV = 4
DSA_HEADS = 8
DSA_HEAD_DIM = 64
DSA_LATENT = 128
IDX_HEADS = 8
IDX_DIM = 32
IDX_TOPK_MAX = 256
REL_BUCKETS = 32
REL_MAX_DIST = 128
D_FF = 2816
N_EXPERTS = 8
TOP_K = 2
D_FF_EXPERT = 3584
DN_ALPHA = (2 * DEPTH) ** 0.25
LN_EPS = 1e-5

A_QK = 2 * MLSTM_HEADS * MLSTM_QK_DIM
A_V = MLSTM_HEADS * MLSTM_V_DIM
A_GATE = 2 * MLSTM_HEADS
B_Q = DSA_HEADS * DSA_HEAD_DIM
I_Q = IDX_HEADS * IDX_DIM
PROJ_SIZES = (A_QK, A_V, A_V, A_GATE, B_Q, DSA_LATENT, I_Q, IDX_DIM, IDX_HEADS, D_MODEL, D_MODEL)

LANES = 128
SUBLANES = 8
VMEM_LIMIT = 56 * 1024 * 1024
TOK_TILE = 512
MLSTM_TILE = 256
DSA_TQ = 128
DSA_KN = 256
DSA_KF = 512
MOE_TILE = 512
MOE_FT = 1792
COMB_TILE = 256
DMA_UNROLL = 8
FF_CHUNK = 256
INT_MIN = -(2 ** 31)
NEG_BIG = -1e30
M_FLOOR = -1e29
LOG2E = math.log2(math.e)
F32_MIN_NORMAL = 2.0 ** -126
VALUE_PASSES = 16

BF16 = jnp.bfloat16
F32 = jnp.float32
NT_DIMS = (((1,), (1,)), ((), ()))


def _cparams(sem):
    return pltpu.CompilerParams(dimension_semantics=sem, vmem_limit_bytes=VMEM_LIMIT)


def _dot(a, b):
    return jnp.dot(a, b, preferred_element_type=F32)


def _dot_nt(a, b):
    return lax.dot_general(a, b, NT_DIMS, preferred_element_type=F32)


def _layer_norm(r, g, b):
    mu = jnp.mean(r, axis=-1, keepdims=True)
    d = r - mu
    var = jnp.mean(d * d, axis=-1, keepdims=True)
    return d * lax.rsqrt(var + LN_EPS) * g + b


def _silu(x):
    return x * jax.nn.sigmoid(x)


IN_PROJ_OUTS = (("qk", A_QK, F32), ("av", A_V, BF16), ("ao", A_V, BF16), ("bq", B_Q, BF16),
                ("iq", I_Q, BF16), ("ga", D_MODEL, BF16), ("gb", D_MODEL, BF16),
                ("aif", LANES, F32), ("bc", LANES, F32), ("ikw", LANES, F32))
IN_PROJ_COLS = sum(w for _, w, _ in IN_PROJ_OUTS)
IKW_W_OFF = IDX_DIM


def _regroup_in_proj(w):
    offs = np.concatenate([[0], np.cumsum(PROJ_SIZES)])
    a_qk, a_v, a_o, a_if, b_q, b_c, i_q, i_k, i_w, g_a, g_b = (w[:, offs[i]:offs[i + 1]] for i in range(11))
    pad = lambda a: jnp.pad(a, ((0, 0), (0, LANES - a.shape[1])))
    groups = [a_qk, a_v, a_o, b_q, i_q, g_a, g_b, pad(a_if), b_c, pad(jnp.concatenate([i_k, i_w], axis=1))]
    return jnp.concatenate(groups, axis=1).astype(BF16)


def _in_proj_kernel(x_ref, w_ref, *out_refs):
    xb = x_ref[...].astype(BF16)
    off = 0
    for (_, width, dtype), o_ref in zip(IN_PROJ_OUTS, out_refs):
        o_ref[...] = _dot(xb, w_ref[:, off:off + width]).astype(dtype)
        off += width


def _in_proj(x2, w):
    n = x2.shape[0]
    tm = min(TOK_TILE, n)
    return pl.pallas_call(
        _in_proj_kernel,
        grid=(n // tm,),
        in_specs=[pl.BlockSpec((tm, D_MODEL), lambda i: (i, 0)),
                  pl.BlockSpec((D_MODEL, IN_PROJ_COLS), lambda i: (0, 0))],
        out_specs=[pl.BlockSpec((tm, wd), lambda i: (i, 0)) for _, wd, _ in IN_PROJ_OUTS],
        out_shape=[jax.ShapeDtypeStruct((n, wd), dt) for _, wd, dt in IN_PROJ_OUTS],
        compiler_params=_cparams(("parallel",)),
        name="in_proj",
    )(x2, w)


def _log_sigmoid(x):
    return jnp.minimum(x, 0.0) - jnp.log1p(jnp.exp(-jnp.abs(x)))


def _mlstm_kernel(qk_ref, av_ref, aif_ref, ao_ref, convw_ref, gbias_ref, ng_ref, out_ref,
                  ext_ref, ct_ref, m_ref, *, tile):
    L = tile
    H, DK, DV = MLSTM_HEADS, MLSTM_QK_DIM, MLSTM_V_DIM

    @pl.when(pl.program_id(1) == 0)
    def _():
        ext_ref[0:SUBLANES, :] = jnp.zeros((SUBLANES, A_QK), F32)
        ct_ref[...] = jnp.zeros_like(ct_ref)
        m_ref[...] = jnp.zeros_like(m_ref)

    u = qk_ref[...]
    ext_ref[SUBLANES:SUBLANES + L, :] = u
    conv = jnp.zeros((L, A_QK), F32)
    for j in range(MLSTM_CONV):
        conv = conv + ext_ref[pl.ds(SUBLANES - (MLSTM_CONV - 1) + j, L), :] * convw_ref[j:j + 1, :]
    ext_ref[0:SUBLANES, :] = u[L - SUBLANES:L, :]
    qk = _silu(conv)

    g_col = aif_ref[...] + gbias_ref[...]
    g_row = g_col.T
    row = lax.broadcasted_iota(jnp.int32, (L, L), 0)
    col = lax.broadcasted_iota(jnp.int32, (L, L), 1)
    causal = row >= col
    tril = jnp.where(causal, 1.0, 0.0).astype(F32)
    triu = jnp.where(row <= col, 1.0, 0.0).astype(F32)
    b_col = jnp.dot(tril, _log_sigmoid(g_col), preferred_element_type=F32, precision=lax.Precision.HIGHEST)
    b_row = jnp.dot(_log_sigmoid(g_row), triu, preferred_element_type=F32, precision=lax.Precision.HIGHEST)

    ones_v = jnp.ones((L, DV), BF16)
    ones_t = jnp.ones((DV, L), F32)
    for h in range(H):
        bcol = b_col[:, H + h:H + h + 1]
        icol = g_col[:, h:h + 1]
        brow = b_row[H + h:H + h + 1, :]
        irow = g_row[h:h + 1, :]
        mprev = m_ref[h:h + 1, 0:1]
        dlog = jnp.where(causal, bcol - brow + irow, -jnp.inf)
        m_t = jnp.maximum(bcol + mprev, jnp.max(dlog, axis=1, keepdims=True))
        q = qk[:, h * DK:(h + 1) * DK].astype(BF16)
        k = qk[:, H * DK + h * DK:H * DK + (h + 1) * DK] * (DK ** -0.5)
        s_w = _dot_nt(q, k.astype(BF16)) * jnp.exp(dlog - m_t)
        inter = jnp.exp(bcol + mprev - m_t)
        v_h = av_ref[:, h * DV:(h + 1) * DV]
        v_ext = jnp.concatenate([v_h, ones_v], axis=1)
        ct_st = ct_ref[h]
        ne = _dot(s_w.astype(BF16), v_ext) + inter * _dot_nt(q, ct_st.astype(BF16))
        h_out = ne[:, :DV] / jnp.maximum(jnp.abs(ne[:, DV:]), jnp.exp(-m_t))
        blast = bcol[L - 1:L, :]
        glog = blast - bcol + icol
        m_new = jnp.maximum(blast + mprev, jnp.max(glog, axis=0, keepdims=True))
        decay = jnp.exp(blast + mprev - m_new)
        w_k = k * jnp.exp(glog - m_new)
        v_ext_t = jnp.concatenate([v_h.astype(F32).T, ones_t], axis=0).astype(BF16)
        ct_ref[h] = decay * ct_st + _dot(v_ext_t, w_k.astype(BF16))
        m_ref[h:h + 1, :] = jnp.broadcast_to(m_new, (1, LANES))
        mu = jnp.mean(h_out, axis=-1, keepdims=True)
        d = h_out - mu
        hn = d * lax.rsqrt(jnp.mean(d * d, axis=-1, keepdims=True) + LN_EPS)
        y = hn * ng_ref[:, h * DV:(h + 1) * DV] * jax.nn.sigmoid(ao_ref[:, h * DV:(h + 1) * DV].astype(F32))
        out_ref[:, h * DV:(h + 1) * DV] = y.astype(BF16)


def _mlstm(qk, av, aif, ao, conv_w, gate_bias, norm_g, bsz, seq):
    tile = min(MLSTM_TILE, seq)
    nc = seq // tile
    gbias = jnp.zeros((1, LANES), F32).at[0, :A_GATE].set(gate_bias.reshape(-1))
    tok = lambda w: pl.BlockSpec((tile, w), lambda b, c: (b * nc + c, 0))
    full = lambda shape: pl.BlockSpec(shape, lambda b, c: (0,) * len(shape))
    return pl.pallas_call(
        functools.partial(_mlstm_kernel, tile=tile),
        grid=(bsz, nc),
        in_specs=[tok(A_QK), tok(A_V), tok(LANES), tok(A_V),
                  full((MLSTM_CONV, A_QK)), full((1, LANES)), full((1, A_V))],
        out_specs=tok(A_V),
        out_shape=jax.ShapeDtypeStruct((bsz * seq, A_V), BF16),
        scratch_shapes=[pltpu.VMEM((tile + SUBLANES, A_QK), F32),
                        pltpu.VMEM((MLSTM_HEADS, 2 * MLSTM_V_DIM, MLSTM_QK_DIM), F32),
                        pltpu.VMEM((SUBLANES, LANES), F32)],
        compiler_params=_cparams(("parallel", "arbitrary")),
        name="mlstm",
    )(qk, av, aif, ao, conv_w, gbias, norm_g.reshape(1, A_V))


def _t5_bucket_np(dist):
    dist = np.maximum(dist, 0)
    exact = REL_BUCKETS // 2
    log_ratio = (np.log(np.maximum(dist, 1).astype(np.float32) / np.float32(exact))
                 / np.float32(math.log(REL_MAX_DIST / exact))).astype(np.float32)
    large = np.minimum(exact + (log_ratio * np.float32(REL_BUCKETS - exact)).astype(np.int32), REL_BUCKETS - 1)
    return np.where(dist < exact, dist, large)


def _near_bias_tables(rel_bias, tq, kn, kf):
    n_off = kf // tq + 1
    assert np.all(_t5_bucket_np(np.arange(tq + 1, 4 * REL_MAX_DIST)) == REL_BUCKETS - 1)
    d_max = (n_off - 1) * tq + tq - 1
    dist = np.arange(d_max, -kn, -1)
    by_dist = (rel_bias[_t5_bucket_np(dist)] - rel_bias[REL_BUCKETS - 1]) * LOG2E
    by_dist = jnp.where((dist >= 0)[:, None], by_dist, 0.0).T.astype(F32)
    rows = [[by_dist[:, d_max - (o * tq + t):d_max - (o * tq + t) + kn] for t in range(tq)] for o in range(n_off)]
    tab = jnp.stack([jnp.stack(r, axis=1) for r in rows], axis=0)
    return tab.reshape(n_off, DSA_HEADS * tq, kn)


def _dsa_kernel(bq_ref, iq_ref, ikwq_ref, ikw_ref, ckv_ref, tab_ref, wuk_ref, wuv_ref, out_ref,
                key_ref, keyt_ref, smax_ref, smin_ref, npos_ref, nnon_ref, acc_ref, m_ref, ql_ref, wb_ref,
                *, tq, kn, kf, topk):
    H = DSA_HEADS
    j = pl.program_id(1)
    t0 = j * tq
    n_score = (t0 + tq - 1) // kf + 1
    n_far = jnp.maximum(t0 - tq, 0) // kf
    near_lo = n_far * (kf // kn)
    near_hi = (t0 + tq - 1) // kn + 1

    ql_all = (_dot(bq_ref[...], wuk_ref[...]) * (DSA_HEAD_DIM ** -0.5 * LOG2E)).astype(BF16)
    for h in range(H):
        ql_ref[h * tq:(h + 1) * tq, :] = ql_all[:, h * DSA_LATENT:(h + 1) * DSA_LATENT]
    w_idx = ikwq_ref[:, IKW_W_OFF:IKW_W_OFF + IDX_HEADS] * ((IDX_HEADS * IDX_DIM) ** -0.5)
    for h in range(IDX_HEADS):
        wb_ref[h] = jnp.broadcast_to(w_idx[:, h:h + 1], (tq, LANES))
    iq = iq_ref[...]
    qis = [iq[:, h * IDX_DIM:(h + 1) * IDX_DIM] for h in range(IDX_HEADS)]
    tpos = t0 + lax.broadcasted_iota(jnp.int32, (tq, kf), 0)
    lpos = lax.broadcasted_iota(jnp.int32, (tq, kf), 1)

    ncol_f = kf // LANES
    tile_f = lambda a: jnp.concatenate([a] * ncol_f, axis=1)

    flip = lambda a: a ^ ((a >> 31) & 0x7FFFFFFF)
    to_key = lambda v: flip(lax.bitcast_convert_type(v, jnp.int32))
    to_val = lambda k: lax.bitcast_convert_type(flip(k), F32)
    fold = lambda a: a.reshape(kf // SUBLANES, SUBLANES, tq)

    smax_ref[...] = jnp.full(smax_ref.shape, -jnp.inf, F32)
    smin_ref[...] = jnp.full(smin_ref.shape, jnp.inf, F32)
    npos_ref[...] = jnp.zeros_like(npos_ref)
    nnon_ref[...] = jnp.zeros_like(nnon_ref)

    def score_chunk(c, masked):
        s0 = pl.multiple_of(c * kf, kf)
        kidx = ikw_ref[pl.ds(s0, kf), :][:, 0:IDX_DIM].astype(BF16)
        parts = [jnp.zeros((tq, LANES), F32)] * ncol_f
        for h in range(IDX_HEADS):
            z = jnp.maximum(_dot_nt(qis[h], kidx), 0.0)
            w_h = wb_ref[h]
            parts = [p + w_h * z[:, g * LANES:(g + 1) * LANES] for g, p in enumerate(parts)]
        sc = jnp.concatenate(parts, axis=1)
        sc = jnp.where(jnp.abs(sc) < F32_MIN_NORMAL, 0.0, sc)
        key = to_key(sc)
        sc_t = sc.T
        sc_t_lo = sc_t
        if masked:
            causal = s0 + lpos <= tpos
            key = jnp.where(causal, key, INT_MIN)
            sc_t = jnp.where(causal, sc, -jnp.inf).T
            sc_t_lo = jnp.where(causal, sc, jnp.inf).T
        key_ref[:, pl.ds(s0, kf)] = key
        keyt_ref[pl.ds(s0, kf), :] = to_key(sc_t)
        smax_ref[...] = jnp.maximum(smax_ref[...], jnp.max(fold(sc_t), axis=0))
        smin_ref[...] = jnp.minimum(smin_ref[...], jnp.min(fold(sc_t_lo), axis=0))
        npos_ref[...] = npos_ref[...] + jnp.sum(fold(jnp.where(sc_t > 0.0, 1.0, 0.0)), axis=0)
        nnon_ref[...] = nnon_ref[...] + jnp.sum(fold(jnp.where(sc_t >= 0.0, 1.0, 0.0)), axis=0)

    def score_run(first, count):
        for u in range(count):
            score_chunk(first + u, False)

    def score_quad(i, carry):
        score_run(4 * i, 4)
        return carry

    n_plain = n_score - 1
    lax.fori_loop(0, n_plain // 4, score_quad, 0)

    @pl.when(n_plain % 4 >= 2)
    def _():
        score_run((n_plain // 4) * 4, 2)

    @pl.when(n_plain % 2 == 1)
    def _():
        score_run(n_plain - 1, 1)

    score_chunk(n_score - 1, True)

    per_query = lambda a, op: jnp.broadcast_to(op(a, axis=0, keepdims=True), (SUBLANES, tq))
    n_acc = 8

    def count_ge(cand):
        def body(c, acc):
            s0 = pl.multiple_of(c * kf, kf)
            ind = jnp.where(fold(keyt_ref[pl.ds(s0, kf), :]) >= cand[None], 1.0, 0.0)
            return acc + jnp.sum(ind.reshape(kf // SUBLANES // n_acc, n_acc, SUBLANES, tq), axis=0)
        acc = lax.fori_loop(0, n_score, body, jnp.zeros((n_acc, SUBLANES, tq), F32))
        return per_query(jnp.sum(acc, axis=0), jnp.sum)

    n_valid = (t0 + 1 + lax.broadcasted_iota(jnp.int32, (SUBLANES, tq), 1)).astype(F32)
    enough = n_valid >= topk
    lo0 = to_key(per_query(smin_ref[...], jnp.min))
    hi0 = to_key(per_query(smax_ref[...], jnp.max)) + 1
    n_pos = per_query(npos_ref[...], jnp.sum)
    n_non = per_query(nnon_ref[...], jnp.sum)
    pos_thr = n_pos >= topk
    zero_thr = jnp.logical_and(n_non >= topk, jnp.logical_not(pos_thr))
    cnt0 = jnp.where(pos_thr, n_pos, jnp.where(zero_thr, n_non, n_valid))
    lo0 = jnp.where(pos_thr, 1, jnp.where(zero_thr, 0, lo0))
    hi0 = jnp.where(pos_thr, jnp.where(n_pos == topk, 2, hi0), jnp.where(zero_thr, 1, 0))

    def probe(mid, lo, hi, cnt_lo):
        cnt = count_ge(mid)
        ge = cnt >= topk
        return (jnp.where(ge, mid, lo), jnp.where(cnt == topk, mid + 1, jnp.where(ge, hi, mid)),
                jnp.where(ge, cnt, cnt_lo))

    def by_value(i, state):
        lo, hi, cnt_lo = state
        mid = to_key(0.5 * to_val(lo) + 0.5 * to_val(hi))
        return probe(jnp.where(jnp.logical_and(mid > lo, mid < hi), mid, lo), lo, hi, cnt_lo)

    state = lax.fori_loop(0, VALUE_PASSES, by_value, (lo0, hi0, cnt0))

    def any_open(lo, hi):
        return jnp.max(jnp.where(jnp.logical_and(enough, hi > lo + 1), 1.0, 0.0))

    def by_key(state):
        lo, hi, cnt_lo, _ = state
        lo, hi, cnt_lo = probe((lo >> 1) + (hi >> 1) + (lo & hi & 1), lo, hi, cnt_lo)
        return lo, hi, cnt_lo, any_open(lo, hi)

    lo, _, cnt_lo, _ = lax.while_loop(lambda s: s[3] > 0.0, by_key, state + (any_open(state[0], state[1]),))
    thr_q = jnp.where(enough, lo, INT_MIN + 1)
    thr = jnp.broadcast_to(thr_q[0:1, :], (LANES, tq)).T

    def count_key(pred):
        def body(c, acc):
            s0 = pl.multiple_of(c * kf, kf)
            ind = jnp.where(pred(key_ref[:, pl.ds(s0, kf)]), 1.0, 0.0)
            for g in range(ncol_f):
                acc = acc + ind[:, g * LANES:(g + 1) * LANES]
            return acc
        acc = lax.fori_loop(0, n_score, body, jnp.zeros((tq, LANES), F32))
        return jnp.sum(acc, axis=1, keepdims=True)

    thr_f = tile_f(thr)
    surplus = jnp.where(enough, cnt_lo - topk, 0.0)

    @pl.when(jnp.max(surplus) > 0.0)
    def _():
        allowed = topk - count_key(lambda k: k > thr_f)
        r = lax.broadcasted_iota(jnp.int32, (kf, kf), 0)
        cidx = lax.broadcasted_iota(jnp.int32, (kf, kf), 1)
        upper = jnp.where(r <= cidx, 1.0, 0.0).astype(BF16)

        def body(c, seen):
            s0 = pl.multiple_of(c * kf, kf)
            key = key_ref[:, pl.ds(s0, kf)]
            tie = key == thr_f
            tie_f = jnp.where(tie, 1.0, 0.0)
            rank = _dot(tie_f.astype(BF16), upper) + seen
            key_ref[:, pl.ds(s0, kf)] = jnp.where(jnp.logical_and(tie, rank > allowed), thr_f - 1, key)
            return seen + jnp.sum(tie_f, axis=1, keepdims=True)
        lax.fori_loop(0, n_score, body, jnp.zeros((tq, 1), F32))

    m_ref[...] = jnp.full(m_ref.shape, M_FLOOR, F32)
    acc_ref[...] = jnp.zeros_like(acc_ref)

    def attend(s0, width, table_idx):
        ncol = width // LANES
        kv = ckv_ref[pl.ds(s0, width), :]
        kv_ext = jnp.concatenate([kv, jnp.ones((width, DSA_LATENT), BF16)], axis=1)
        thr_w = jnp.concatenate([thr] * ncol, axis=1)
        madd = jnp.where(key_ref[:, pl.ds(s0, width)] >= thr_w, 0.0, NEG_BIG)
        for h in range(H):
            rows = slice(h * tq, (h + 1) * tq)
            x = _dot_nt(ql_ref[rows, :], kv) + madd
            if table_idx is not None:
                x = x + tab_ref[table_idx, rows, :]
            m_old = m_ref[rows, :]
            m_new = jnp.maximum(m_old, jnp.max(x, axis=1, keepdims=True))
            alpha = jnp.exp2(m_old - m_new)
            p = jnp.exp2(x - jnp.concatenate([m_new] * ncol, axis=1))
            acc_ref[rows, :] = jnp.concatenate([alpha, alpha], axis=1) * acc_ref[rows, :] + _dot(p.astype(BF16), kv_ext)
            m_ref[rows, :] = m_new

    def far_run(first, count):
        for u in range(count):
            attend(pl.multiple_of((first + u) * kf, kf), kf, None)

    def far_quad(i, carry):
        far_run(4 * i, 4)
        return carry

    def near_chunk(c, carry):
        s0 = pl.multiple_of(c * kn, kn)
        attend(s0, kn, (t0 - s0) // tq)
        return carry

    lax.fori_loop(0, n_far // 4, far_quad, 0)

    @pl.when(n_far % 4 >= 2)
    def _():
        far_run((n_far // 4) * 4, 2)

    @pl.when(n_far % 2 == 1)
    def _():
        far_run(n_far - 1, 1)

    lax.fori_loop(near_lo, near_hi, near_chunk, 0)

    outs = []
    for h in range(H):
        a = acc_ref[h * tq:(h + 1) * tq, :]
        outs.append((a[:, :DSA_LATENT] / a[:, DSA_LATENT:]).astype(BF16))
    out_ref[...] = _dot(jnp.concatenate(outs, axis=1), wuv_ref[...]).astype(BF16)


def _dsa_tiles(seq):
    tq = min(DSA_TQ, seq)
    return tq, min(DSA_KN, seq), min(DSA_KF, seq)


def _dsa(bq, iq, ikw, ckv, tab, wuk_pad, wuv_pad, bsz, seq):
    tq, kn, kf = _dsa_tiles(seq)
    nq = seq // tq
    topk = min(IDX_TOPK_MAX, seq // 4)
    n_off = kf // tq + 1
    qblk = lambda w: pl.BlockSpec((tq, w), lambda b, j: (b * nq + j, 0))
    seqblk = lambda w: pl.BlockSpec((seq, w), lambda b, j: (b, 0))
    full = lambda shape: pl.BlockSpec(shape, lambda b, j: (0,) * len(shape))
    return pl.pallas_call(
        functools.partial(_dsa_kernel, tq=tq, kn=kn, kf=kf, topk=topk),
        grid=(bsz, nq),
        in_specs=[qblk(B_Q), qblk(I_Q), qblk(LANES), seqblk(LANES), seqblk(DSA_LATENT),
                  full((n_off, DSA_HEADS * tq, kn)), full((B_Q, DSA_HEADS * DSA_LATENT)),
                  full((DSA_HEADS * DSA_LATENT, B_Q))],
        out_specs=qblk(B_Q),
        out_shape=jax.ShapeDtypeStruct((bsz * seq, B_Q), BF16),
        scratch_shapes=[pltpu.VMEM((tq, seq), jnp.int32),
                        pltpu.VMEM((seq, tq), jnp.int32),
                        pltpu.VMEM((SUBLANES, tq), F32),
                        pltpu.VMEM((SUBLANES, tq), F32),
                        pltpu.VMEM((SUBLANES, tq), F32),
                        pltpu.VMEM((SUBLANES, tq), F32),
                        pltpu.VMEM((DSA_HEADS * tq, 2 * DSA_LATENT), F32),
                        pltpu.VMEM((DSA_HEADS * tq, LANES), F32),
                        pltpu.VMEM((DSA_HEADS * tq, DSA_LATENT), BF16),
                        pltpu.VMEM((IDX_HEADS, tq, LANES), F32)],
        compiler_params=_cparams(("parallel", "arbitrary")),
        name="dsa",
    )(bq, iq, ikw, ikw, ckv, tab, wuk_pad, wuv_pad)


def _kv_norm_kernel(bc_ref, g_ref, out_ref):
    x = bc_ref[...]
    out_ref[...] = (x * lax.rsqrt(jnp.mean(x * x, axis=-1, keepdims=True) + LN_EPS) * g_ref[...]).astype(BF16)


def _kv_norm(bc, g):
    n = bc.shape[0]
    tm = min(4 * TOK_TILE, n)
    return pl.pallas_call(
        _kv_norm_kernel,
        grid=(n // tm,),
        in_specs=[pl.BlockSpec((tm, DSA_LATENT), lambda i: (i, 0)), pl.BlockSpec((1, DSA_LATENT), lambda i: (0, 0))],
        out_specs=pl.BlockSpec((tm, DSA_LATENT), lambda i: (i, 0)),
        out_shape=jax.ShapeDtypeStruct((n, DSA_LATENT), BF16),
        compiler_params=_cparams(("parallel",)),
        name="kv_norm",
    )(bc, g.reshape(1, DSA_LATENT))


def _merge_kernel(x_ref, ya_ref, yb_ref, ga_ref, gb_ref, wa_ref, wb_ref, wo_ref, g_ref, b_ref, out_ref):
    merged = (jax.nn.sigmoid(ga_ref[...].astype(F32)) * _dot(ya_ref[...], wa_ref[...])
              + jax.nn.sigmoid(gb_ref[...].astype(F32)) * _dot(yb_ref[...], wb_ref[...]))
    mix = _dot(merged.astype(BF16), wo_ref[...])
    out_ref[...] = _layer_norm(DN_ALPHA * x_ref[...] + mix, g_ref[...], b_ref[...])


def _merge(x2, ya, yb, ga, gb, wa, wb, wo, g, b):
    n = x2.shape[0]
    tm = min(TOK_TILE, n)
    tok = lambda w: pl.BlockSpec((tm, w), lambda i: (i, 0))
    full = lambda shape: pl.BlockSpec(shape, lambda i: (0,) * len(shape))
    return pl.pallas_call(
        _merge_kernel,
        grid=(n // tm,),
        in_specs=[tok(D_MODEL), tok(A_V), tok(B_Q), tok(D_MODEL), tok(D_MODEL),
                  full((A_V, D_MODEL)), full((B_Q, D_MODEL)), full((D_MODEL, D_MODEL)),
                  full((1, D_MODEL)), full((1, D_MODEL))],
        out_specs=tok(D_MODEL),
        out_shape=jax.ShapeDtypeStruct((n, D_MODEL), F32),
        compiler_params=_cparams(("parallel",)),
        name="merge_ln",
    )(x2, ya, yb, ga, gb, wa, wb, wo, g.reshape(1, D_MODEL), b.reshape(1, D_MODEL))


def _dense_ffn_kernel(x_ref, wg_ref, wu_ref, wd_ref, g_ref, b_ref, out_ref):
    x = x_ref[...]
    xb = x.astype(BF16)
    f = jnp.zeros(x.shape, F32)
    for c in range(D_FF // FF_CHUNK):
        cols = slice(c * FF_CHUNK, (c + 1) * FF_CHUNK)
        hdn = _silu(_dot(xb, wg_ref[:, cols])) * _dot(xb, wu_ref[:, cols])
        f = f + _dot(hdn.astype(BF16), wd_ref[cols, :])
    out_ref[...] = _layer_norm(DN_ALPHA * x + f, g_ref[...], b_ref[...])


def _dense_ffn(x2, wg, wu, wd, g, b):
    n = x2.shape[0]
    tm = min(TOK_TILE, n)
    tok = pl.BlockSpec((tm, D_MODEL), lambda i: (i, 0))
    full = lambda shape: pl.BlockSpec(shape, lambda i: (0,) * len(shape))
    return pl.pallas_call(
        _dense_ffn_kernel,
        grid=(n // tm,),
        in_specs=[tok, full((D_MODEL, D_FF)), full((D_MODEL, D_FF)), full((D_FF, D_MODEL)),
                  full((1, D_MODEL)), full((1, D_MODEL))],
        out_specs=tok,
        out_shape=jax.ShapeDtypeStruct((n, D_MODEL), F32),
        compiler_params=_cparams(("parallel",)),
        name="dense_ffn_ln",
    )(x2, wg, wu, wd, g.reshape(1, D_MODEL), b.reshape(1, D_MODEL))


def _router_kernel(x_ref, rw_ref, e_ref, g_ref):
    logits = lax.dot_general(rw_ref[...], x_ref[...], NT_DIMS, preferred_element_type=F32,
                             precision=lax.Precision.HIGHEST)
    eidx = lax.broadcasted_iota(jnp.int32, logits.shape, 0)
    m1 = jnp.max(logits, axis=0, keepdims=True)
    i1 = jnp.min(jnp.where(logits == m1, eidx, N_EXPERTS), axis=0, keepdims=True)
    rest = jnp.where(eidx == i1, -jnp.inf, logits)
    m2 = jnp.max(rest, axis=0, keepdims=True)
    i2 = jnp.min(jnp.where(rest == m2, eidx, N_EXPERTS), axis=0, keepdims=True)
    e2 = jnp.exp(m2 - m1)
    den = 1.0 + e2
    e_ref[0:1, :] = i1
    e_ref[1:2, :] = i2
    g_ref[0:1, :] = 1.0 / den
    g_ref[1:2, :] = e2 / den


def _router(x2, rw_t):
    n = x2.shape[0]
    tm = min(TOK_TILE, n)
    return pl.pallas_call(
        _router_kernel,
        grid=(n // tm,),
        in_specs=[pl.BlockSpec((tm, D_MODEL), lambda i: (i, 0)), pl.BlockSpec((N_EXPERTS, D_MODEL), lambda i: (0, 0))],
        out_specs=[pl.BlockSpec((TOP_K, tm), lambda i: (0, i)), pl.BlockSpec((TOP_K, tm), lambda i: (0, i))],
        out_shape=[jax.ShapeDtypeStruct((TOP_K, n), jnp.int32), jax.ShapeDtypeStruct((TOP_K, n), F32)],
        compiler_params=_cparams(("parallel",)),
        name="router",
    )(x2, rw_t)


def _row_copy(src, src_row, dst, dst_row, sem):
    return pltpu.make_async_copy(src.at[pl.ds(src_row, 1)], dst.at[pl.ds(dst_row, 1)], sem)


def _dispatch_kernel(d0_ref, d1_ref, x_ref, xs_init_hbm, xs_hbm, sem, *, tile):
    del xs_init_hbm

    def start(r, c):
        _row_copy(x_ref, r, xs_hbm, d0_ref[0, r], sem).start(priority=0)
        _row_copy(x_ref, r, xs_hbm, d1_ref[0, r], sem).start(priority=1)
        return c
    lax.fori_loop(0, tile, start, 0, unroll=DMA_UNROLL)

    def wait(r, c):
        _row_copy(x_ref, r, xs_hbm, 0, sem).wait()
        _row_copy(x_ref, r, xs_hbm, 0, sem).wait()
        return c
    lax.fori_loop(0, tile, wait, 0, unroll=DMA_UNROLL)


def _dispatch(x2, pos, total):
    n = x2.shape[0]
    tile = min(COMB_TILE, n)
    nb = n // tile
    smem = pl.BlockSpec((None, 1, tile), lambda i: (i, 0, 0), memory_space=pltpu.SMEM)
    return pl.pallas_call(
        functools.partial(_dispatch_kernel, tile=tile),
        grid=(nb,),
        in_specs=[smem, smem, pl.BlockSpec((tile, D_MODEL), lambda i: (i, 0)), pl.BlockSpec(memory_space=pl.ANY)],
        out_specs=pl.BlockSpec(memory_space=pl.ANY),
        out_shape=jax.ShapeDtypeStruct((total, D_MODEL), F32),
        scratch_shapes=[pltpu.SemaphoreType.DMA(())],
        input_output_aliases={3: 0},
        compiler_params=_cparams(("arbitrary",)),
        name="moe_dispatch",
    )(pos[:, 0].reshape(nb, 1, tile), pos[:, 1].reshape(nb, 1, tile), x2, jnp.zeros((total, D_MODEL), F32))


def _expert_kernel(blk_e_ref, nused_ref, xs_ref, wg_ref, wu_ref, wd_ref, y_ref, xb, acc):
    i = pl.program_id(0)
    f = pl.program_id(1)
    used = i < nused_ref[0]
    last = f == pl.num_programs(1) - 1

    @pl.when(jnp.logical_and(used, f == 0))
    def _():
        xb[...] = xs_ref[...].astype(BF16)
        acc[...] = jnp.zeros_like(acc)

    @pl.when(used)
    def _():
        x = xb[...]
        hdn = _silu(_dot(x, wg_ref[...])) * _dot(x, wu_ref[...])
        acc[...] += _dot(hdn.astype(BF16), wd_ref[...])

    @pl.when(jnp.logical_and(used, last))
    def _():
        y_ref[...] = acc[...]

    @pl.when(jnp.logical_and(jnp.logical_not(used), last))
    def _():
        y_ref[...] = jnp.zeros_like(y_ref)


def _experts(xs, blk_e, nused, wg, wu, wd):
    total = xs.shape[0]
    tile = MOE_TILE
    nblk = total // tile
    nf = D_FF_EXPERT // MOE_FT
    grid_spec = pltpu.PrefetchScalarGridSpec(
        num_scalar_prefetch=2,
        grid=(nblk, nf),
        in_specs=[pl.BlockSpec((tile, D_MODEL), lambda i, f, be, nu: (i, 0)),
                  pl.BlockSpec((None, D_MODEL, MOE_FT), lambda i, f, be, nu: (be[i], 0, f)),
                  pl.BlockSpec((None, D_MODEL, MOE_FT), lambda i, f, be, nu: (be[i], 0, f)),
                  pl.BlockSpec((None, MOE_FT, D_MODEL), lambda i, f, be, nu: (be[i], f, 0))],
        out_specs=pl.BlockSpec((tile, D_MODEL), lambda i, f, be, nu: (i, 0)),
        scratch_shapes=[pltpu.VMEM((tile, D_MODEL), BF16), pltpu.VMEM((tile, D_MODEL), F32)],
    )
    return pl.pallas_call(
        _expert_kernel,
        grid_spec=grid_spec,
        out_shape=jax.ShapeDtypeStruct((total, D_MODEL), F32),
        compiler_params=_cparams(("arbitrary", "arbitrary")),
        name="experts",
    )(blk_e, nused, xs, wg, wu, wd)


def _combine_kernel(p0_ref, p1_ref, y_hbm, x_ref, gate_ref, g_ref, b_ref, out_ref, buf0, buf1, sem, *, tile):
    def start(r, c):
        _row_copy(y_hbm, p0_ref[0, r], buf0, r, sem).start(priority=0)
        _row_copy(y_hbm, p1_ref[0, r], buf1, r, sem).start(priority=1)
        return c
    lax.fori_loop(0, tile, start, 0, unroll=DMA_UNROLL)

    def wait(r, c):
        _row_copy(y_hbm, 0, buf0, r, sem).wait()
        _row_copy(y_hbm, 0, buf1, r, sem).wait()
        return c
    lax.fori_loop(0, tile, wait, 0, unroll=DMA_UNROLL)
    f = buf0[...] * gate_ref[:, 0:1] + buf1[...] * gate_ref[:, 1:2]
    out_ref[...] = _layer_norm(DN_ALPHA * x_ref[...] + f, g_ref[...], b_ref[...])


def _combine(y, pos, gates, x2, g, b):
    n = x2.shape[0]
    tile = min(COMB_TILE, n)
    nb = n // tile
    smem = pl.BlockSpec((None, 1, tile), lambda i: (i, 0, 0), memory_space=pltpu.SMEM)
    full = pl.BlockSpec((1, D_MODEL), lambda i: (0, 0))
    return pl.pallas_call(
        functools.partial(_combine_kernel, tile=tile),
        grid=(nb,),
        in_specs=[smem, smem, pl.BlockSpec(memory_space=pl.ANY),
                  pl.BlockSpec((tile, D_MODEL), lambda i: (i, 0)),
                  pl.BlockSpec((tile, TOP_K), lambda i: (i, 0)), full, full],
        out_specs=pl.BlockSpec((tile, D_MODEL), lambda i: (i, 0)),
        out_shape=jax.ShapeDtypeStruct((n, D_MODEL), F32),
        scratch_shapes=[pltpu.VMEM((tile, D_MODEL), F32), pltpu.VMEM((tile, D_MODEL), F32),
                        pltpu.SemaphoreType.DMA(())],
        compiler_params=_cparams(("arbitrary",)),
        name="moe_combine_ln",
    )(pos[:, 0].reshape(nb, 1, tile), pos[:, 1].reshape(nb, 1, tile), y, x2, gates,
      g.reshape(1, D_MODEL), b.reshape(1, D_MODEL))


def _moe(x2, router_w, wg, wu, wd, g, b):
    n = x2.shape[0]
    top_e, gates = _router(x2, router_w.T)
    flat_e = top_e.T.reshape(-1)
    onehot = (flat_e[:, None] == jnp.arange(N_EXPERTS, dtype=jnp.int32)[None, :]).astype(jnp.int32)
    rank = jnp.sum((jnp.cumsum(onehot, axis=0) - onehot) * onehot, axis=1)
    counts = jnp.sum(onehot, axis=0)
    padded = ((counts + MOE_TILE - 1) // MOE_TILE) * MOE_TILE
    pad_end = jnp.cumsum(padded)
    pad_start = pad_end - padded
    pos = (pad_start[flat_e] + rank).astype(jnp.int32).reshape(n, TOP_K)
    total = n * TOP_K + N_EXPERTS * MOE_TILE
    nblk = total // MOE_TILE
    blk_e = jnp.minimum(jnp.searchsorted(pad_end, jnp.arange(nblk, dtype=jnp.int32) * MOE_TILE, side='right'),
                        N_EXPERTS - 1).astype(jnp.int32)
    nused = (pad_end[-1:] // MOE_TILE).astype(jnp.int32)
    xs = _dispatch(x2, pos, total)
    y = _experts(xs, blk_e, nused, wg, wu, wd)
    return _combine(y, pos, gates.T, x2, g, b)


def _pad_head_proj(w_uk, w_uv):
    H, DH, DC = DSA_HEADS, DSA_HEAD_DIM, DSA_LATENT
    eye = jnp.eye(H, dtype=w_uk.dtype)
    wuk_bd = jnp.einsum('hg,hdc->hdgc', eye, w_uk).reshape(H * DH, H * DC)
    wuv_bd = jnp.einsum('hg,hcd->hcgd', eye, w_uv).reshape(H * DC, H * DH)
    return wuk_bd.astype(BF16), wuv_bd.astype(BF16)


def kernel(x, w_in, mlstm_conv_w, mlstm_gate_bias, mlstm_norm_g, dsa_kv_norm_g, dsa_w_uk, dsa_w_uv, rel_bias,
           w_branch_a, w_branch_b, w_out, ln_g, ln_b, dense_w_gate, dense_w_up, dense_w_down, router_w,
           expert_w_gate, expert_w_up, expert_w_down):
    bsz, seq, d = x.shape
    n = bsz * seq
    x2 = x.reshape(n, d)
    tab = _near_bias_tables(rel_bias, *_dsa_tiles(seq))
    for l in range(DEPTH):
        p = dict(zip([nm for nm, _, _ in IN_PROJ_OUTS], _in_proj(x2, _regroup_in_proj(w_in[l]))))
        ya = _mlstm(p["qk"], p["av"], p["aif"], p["ao"], mlstm_conv_w[l], mlstm_gate_bias[l], mlstm_norm_g[l],
                    bsz, seq)
        ckv = _kv_norm(p["bc"], dsa_kv_norm_g[l])
        wuk_pad, wuv_pad = _pad_head_proj(dsa_w_uk[l], dsa_w_uv[l])
        yb = _dsa(p["bq"], p["iq"], p["ikw"], ckv, tab, wuk_pad, wuv_pad, bsz, seq)
        x2 = _merge(x2, ya, yb, p["ga"], p["gb"], w_branch_a[l].astype(BF16), w_branch_b[l].astype(BF16),
                    w_out[l].astype(BF16), ln_g[l, 0], ln_b[l, 0])
        j = l // 2
        if l % 2 == 0:
            x2 = _dense_ffn(x2, dense_w_gate[j].astype(BF16), dense_w_up[j].astype(BF16),
                            dense_w_down[j].astype(BF16), ln_g[l, 1], ln_b[l, 1])
        else:
            x2 = _moe(x2, router_w[j], expert_w_gate[j].astype(BF16), expert_w_up[j].astype(BF16),
                      expert_w_down[j].astype(BF16), ln_g[l, 1], ln_b[l, 1])
    return x2.reshape(bsz, seq, d)
```

```python
import functools
import math

import numpy as np
import jax
import jax.numpy as jnp
from jax import lax
from jax.experimental import pallas as pl
from jax.experimental.pallas import tpu as pltpu

D_MODEL = 1024
DEPTH = 4
MLSTM_HEADS = 4
MLSTM_QK_DIM = 64
MLSTM_V_DIM = 128
MLSTM_CONV = 4
DSA_HEADS = 8
DSA_HEAD_DIM = 64
DSA_LATENT = 128
IDX_HEADS = 8
IDX_DIM = 32
IDX_TOPK_MAX = 256
REL_BUCKETS = 32
REL_MAX_DIST = 128
D_FF = 2816
N_EXPERTS = 8
TOP_K = 2
D_FF_EXPERT = 3584
DN_ALPHA = (2 * DEPTH) ** 0.25
LN_EPS = 1e-5

A_QK = 2 * MLSTM_HEADS * MLSTM_QK_DIM
A_V = MLSTM_HEADS * MLSTM_V_DIM
A_GATE = 2 * MLSTM_HEADS
B_Q = DSA_HEADS * DSA_HEAD_DIM
I_Q = IDX_HEADS * IDX_DIM
PROJ_SIZES = (A_QK, A_V, A_V, A_GATE, B_Q, DSA_LATENT, I_Q, IDX_DIM, IDX_HEADS, D_MODEL, D_MODEL)

LANES = 128
SUBLANES = 8
VMEM_LIMIT = 56 * 1024 * 1024
TOK_TILE = 1024
MLSTM_TILE = 256
MLSTM_CHUNKS_PER_STEP = 2
DSA_TQ = 128
DSA_KN = 256
DSA_KF = 512
MOE_TILE = 512
MOE_FT = 1792
COMB_TILE = 256
DMA_UNROLL = 8
FF_CHUNK = 256
INT_MIN = -(2 ** 31)
NEG_BIG = -1e30
M_FLOOR = -1e29
LOG2E = math.log2(math.e)
F32_MIN_NORMAL = 2.0 ** -126
VALUE_PASSES = 16

BF16 = jnp.bfloat16
F32 = jnp.float32
NT_DIMS = (((1,), (1,)), ((), ()))


def _cparams(sem):
    return pltpu.CompilerParams(dimension_semantics=sem, vmem_limit_bytes=VMEM_LIMIT)


def _resident(shape):
    return pl.BlockSpec(shape, lambda *_: (0,) * len(shape), pipeline_mode=pl.Buffered(1))


def _dot(a, b):
    return jnp.dot(a, b, preferred_element_type=F32)


def _dot_nt(a, b):
    return lax.dot_general(a, b, NT_DIMS, preferred_element_type=F32)


def _layer_norm(r, g, b):
    mu = jnp.mean(r, axis=-1, keepdims=True)
    d = r - mu
    var = jnp.mean(d * d, axis=-1, keepdims=True)
    return d * lax.rsqrt(var + LN_EPS) * g + b


def _silu(x):
    return x * jax.nn.sigmoid(x)


IN_PROJ_OUTS = (("qk", A_QK, F32), ("av", A_V, BF16), ("ao", A_V, BF16), ("bq", B_Q, BF16),
                ("iq", I_Q, BF16), ("ga", D_MODEL, BF16), ("gb", D_MODEL, BF16),
                ("aif", LANES, F32), ("bc", LANES, F32), ("ikw", LANES, F32))
IN_PROJ_COLS = sum(w for _, w, _ in IN_PROJ_OUTS)
IKW_W_OFF = IDX_DIM


def _regroup_in_proj(w):
    offs = np.concatenate([[0], np.cumsum(PROJ_SIZES)])
    a_qk, a_v, a_o, a_if, b_q, b_c, i_q, i_k, i_w, g_a, g_b = (w[:, offs[i]:offs[i + 1]] for i in range(11))
    pad = lambda a: jnp.pad(a, ((0, 0), (0, LANES - a.shape[1])))
    groups = [a_qk, a_v, a_o, b_q, i_q, g_a, g_b, pad(a_if), b_c, pad(jnp.concatenate([i_k, i_w], axis=1))]
    return jnp.concatenate(groups, axis=1).astype(BF16)


def _in_proj_kernel(x_ref, w_ref, *out_refs):
    xb = x_ref[...].astype(BF16)
    off = 0
    for (_, width, dtype), o_ref in zip(IN_PROJ_OUTS, out_refs):
        o_ref[...] = _dot(xb, w_ref[:, off:off + width]).astype(dtype)
        off += width


def _in_proj(x2, w):
    n = x2.shape[0]
    tm = min(TOK_TILE, n)
    return pl.pallas_call(
        _in_proj_kernel,
        grid=(n // tm,),
        in_specs=[pl.BlockSpec((tm, D_MODEL), lambda i: (i, 0)),
                  _resident((D_MODEL, IN_PROJ_COLS))],
        out_specs=[pl.BlockSpec((tm, wd), lambda i: (i, 0)) for _, wd, _ in IN_PROJ_OUTS],
        out_shape=[jax.ShapeDtypeStruct((n, wd), dt) for _, wd, dt in IN_PROJ_OUTS],
        compiler_params=_cparams(("parallel",)),
        name="in_proj",
    )(x2, w)


def _log_sigmoid(x):
    return jnp.minimum(x, 0.0) - jnp.log1p(jnp.exp(-jnp.abs(x)))


def _mlstm_kernel(qk_ref, av_ref, aif_ref, ao_ref, convw_ref, gbias_ref, ng_ref, out_ref,
                  ext_ref, ct_ref, m_ref, *, tile, chunks):
    @pl.when(pl.program_id(1) == 0)
    def _():
        ext_ref[0:SUBLANES, :] = jnp.zeros((SUBLANES, A_QK), F32)
        ct_ref[...] = jnp.zeros_like(ct_ref)
        m_ref[...] = jnp.zeros_like(m_ref)

    for c in range(chunks):
        rows = pl.ds(c * tile, tile)
        _mlstm_chunk(qk_ref.at[rows], av_ref.at[rows], aif_ref.at[rows], ao_ref.at[rows], convw_ref, gbias_ref,
                     ng_ref, out_ref.at[rows], ext_ref, ct_ref, m_ref, tile=tile)


def _mlstm_chunk(qk_ref, av_ref, aif_ref, ao_ref, convw_ref, gbias_ref, ng_ref, out_ref,
                 ext_ref, ct_ref, m_ref, *, tile):
    L = tile
    H, DK, DV = MLSTM_HEADS, MLSTM_QK_DIM, MLSTM_V_DIM

    u = qk_ref[...]
    ext_ref[SUBLANES:SUBLANES + L, :] = u
    conv = jnp.zeros((L, A_QK), F32)
    for j in range(MLSTM_CONV):
        conv = conv + ext_ref[pl.ds(SUBLANES - (MLSTM_CONV - 1) + j, L), :] * convw_ref[j:j + 1, :]
    ext_ref[0:SUBLANES, :] = u[L - SUBLANES:L, :]
    qk = _silu(conv)

    g_col = aif_ref[...] + gbias_ref[...]
    g_row = g_col.T
    row = lax.broadcasted_iota(jnp.int32, (L, L), 0)
    col = lax.broadcasted_iota(jnp.int32, (L, L), 1)
    causal = row >= col
    tril = jnp.where(causal, 1.0, 0.0).astype(F32)
    triu = jnp.where(row <= col, 1.0, 0.0).astype(F32)
    b_col = jnp.dot(tril, _log_sigmoid(g_col), preferred_element_type=F32, precision=lax.Precision.HIGHEST)
    b_row = jnp.dot(_log_sigmoid(g_row), triu, preferred_element_type=F32, precision=lax.Precision.HIGHEST)

    ones_v = jnp.ones((L, DV), BF16)
    ones_t = jnp.ones((DV, L), F32)
    for h in range(H):
        bcol = b_col[:, H + h:H + h + 1]
        icol = g_col[:, h:h + 1]
        brow = b_row[H + h:H + h + 1, :]
        irow = g_row[h:h + 1, :]
        mprev = m_ref[h:h + 1, 0:1]
        dlog = jnp.where(causal, bcol - brow + irow, -jnp.inf)
        m_t = jnp.maximum(bcol + mprev, jnp.max(dlog, axis=1, keepdims=True))
        q = qk[:, h * DK:(h + 1) * DK].astype(BF16)
        k = qk[:, H * DK + h * DK:H * DK + (h + 1) * DK] * (DK ** -0.5)
        s_w = _dot_nt(q, k.astype(BF16)) * jnp.exp(dlog - m_t)
        inter = jnp.exp(bcol + mprev - m_t)
        v_h = av_ref[:, h * DV:(h + 1) * DV]
        v_ext = jnp.concatenate([v_h, ones_v], axis=1)
        ct_st = ct_ref[h]
        ne = _dot(s_w.astype(BF16), v_ext) + inter * _dot_nt(q, ct_st.astype(BF16))
        h_out = ne[:, :DV] / jnp.maximum(jnp.abs(ne[:, DV:]), jnp.exp(-m_t))
        blast = bcol[L - 1:L, :]
        glog = blast - bcol + icol
        m_new = jnp.maximum(blast + mprev, jnp.max(glog, axis=0, keepdims=True))
        decay = jnp.exp(blast + mprev - m_new)
        w_k = k * jnp.exp(glog - m_new)
        v_ext_t = jnp.concatenate([v_h.astype(F32).T, ones_t], axis=0).astype(BF16)
        ct_ref[h] = decay * ct_st + _dot(v_ext_t, w_k.astype(BF16))
        m_ref[h:h + 1, :] = jnp.broadcast_to(m_new, (1, LANES))
        mu = jnp.mean(h_out, axis=-1, keepdims=True)
        d = h_out - mu
        hn = d * lax.rsqrt(jnp.mean(d * d, axis=-1, keepdims=True) + LN_EPS)
        y = hn * ng_ref[:, h * DV:(h + 1) * DV] * jax.nn.sigmoid(ao_ref[:, h * DV:(h + 1) * DV].astype(F32))
        out_ref[:, h * DV:(h + 1) * DV] = y.astype(BF16)


def _mlstm(qk, av, aif, ao, conv_w, gate_bias, norm_g, bsz, seq):
    tile = min(MLSTM_TILE, seq)
    chunks = min(MLSTM_CHUNKS_PER_STEP, seq // tile)
    nc = seq // (tile * chunks)
    gbias = jnp.zeros((1, LANES), F32).at[0, :A_GATE].set(gate_bias.reshape(-1))
    tok = lambda w: pl.BlockSpec((tile * chunks, w), lambda b, c: (b * nc + c, 0))
    full = lambda shape: pl.BlockSpec(shape, lambda b, c: (0,) * len(shape))
    return pl.pallas_call(
        functools.partial(_mlstm_kernel, tile=tile, chunks=chunks),
        grid=(bsz, nc),
        in_specs=[tok(A_QK), tok(A_V), tok(LANES), tok(A_V),
                  full((MLSTM_CONV, A_QK)), full((1, LANES)), full((1, A_V))],
        out_specs=tok(A_V),
        out_shape=jax.ShapeDtypeStruct((bsz * seq, A_V), BF16),
        scratch_shapes=[pltpu.VMEM((tile + SUBLANES, A_QK), F32),
                        pltpu.VMEM((MLSTM_HEADS, 2 * MLSTM_V_DIM, MLSTM_QK_DIM), F32),
                        pltpu.VMEM((SUBLANES, LANES), F32)],
        compiler_params=_cparams(("parallel", "arbitrary")),
        name="mlstm",
    )(qk, av, aif, ao, conv_w, gbias, norm_g.reshape(1, A_V))


def _t5_bucket_np(dist):
    dist = np.maximum(dist, 0)
    exact = REL_BUCKETS // 2
    log_ratio = (np.log(np.maximum(dist, 1).astype(np.float32) / np.float32(exact))
                 / np.float32(math.log(REL_MAX_DIST / exact))).astype(np.float32)
    large = np.minimum(exact + (log_ratio * np.float32(REL_BUCKETS - exact)).astype(np.int32), REL_BUCKETS - 1)
    return np.where(dist < exact, dist, large)


def _near_bias_tables(rel_bias, tq, kn, kf):
    n_off = kf // tq + 1
    assert np.all(_t5_bucket_np(np.arange(tq + 1, 4 * REL_MAX_DIST)) == REL_BUCKETS - 1)
    d_max = (n_off - 1) * tq + tq - 1
    dist = np.arange(d_max, -kn, -1)
    by_dist = (rel_bias[_t5_bucket_np(dist)] - rel_bias[REL_BUCKETS - 1]) * LOG2E
    by_dist = jnp.where((dist >= 0)[:, None], by_dist, 0.0).T.astype(F32)
    rows = [[by_dist[:, d_max - (o * tq + t):d_max - (o * tq + t) + kn] for t in range(tq)] for o in range(n_off)]
    tab = jnp.stack([jnp.stack(r, axis=1) for r in rows], axis=0)
    return tab.reshape(n_off, DSA_HEADS * tq, kn)


def _dsa_kernel(bq_ref, iq_ref, ikwq_ref, ikw_ref, ckv_ref, tab_ref, wuk_ref, wuv_ref, out_ref,
                key_ref, keyt_ref, smax_ref, smin_ref, npos_ref, nnon_ref, acc_ref, m_ref, ql_ref, wb_ref,
                *, tq, kn, kf, topk):
    H = DSA_HEADS
    j = pl.program_id(1)
    t0 = j * tq
    n_score = (t0 + tq - 1) // kf + 1
    n_far = jnp.maximum(t0 - tq, 0) // kf
    near_lo = n_far * (kf // kn)
    near_hi = (t0 + tq - 1) // kn + 1

    ql_all = (_dot(bq_ref[...], wuk_ref[...]) * (DSA_HEAD_DIM ** -0.5 * LOG2E)).astype(BF16)
    for h in range(H):
        ql_ref[h * tq:(h + 1) * tq, :] = ql_all[:, h * DSA_LATENT:(h + 1) * DSA_LATENT]
    w_idx = ikwq_ref[:, IKW_W_OFF:IKW_W_OFF + IDX_HEADS] * ((IDX_HEADS * IDX_DIM) ** -0.5)
    for h in range(IDX_HEADS):
        wb_ref[h] = jnp.broadcast_to(w_idx[:, h:h + 1], (tq, LANES))
    iq = iq_ref[...]
    qis = [iq[:, h * IDX_DIM:(h + 1) * IDX_DIM] for h in range(IDX_HEADS)]
    tpos = t0 + lax.broadcasted_iota(jnp.int32, (tq, kf), 0)
    lpos = lax.broadcasted_iota(jnp.int32, (tq, kf), 1)

    ncol_f = kf // LANES
    tile_f = lambda a: jnp.concatenate([a] * ncol_f, axis=1)

    flip = lambda a: a ^ ((a >> 31) & 0x7FFFFFFF)
    to_key = lambda v: flip(lax.bitcast_convert_type(v, jnp.int32))
    to_val = lambda k: lax.bitcast_convert_type(flip(k), F32)
    fold = lambda a: a.reshape(kf // SUBLANES, SUBLANES, tq)

    smax_ref[...] = jnp.full(smax_ref.shape, -jnp.inf, F32)
    smin_ref[...] = jnp.full(smin_ref.shape, jnp.inf, F32)
    npos_ref[...] = jnp.zeros_like(npos_ref)
    nnon_ref[...] = jnp.zeros_like(nnon_ref)

    def score_chunk(c, masked):
        s0 = pl.multiple_of(c * kf, kf)
        kidx = ikw_ref[pl.ds(s0, kf), :][:, 0:IDX_DIM].astype(BF16)
        parts = [jnp.zeros((tq, LANES), F32)] * ncol_f
        for h in range(IDX_HEADS):
            z = jnp.maximum(_dot_nt(qis[h], kidx), 0.0)
            w_h = wb_ref[h]
            parts = [p + w_h * z[:, g * LANES:(g + 1) * LANES] for g, p in enumerate(parts)]
        sc = jnp.concatenate(parts, axis=1)
        sc = jnp.where(jnp.abs(sc) < F32_MIN_NORMAL, 0.0, sc)
        key = to_key(sc)
        sc_t = sc.T
        sc_t_lo = sc_t
        if masked:
            causal = s0 + lpos <= tpos
            key = jnp.where(causal, key, INT_MIN)
            sc_t = jnp.where(causal, sc, -jnp.inf).T
            sc_t_lo = jnp.where(causal, sc, jnp.inf).T
        key_ref[:, pl.ds(s0, kf)] = key
        keyt_ref[pl.ds(s0, kf), :] = to_key(sc_t)
        smax_ref[...] = jnp.maximum(smax_ref[...], jnp.max(fold(sc_t), axis=0))
        smin_ref[...] = jnp.minimum(smin_ref[...], jnp.min(fold(sc_t_lo), axis=0))
        npos_ref[...] = npos_ref[...] + jnp.sum(fold(jnp.where(sc_t > 0.0, 1.0, 0.0)), axis=0)
        nnon_ref[...] = nnon_ref[...] + jnp.sum(fold(jnp.where(sc_t >= 0.0, 1.0, 0.0)), axis=0)

    def score_run(first, count):
        for u in range(count):
            score_chunk(first + u, False)

    def score_quad(i, carry):
        score_run(4 * i, 4)
        return carry

    n_plain = n_score - 1
    lax.fori_loop(0, n_plain // 4, score_quad, 0)

    @pl.when(n_plain % 4 >= 2)
    def _():
        score_run((n_plain // 4) * 4, 2)

    @pl.when(n_plain % 2 == 1)
    def _():
        score_run(n_plain - 1, 1)

    score_chunk(n_score - 1, True)

    per_query = lambda a, op: jnp.broadcast_to(op(a, axis=0, keepdims=True), (SUBLANES, tq))
    n_acc = 8

    def count_ge(cand):
        def body(c, acc):
            s0 = pl.multiple_of(c * kf, kf)
            ind = jnp.where(fold(keyt_ref[pl.ds(s0, kf), :]) >= cand[None], 1.0, 0.0)
            return acc + jnp.sum(ind.reshape(kf // SUBLANES // n_acc, n_acc, SUBLANES, tq), axis=0)
        acc = lax.fori_loop(0, n_score, body, jnp.zeros((n_acc, SUBLANES, tq), F32))
        return per_query(jnp.sum(acc, axis=0), jnp.sum)

    n_valid = (t0 + 1 + lax.broadcasted_iota(jnp.int32, (SUBLANES, tq), 1)).astype(F32)
    enough = n_valid >= topk
    lo0 = to_key(per_query(smin_ref[...], jnp.min))
    hi0 = to_key(per_query(smax_ref[...], jnp.max)) + 1
    n_pos = per_query(npos_ref[...], jnp.sum)
    n_non = per_query(nnon_ref[...], jnp.sum)
    pos_thr = n_pos >= topk
    zero_thr = jnp.logical_and(n_non >= topk, jnp.logical_not(pos_thr))
    cnt0 = jnp.where(pos_thr, n_pos, jnp.where(zero_thr, n_non, n_valid))
    lo0 = jnp.where(pos_thr, 1, jnp.where(zero_thr, 0, lo0))
    hi0 = jnp.where(pos_thr, jnp.where(n_pos == topk, 2, hi0), jnp.where(zero_thr, 1, 0))

    def probe(mid, lo, hi, cnt_lo):
        cnt = count_ge(mid)
        ge = cnt >= topk
        return (jnp.where(ge, mid, lo), jnp.where(cnt == topk, mid + 1, jnp.where(ge, hi, mid)),
                jnp.where(ge, cnt, cnt_lo))

    def by_value(i, state):
        lo, hi, cnt_lo = state
        mid = to_key(0.5 * to_val(lo) + 0.5 * to_val(hi))
        return probe(jnp.where(jnp.logical_and(mid > lo, mid < hi), mid, lo), lo, hi, cnt_lo)

    state = lax.fori_loop(0, VALUE_PASSES, by_value, (lo0, hi0, cnt0))

    def any_open(lo, hi):
        return jnp.max(jnp.where(jnp.logical_and(enough, hi > lo + 1), 1.0, 0.0))

    def by_key(state):
        lo, hi, cnt_lo, _ = state
        lo, hi, cnt_lo = probe((lo >> 1) + (hi >> 1) + (lo & hi & 1), lo, hi, cnt_lo)
        return lo, hi, cnt_lo, any_open(lo, hi)

    lo, _, cnt_lo, _ = lax.while_loop(lambda s: s[3] > 0.0, by_key, state + (any_open(state[0], state[1]),))
    thr_q = jnp.where(enough, lo, INT_MIN + 1)
    thr = jnp.broadcast_to(thr_q[0:1, :], (LANES, tq)).T

    def count_key(pred):
        def body(c, acc):
            s0 = pl.multiple_of(c * kf, kf)
            ind = jnp.where(pred(key_ref[:, pl.ds(s0, kf)]), 1.0, 0.0)
            for g in range(ncol_f):
                acc = acc + ind[:, g * LANES:(g + 1) * LANES]
            return acc
        acc = lax.fori_loop(0, n_score, body, jnp.zeros((tq, LANES), F32))
        return jnp.sum(acc, axis=1, keepdims=True)

    thr_f = tile_f(thr)
    surplus = jnp.where(enough, cnt_lo - topk, 0.0)

    @pl.when(jnp.max(surplus) > 0.0)
    def _():
        allowed = topk - count_key(lambda k: k > thr_f)
        r = lax.broadcasted_iota(jnp.int32, (kf, kf), 0)
        cidx = lax.broadcasted_iota(jnp.int32, (kf, kf), 1)
        upper = jnp.where(r <= cidx, 1.0, 0.0).astype(BF16)

        def body(c, seen):
            s0 = pl.multiple_of(c * kf, kf)
            key = key_ref[:, pl.ds(s0, kf)]
            tie = key == thr_f
            tie_f = jnp.where(tie, 1.0, 0.0)
            rank = _dot(tie_f.astype(BF16), upper) + seen
            key_ref[:, pl.ds(s0, kf)] = jnp.where(jnp.logical_and(tie, rank > allowed), thr_f - 1, key)
            return seen + jnp.sum(tie_f, axis=1, keepdims=True)
        lax.fori_loop(0, n_score, body, jnp.zeros((tq, 1), F32))

    m_ref[...] = jnp.full(m_ref.shape, M_FLOOR, F32)
    acc_ref[...] = jnp.zeros_like(acc_ref)

    def attend(s0, width, table_idx):
        ncol = width // LANES
        kv = ckv_ref[pl.ds(s0, width), :]
        kv_ext = jnp.concatenate([kv, jnp.ones((width, DSA_LATENT), BF16)], axis=1)
        thr_w = jnp.concatenate([thr] * ncol, axis=1)
        madd = jnp.where(key_ref[:, pl.ds(s0, width)] >= thr_w, 0.0, NEG_BIG)
        for h in range(H):
            rows = slice(h * tq, (h + 1) * tq)
            x = _dot_nt(ql_ref[rows, :], kv) + madd
            if table_idx is not None:
                x = x + tab_ref[table_idx, rows, :]
            m_old = m_ref[rows, :]
            m_new = jnp.maximum(m_old, jnp.max(x, axis=1, keepdims=True))
            alpha = jnp.exp2(m_old - m_new)
            p = jnp.exp2(x - jnp.concatenate([m_new] * ncol, axis=1))
            acc_ref[rows, :] = jnp.concatenate([alpha, alpha], axis=1) * acc_ref[rows, :] + _dot(p.astype(BF16), kv_ext)
            m_ref[rows, :] = m_new

    def far_run(first, count):
        for u in range(count):
            attend(pl.multiple_of((first + u) * kf, kf), kf, None)

    def far_quad(i, carry):
        far_run(4 * i, 4)
        return carry

    def near_chunk(c, carry):
        s0 = pl.multiple_of(c * kn, kn)
        attend(s0, kn, (t0 - s0) // tq)
        return carry

    lax.fori_loop(0, n_far // 4, far_quad, 0)

    @pl.when(n_far % 4 >= 2)
    def _():
        far_run((n_far // 4) * 4, 2)

    @pl.when(n_far % 2 == 1)
    def _():
        far_run(n_far - 1, 1)

    lax.fori_loop(near_lo, near_hi, near_chunk, 0)

    outs = []
    for h in range(H):
        a = acc_ref[h * tq:(h + 1) * tq, :]
        outs.append((a[:, :DSA_LATENT] / a[:, DSA_LATENT:]).astype(BF16))
    out_ref[...] = _dot(jnp.concatenate(outs, axis=1), wuv_ref[...]).astype(BF16)


def _dsa_tiles(seq):
    tq = min(DSA_TQ, seq)
    return tq, min(DSA_KN, seq), min(DSA_KF, seq)


def _dsa(bq, iq, ikw, ckv, tab, wuk_pad, wuv_pad, bsz, seq):
    tq, kn, kf = _dsa_tiles(seq)
    nq = seq // tq
    topk = min(IDX_TOPK_MAX, seq // 4)
    n_off = kf // tq + 1
    qblk = lambda w: pl.BlockSpec((tq, w), lambda b, j: (b * nq + j, 0))
    seqblk = lambda w: pl.BlockSpec((seq, w), lambda b, j: (b, 0))
    full = lambda shape: pl.BlockSpec(shape, lambda b, j: (0,) * len(shape))
    return pl.pallas_call(
        functools.partial(_dsa_kernel, tq=tq, kn=kn, kf=kf, topk=topk),
        grid=(bsz, nq),
        in_specs=[qblk(B_Q), qblk(I_Q), qblk(LANES), seqblk(LANES), seqblk(DSA_LATENT),
                  full((n_off, DSA_HEADS * tq, kn)), full((B_Q, DSA_HEADS * DSA_LATENT)),
                  full((DSA_HEADS * DSA_LATENT, B_Q))],
        out_specs=qblk(B_Q),
        out_shape=jax.ShapeDtypeStruct((bsz * seq, B_Q), BF16),
        scratch_shapes=[pltpu.VMEM((tq, seq), jnp.int32),
                        pltpu.VMEM((seq, tq), jnp.int32),
                        pltpu.VMEM((SUBLANES, tq), F32),
                        pltpu.VMEM((SUBLANES, tq), F32),
                        pltpu.VMEM((SUBLANES, tq), F32),
                        pltpu.VMEM((SUBLANES, tq), F32),
                        pltpu.VMEM((DSA_HEADS * tq, 2 * DSA_LATENT), F32),
                        pltpu.VMEM((DSA_HEADS * tq, LANES), F32),
                        pltpu.VMEM((DSA_HEADS * tq, DSA_LATENT), BF16),
                        pltpu.VMEM((IDX_HEADS, tq, LANES), F32)],
        compiler_params=_cparams(("parallel", "arbitrary")),
        name="dsa",
    )(bq, iq, ikw, ikw, ckv, tab, wuk_pad, wuv_pad)


def _kv_norm_kernel(bc_ref, g_ref, out_ref):
    x = bc_ref[...]
    out_ref[...] = (x * lax.rsqrt(jnp.mean(x * x, axis=-1, keepdims=True) + LN_EPS) * g_ref[...]).astype(BF16)


def _kv_norm(bc, g):
    n = bc.shape[0]
    tm = min(2 * TOK_TILE, n)
    return pl.pallas_call(
        _kv_norm_kernel,
        grid=(n // tm,),
        in_specs=[pl.BlockSpec((tm, DSA_LATENT), lambda i: (i, 0)), pl.BlockSpec((1, DSA_LATENT), lambda i: (0, 0))],
        out_specs=pl.BlockSpec((tm, DSA_LATENT), lambda i: (i, 0)),
        out_shape=jax.ShapeDtypeStruct((n, DSA_LATENT), BF16),
        compiler_params=_cparams(("parallel",)),
        name="kv_norm",
    )(bc, g.reshape(1, DSA_LATENT))


def _merge_kernel(x_ref, ya_ref, yb_ref, ga_ref, gb_ref, wa_ref, wb_ref, wo_ref, g_ref, b_ref, out_ref):
    merged = (jax.nn.sigmoid(ga_ref[...].astype(F32)) * _dot(ya_ref[...], wa_ref[...])
              + jax.nn.sigmoid(gb_ref[...].astype(F32)) * _dot(yb_ref[...], wb_ref[...]))
    mix = _dot(merged.astype(BF16), wo_ref[...])
    out_ref[...] = _layer_norm(DN_ALPHA * x_ref[...] + mix, g_ref[...], b_ref[...])


def _merge(x2, ya, yb, ga, gb, wa, wb, wo, g, b):
    n = x2.shape[0]
    tm = min(TOK_TILE, n)
    tok = lambda w: pl.BlockSpec((tm, w), lambda i: (i, 0))
    full = _resident
    return pl.pallas_call(
        _merge_kernel,
        grid=(n // tm,),
        in_specs=[tok(D_MODEL), tok(A_V), tok(B_Q), tok(D_MODEL), tok(D_MODEL),
                  full((A_V, D_MODEL)), full((B_Q, D_MODEL)), full((D_MODEL, D_MODEL)),
                  full((1, D_MODEL)), full((1, D_MODEL))],
        out_specs=tok(D_MODEL),
        out_shape=jax.ShapeDtypeStruct((n, D_MODEL), F32),
        compiler_params=_cparams(("parallel",)),
        name="merge_ln",
    )(x2, ya, yb, ga, gb, wa, wb, wo, g.reshape(1, D_MODEL), b.reshape(1, D_MODEL))


def _dense_ffn_kernel(x_ref, wg_ref, wu_ref, wd_ref, g_ref, b_ref, out_ref):
    x = x_ref[...]
    xb = x.astype(BF16)
    f = jnp.zeros(x.shape, F32)
    for c in range(D_FF // FF_CHUNK):
        cols = slice(c * FF_CHUNK, (c + 1) * FF_CHUNK)
        hdn = _silu(_dot(xb, wg_ref[:, cols])) * _dot(xb, wu_ref[:, cols])
        f = f + _dot(hdn.astype(BF16), wd_ref[cols, :])
    out_ref[...] = _layer_norm(DN_ALPHA * x + f, g_ref[...], b_ref[...])


def _dense_ffn(x2, wg, wu, wd, g, b):
    n = x2.shape[0]
    tm = min(TOK_TILE, n)
    tok = pl.BlockSpec((tm, D_MODEL), lambda i: (i, 0))
    full = _resident
    return pl.pallas_call(
        _dense_ffn_kernel,
        grid=(n // tm,),
        in_specs=[tok, full((D_MODEL, D_FF)), full((D_MODEL, D_FF)), full((D_FF, D_MODEL)),
                  full((1, D_MODEL)), full((1, D_MODEL))],
        out_specs=tok,
        out_shape=jax.ShapeDtypeStruct((n, D_MODEL), F32),
        compiler_params=_cparams(("parallel",)),
        name="dense_ffn_ln",
    )(x2, wg, wu, wd, g.reshape(1, D_MODEL), b.reshape(1, D_MODEL))


def _router_kernel(x_ref, rw_ref, e_ref, g_ref):
    logits = lax.dot_general(rw_ref[...], x_ref[...], NT_DIMS, preferred_element_type=F32,
                             precision=lax.Precision.HIGHEST)
    eidx = lax.broadcasted_iota(jnp.int32, logits.shape, 0)
    m1 = jnp.max(logits, axis=0, keepdims=True)
    i1 = jnp.min(jnp.where(logits == m1, eidx, N_EXPERTS), axis=0, keepdims=True)
    rest = jnp.where(eidx == i1, -jnp.inf, logits)
    m2 = jnp.max(rest, axis=0, keepdims=True)
    i2 = jnp.min(jnp.where(rest == m2, eidx, N_EXPERTS), axis=0, keepdims=True)
    e2 = jnp.exp(m2 - m1)
    den = 1.0 + e2
    e_ref[0:1, :] = i1
    e_ref[1:2, :] = i2
    g_ref[0:1, :] = 1.0 / den
    g_ref[1:2, :] = e2 / den


def _router(x2, rw_t):
    n = x2.shape[0]
    tm = min(TOK_TILE, n)
    return pl.pallas_call(
        _router_kernel,
        grid=(n // tm,),
        in_specs=[pl.BlockSpec((tm, D_MODEL), lambda i: (i, 0)), pl.BlockSpec((N_EXPERTS, D_MODEL), lambda i: (0, 0))],
        out_specs=[pl.BlockSpec((TOP_K, tm), lambda i: (0, i)), pl.BlockSpec((TOP_K, tm), lambda i: (0, i))],
        out_shape=[jax.ShapeDtypeStruct((TOP_K, n), jnp.int32), jax.ShapeDtypeStruct((TOP_K, n), F32)],
        compiler_params=_cparams(("parallel",)),
        name="router",
    )(x2, rw_t)


def _row_copy(src, src_row, dst, dst_row, sem):
    return pltpu.make_async_copy(src.at[pl.ds(src_row, 1)], dst.at[pl.ds(dst_row, 1)], sem)


def _dispatch_kernel(d0_ref, d1_ref, x_ref, xs_init_hbm, xs_hbm, sem, *, tile):
    del xs_init_hbm

    def start(r, c):
        _row_copy(x_ref, r, xs_hbm, d0_ref[0, r], sem).start(priority=0)
        _row_copy(x_ref, r, xs_hbm, d1_ref[0, r], sem).start(priority=1)
        return c
    lax.fori_loop(0, tile, start, 0, unroll=DMA_UNROLL)

    def wait(r, c):
        _row_copy(x_ref, r, xs_hbm, 0, sem).wait()
        _row_copy(x_ref, r, xs_hbm, 0, sem).wait()
        return c
    lax.fori_loop(0, tile, wait, 0, unroll=DMA_UNROLL)


def _dispatch(x2, pos, total):
    n = x2.shape[0]
    tile = min(COMB_TILE, n)
    nb = n // tile
    smem = pl.BlockSpec((None, 1, tile), lambda i: (i, 0, 0), memory_space=pltpu.SMEM)
    return pl.pallas_call(
        functools.partial(_dispatch_kernel, tile=tile),
        grid=(nb,),
        in_specs=[smem, smem, pl.BlockSpec((tile, D_MODEL), lambda i: (i, 0)), pl.BlockSpec(memory_space=pl.ANY)],
        out_specs=pl.BlockSpec(memory_space=pl.ANY),
        out_shape=jax.ShapeDtypeStruct((total, D_MODEL), F32),
        scratch_shapes=[pltpu.SemaphoreType.DMA(())],
        input_output_aliases={3: 0},
        compiler_params=_cparams(("arbitrary",)),
        name="moe_dispatch",
    )(pos[:, 0].reshape(nb, 1, tile), pos[:, 1].reshape(nb, 1, tile), x2, jnp.zeros((total, D_MODEL), F32))


def _expert_kernel(blk_e_ref, nused_ref, xs_ref, wg_ref, wu_ref, wd_ref, y_ref, xb, acc):
    i = pl.program_id(0)
    f = pl.program_id(1)
    used = i < nused_ref[0]
    last = f == pl.num_programs(1) - 1

    @pl.when(jnp.logical_and(used, f == 0))
    def _():
        xb[...] = xs_ref[...].astype(BF16)
        acc[...] = jnp.zeros_like(acc)

    @pl.when(used)
    def _():
        x = xb[...]
        hdn = _silu(_dot(x, wg_ref[...])) * _dot(x, wu_ref[...])
        acc[...] += _dot(hdn.astype(BF16), wd_ref[...])

    @pl.when(jnp.logical_and(used, last))
    def _():
        y_ref[...] = acc[...]

    @pl.when(jnp.logical_and(jnp.logical_not(used), last))
    def _():
        y_ref[...] = jnp.zeros_like(y_ref)


def _experts(xs, blk_e, nused, wg, wu, wd):
    total = xs.shape[0]
    tile = MOE_TILE
    nblk = total // tile
    nf = D_FF_EXPERT // MOE_FT
    grid_spec = pltpu.PrefetchScalarGridSpec(
        num_scalar_prefetch=2,
        grid=(nblk, nf),
        in_specs=[pl.BlockSpec((tile, D_MODEL), lambda i, f, be, nu: (i, 0)),
                  pl.BlockSpec((None, D_MODEL, MOE_FT), lambda i, f, be, nu: (be[i], 0, f)),
                  pl.BlockSpec((None, D_MODEL, MOE_FT), lambda i, f, be, nu: (be[i], 0, f)),
                  pl.BlockSpec((None, MOE_FT, D_MODEL), lambda i, f, be, nu: (be[i], f, 0))],
        out_specs=pl.BlockSpec((tile, D_MODEL), lambda i, f, be, nu: (i, 0)),
        scratch_shapes=[pltpu.VMEM((tile, D_MODEL), BF16), pltpu.VMEM((tile, D_MODEL), F32)],
    )
    return pl.pallas_call(
        _expert_kernel,
        grid_spec=grid_spec,
        out_shape=jax.ShapeDtypeStruct((total, D_MODEL), F32),
        compiler_params=_cparams(("arbitrary", "arbitrary")),
        name="experts",
    )(blk_e, nused, xs, wg, wu, wd)


def _combine_kernel(p0_ref, p1_ref, y_hbm, x_ref, gate_ref, g_ref, b_ref, out_ref, buf0, buf1, sem, *, tile):
    def start(r, c):
        _row_copy(y_hbm, p0_ref[0, r], buf0, r, sem).start(priority=0)
        _row_copy(y_hbm, p1_ref[0, r], buf1, r, sem).start(priority=1)
        return c
    lax.fori_loop(0, tile, start, 0, unroll=DMA_UNROLL)

    def wait(r, c):
        _row_copy(y_hbm, 0, buf0, r, sem).wait()
        _row_copy(y_hbm, 0, buf1, r, sem).wait()
        return c
    lax.fori_loop(0, tile, wait, 0, unroll=DMA_UNROLL)
    f = buf0[...] * gate_ref[:, 0:1] + buf1[...] * gate_ref[:, 1:2]
    out_ref[...] = _layer_norm(DN_ALPHA * x_ref[...] + f, g_ref[...], b_ref[...])


def _combine(y, pos, gates, x2, g, b):
    n = x2.shape[0]
    tile = min(COMB_TILE, n)
    nb = n // tile
    smem = pl.BlockSpec((None, 1, tile), lambda i: (i, 0, 0), memory_space=pltpu.SMEM)
    full = pl.BlockSpec((1, D_MODEL), lambda i: (0, 0))
    return pl.pallas_call(
        functools.partial(_combine_kernel, tile=tile),
        grid=(nb,),
        in_specs=[smem, smem, pl.BlockSpec(memory_space=pl.ANY),
                  pl.BlockSpec((tile, D_MODEL), lambda i: (i, 0)),
                  pl.BlockSpec((tile, TOP_K), lambda i: (i, 0)), full, full],
        out_specs=pl.BlockSpec((tile, D_MODEL), lambda i: (i, 0)),
        out_shape=jax.ShapeDtypeStruct((n, D_MODEL), F32),
        scratch_shapes=[pltpu.VMEM((tile, D_MODEL), F32), pltpu.VMEM((tile, D_MODEL), F32),
                        pltpu.SemaphoreType.DMA(())],
        compiler_params=_cparams(("arbitrary",)),
        name="moe_combine_ln",
    )(pos[:, 0].reshape(nb, 1, tile), pos[:, 1].reshape(nb, 1, tile), y, x2, gates,
      g.reshape(1, D_MODEL), b.reshape(1, D_MODEL))


def _moe(x2, router_w, wg, wu, wd, g, b):
    n = x2.shape[0]
    top_e, gates = _router(x2, router_w.T)
    flat_e = top_e.T.reshape(-1)
    onehot = (flat_e[:, None] == jnp.arange(N_EXPERTS, dtype=jnp.int32)[None, :]).astype(jnp.int32)
    rank = jnp.sum((jnp.cumsum(onehot, axis=0) - onehot) * onehot, axis=1)
    counts = jnp.sum(onehot, axis=0)
    padded = ((counts + MOE_TILE - 1) // MOE_TILE) * MOE_TILE
    pad_end = jnp.cumsum(padded)
    pad_start = pad_end - padded
    pos = (pad_start[flat_e] + rank).astype(jnp.int32).reshape(n, TOP_K)
    total = n * TOP_K + N_EXPERTS * MOE_TILE
    nblk = total // MOE_TILE
    blk_e = jnp.minimum(jnp.searchsorted(pad_end, jnp.arange(nblk, dtype=jnp.int32) * MOE_TILE, side='right'),
                        N_EXPERTS - 1).astype(jnp.int32)
    nused = (pad_end[-1:] // MOE_TILE).astype(jnp.int32)
    xs = _dispatch(x2, pos, total)
    y = _experts(xs, blk_e, nused, wg, wu, wd)
    return _combine(y, pos, gates.T, x2, g, b)


def _pad_head_proj(w_uk, w_uv):
    H, DH, DC = DSA_HEADS, DSA_HEAD_DIM, DSA_LATENT
    eye = jnp.eye(H, dtype=w_uk.dtype)
    wuk_bd = jnp.einsum('hg,hdc->hdgc', eye, w_uk).reshape(H * DH, H * DC)
    wuv_bd = jnp.einsum('hg,hcd->hcgd', eye, w_uv).reshape(H * DC, H * DH)
    return wuk_bd.astype(BF16), wuv_bd.astype(BF16)


def kernel(x, w_in, mlstm_conv_w, mlstm_gate_bias, mlstm_norm_g, dsa_kv_norm_g, dsa_w_uk, dsa_w_uv, rel_bias,
           w_branch_a, w_branch_b, w_out, ln_g, ln_b, dense_w_gate, dense_w_up, dense_w_down, router_w,
           expert_w_gate, expert_w_up, expert_w_down):
    bsz, seq, d = x.shape
    n = bsz * seq
    x2 = x.reshape(n, d)
    tab = _near_bias_tables(rel_bias, *_dsa_tiles(seq))
    for l in range(DEPTH):
        p = dict(zip([nm for nm, _, _ in IN_PROJ_OUTS], _in_proj(x2, _regroup_in_proj(w_in[l]))))
        ya = _mlstm(p["qk"], p["av"], p["aif"], p["ao"], mlstm_conv_w[l], mlstm_gate_bias[l], mlstm_norm_g[l],
                    bsz, seq)
        ckv = _kv_norm(p["bc"], dsa_kv_norm_g[l])
        wuk_pad, wuv_pad = _pad_head_proj(dsa_w_uk[l], dsa_w_uv[l])
        yb = _dsa(p["bq"], p["iq"], p["ikw"], ckv, tab, wuk_pad, wuv_pad, bsz, seq)
        x2 = _merge(x2, ya, yb, p["ga"], p["gb"], w_branch_a[l].astype(BF16), w_branch_b[l].astype(BF16),
                    w_out[l].astype(BF16), ln_g[l, 0], ln_b[l, 0])
        j = l // 2
        if l % 2 == 0:
            x2 = _dense_ffn(x2, dense_w_gate[j].astype(BF16), dense_w_up[j].astype(BF16),
                            dense_w_down[j].astype(BF16), ln_g[l, 1], ln_b[l, 1])
        else:
            x2 = _moe(x2, router_w[j], expert_w_gate[j].astype(BF16), expert_w_up[j].astype(BF16),
                      expert_w_down[j].astype(BF16), ln_g[l, 1], ln_b[l, 1])
    return x2.reshape(bsz, seq, d)
```

```python
import functools
import math

import numpy as np
import jax
import jax.numpy as jnp
from jax import lax
from jax.experimental import pallas as pl
from jax.experimental.pallas import tpu as pltpu

D_MODEL = 1024
DEPTH = 4
MLSTM_HEADS = 4
MLSTM_QK_DIM = 64
MLSTM_V_DIM = 128
MLSTM_CONV = 4
DSA_HEADS = 8
DSA_HEAD_DIM = 64
DSA_LATENT = 128
IDX_HEADS = 8
IDX_DIM = 32
IDX_TOPK_MAX = 256
REL_BUCKETS = 32
REL_MAX_DIST = 128
D_FF = 2816
N_EXPERTS = 8
TOP_K = 2
D_FF_EXPERT = 3584
DN_ALPHA = (2 * DEPTH) ** 0.25
LN_EPS = 1e-5

A_QK = 2 * MLSTM_HEADS * MLSTM_QK_DIM
A_V = MLSTM_HEADS * MLSTM_V_DIM
A_GATE = 2 * MLSTM_HEADS
B_Q = DSA_HEADS * DSA_HEAD_DIM
I_Q = IDX_HEADS * IDX_DIM
PROJ_SIZES = (A_QK, A_V, A_V, A_GATE, B_Q, DSA_LATENT, I_Q, IDX_DIM, IDX_HEADS, D_MODEL, D_MODEL)

LANES = 128
SUBLANES = 8
VMEM_LIMIT = 56 * 1024 * 1024
TOK_TILE = 1024
MLSTM_TILE = 256
MLSTM_CHUNKS_PER_STEP = 2
DSA_TQ = 256
DSA_KN = 256
DSA_KF = 512
MOE_TILE = 512
MOE_FT = 1792
COMB_TILE = 256
DMA_UNROLL = 8
FF_CHUNK = 256
INT_MIN = -(2 ** 31)
NEG_BIG = -1e30
M_FLOOR = -1e29
LOG2E = math.log2(math.e)
F32_MIN_NORMAL = 2.0 ** -126
VALUE_PASSES = 16

BF16 = jnp.bfloat16
F32 = jnp.float32
NT_DIMS = (((1,), (1,)), ((), ()))


def _cparams(sem):
    return pltpu.CompilerParams(dimension_semantics=sem, vmem_limit_bytes=VMEM_LIMIT)


def _resident(shape):
    return pl.BlockSpec(shape, lambda *_: (0,) * len(shape), pipeline_mode=pl.Buffered(1))


def _dot(a, b):
    return jnp.dot(a, b, preferred_element_type=F32)


def _dot_nt(a, b):
    return lax.dot_general(a, b, NT_DIMS, preferred_element_type=F32)


def _layer_norm(r, g, b):
    mu = jnp.mean(r, axis=-1, keepdims=True)
    d = r - mu
    var = jnp.mean(d * d, axis=-1, keepdims=True)
    return d * lax.rsqrt(var + LN_EPS) * g + b


def _silu(x):
    return x * jax.nn.sigmoid(x)


IN_PROJ_OUTS = (("qk", A_QK, F32), ("av", A_V, BF16), ("ao", A_V, BF16), ("bq", B_Q, BF16),
                ("iq", I_Q, BF16), ("ga", D_MODEL, BF16), ("gb", D_MODEL, BF16),
                ("aif", LANES, F32), ("bc", LANES, F32), ("ikw", LANES, F32))
IN_PROJ_COLS = sum(w for _, w, _ in IN_PROJ_OUTS)
IKW_W_OFF = IDX_DIM


def _regroup_in_proj(w):
    offs = np.concatenate([[0], np.cumsum(PROJ_SIZES)])
    a_qk, a_v, a_o, a_if, b_q, b_c, i_q, i_k, i_w, g_a, g_b = (w[:, offs[i]:offs[i + 1]] for i in range(11))
    pad = lambda a: jnp.pad(a, ((0, 0), (0, LANES - a.shape[1])))
    groups = [a_qk, a_v, a_o, b_q, i_q, g_a, g_b, pad(a_if), b_c, pad(jnp.concatenate([i_k, i_w], axis=1))]
    return jnp.concatenate(groups, axis=1).astype(BF16)


def _in_proj_kernel(x_ref, w_ref, *out_refs):
    xb = x_ref[...].astype(BF16)
    off = 0
    for (_, width, dtype), o_ref in zip(IN_PROJ_OUTS, out_refs):
        o_ref[...] = _dot(xb, w_ref[:, off:off + width]).astype(dtype)
        off += width


def _in_proj(x2, w):
    n = x2.shape[0]
    tm = min(TOK_TILE, n)
    return pl.pallas_call(
        _in_proj_kernel,
        grid=(n // tm,),
        in_specs=[pl.BlockSpec((tm, D_MODEL), lambda i: (i, 0)),
                  _resident((D_MODEL, IN_PROJ_COLS))],
        out_specs=[pl.BlockSpec((tm, wd), lambda i: (i, 0)) for _, wd, _ in IN_PROJ_OUTS],
        out_shape=[jax.ShapeDtypeStruct((n, wd), dt) for _, wd, dt in IN_PROJ_OUTS],
        compiler_params=_cparams(("parallel",)),
        name="in_proj",
    )(x2, w)


def _log_sigmoid(x):
    return jnp.minimum(x, 0.0) - jnp.log1p(jnp.exp(-jnp.abs(x)))


def _mlstm_kernel(qk_ref, av_ref, aif_ref, ao_ref, convw_ref, gbias_ref, ng_ref, out_ref,
                  ext_ref, ct_ref, m_ref, *, tile, chunks):
    @pl.when(pl.program_id(1) == 0)
    def _():
        ext_ref[0:SUBLANES, :] = jnp.zeros((SUBLANES, A_QK), F32)
        ct_ref[...] = jnp.zeros_like(ct_ref)
        m_ref[...] = jnp.zeros_like(m_ref)

    for c in range(chunks):
        rows = pl.ds(c * tile, tile)
        _mlstm_chunk(qk_ref.at[rows], av_ref.at[rows], aif_ref.at[rows], ao_ref.at[rows], convw_ref, gbias_ref,
                     ng_ref, out_ref.at[rows], ext_ref, ct_ref, m_ref, tile=tile)


def _mlstm_chunk(qk_ref, av_ref, aif_ref, ao_ref, convw_ref, gbias_ref, ng_ref, out_ref,
                 ext_ref, ct_ref, m_ref, *, tile):
    L = tile
    H, DK, DV = MLSTM_HEADS, MLSTM_QK_DIM, MLSTM_V_DIM

    u = qk_ref[...]
    ext_ref[SUBLANES:SUBLANES + L, :] = u
    conv = jnp.zeros((L, A_QK), F32)
    for j in range(MLSTM_CONV):
        conv = conv + ext_ref[pl.ds(SUBLANES - (MLSTM_CONV - 1) + j, L), :] * convw_ref[j:j + 1, :]
    ext_ref[0:SUBLANES, :] = u[L - SUBLANES:L, :]
    qk = _silu(conv)

    g_col = aif_ref[...] + gbias_ref[...]
    g_row = g_col.T
    row = lax.broadcasted_iota(jnp.int32, (L, L), 0)
    col = lax.broadcasted_iota(jnp.int32, (L, L), 1)
    causal = row >= col
    tril = jnp.where(causal, 1.0, 0.0).astype(F32)
    triu = jnp.where(row <= col, 1.0, 0.0).astype(F32)
    b_col = jnp.dot(tril, _log_sigmoid(g_col), preferred_element_type=F32, precision=lax.Precision.HIGHEST)
    b_row = jnp.dot(_log_sigmoid(g_row), triu, preferred_element_type=F32, precision=lax.Precision.HIGHEST)

    ones_v = jnp.ones((L, DV), BF16)
    ones_t = jnp.ones((DV, L), F32)
    for h in range(H):
        bcol = b_col[:, H + h:H + h + 1]
        icol = g_col[:, h:h + 1]
        brow = b_row[H + h:H + h + 1, :]
        irow = g_row[h:h + 1, :]
        mprev = m_ref[h:h + 1, 0:1]
        dlog = jnp.where(causal, bcol - brow + irow, -jnp.inf)
        m_t = jnp.maximum(bcol + mprev, jnp.max(dlog, axis=1, keepdims=True))
        q = qk[:, h * DK:(h + 1) * DK].astype(BF16)
        k = qk[:, H * DK + h * DK:H * DK + (h + 1) * DK] * (DK ** -0.5)
        s_w = _dot_nt(q, k.astype(BF16)) * jnp.exp(dlog - m_t)
        inter = jnp.exp(bcol + mprev - m_t)
        v_h = av_ref[:, h * DV:(h + 1) * DV]
        v_ext = jnp.concatenate([v_h, ones_v], axis=1)
        ct_st = ct_ref[h]
        ne = _dot(s_w.astype(BF16), v_ext) + inter * _dot_nt(q, ct_st.astype(BF16))
        h_out = ne[:, :DV] / jnp.maximum(jnp.abs(ne[:, DV:]), jnp.exp(-m_t))
        blast = bcol[L - 1:L, :]
        glog = blast - bcol + icol
        m_new = jnp.maximum(blast + mprev, jnp.max(glog, axis=0, keepdims=True))
        decay = jnp.exp(blast + mprev - m_new)
        w_k = k * jnp.exp(glog - m_new)
        v_ext_t = jnp.concatenate([v_h.astype(F32).T, ones_t], axis=0).astype(BF16)
        ct_ref[h] = decay * ct_st + _dot(v_ext_t, w_k.astype(BF16))
        m_ref[h:h + 1, :] = jnp.broadcast_to(m_new, (1, LANES))
        mu = jnp.mean(h_out, axis=-1, keepdims=True)
        d = h_out - mu
        hn = d * lax.rsqrt(jnp.mean(d * d, axis=-1, keepdims=True) + LN_EPS)
        y = hn * ng_ref[:, h * DV:(h + 1) * DV] * jax.nn.sigmoid(ao_ref[:, h * DV:(h + 1) * DV].astype(F32))
        out_ref[:, h * DV:(h + 1) * DV] = y.astype(BF16)


def _mlstm(qk, av, aif, ao, conv_w, gate_bias, norm_g, bsz, seq):
    tile = min(MLSTM_TILE, seq)
    chunks = min(MLSTM_CHUNKS_PER_STEP, seq // tile)
    nc = seq // (tile * chunks)
    gbias = jnp.zeros((1, LANES), F32).at[0, :A_GATE].set(gate_bias.reshape(-1))
    tok = lambda w: pl.BlockSpec((tile * chunks, w), lambda b, c: (b * nc + c, 0))
    full = lambda shape: pl.BlockSpec(shape, lambda b, c: (0,) * len(shape))
    return pl.pallas_call(
        functools.partial(_mlstm_kernel, tile=tile, chunks=chunks),
        grid=(bsz, nc),
        in_specs=[tok(A_QK), tok(A_V), tok(LANES), tok(A_V),
                  full((MLSTM_CONV, A_QK)), full((1, LANES)), full((1, A_V))],
        out_specs=tok(A_V),
        out_shape=jax.ShapeDtypeStruct((bsz * seq, A_V), BF16),
        scratch_shapes=[pltpu.VMEM((tile + SUBLANES, A_QK), F32),
                        pltpu.VMEM((MLSTM_HEADS, 2 * MLSTM_V_DIM, MLSTM_QK_DIM), F32),
                        pltpu.VMEM((SUBLANES, LANES), F32)],
        compiler_params=_cparams(("parallel", "arbitrary")),
        name="mlstm",
    )(qk, av, aif, ao, conv_w, gbias, norm_g.reshape(1, A_V))


def _t5_bucket_np(dist):
    dist = np.maximum(dist, 0)
    exact = REL_BUCKETS // 2
    log_ratio = (np.log(np.maximum(dist, 1).astype(np.float32) / np.float32(exact))
                 / np.float32(math.log(REL_MAX_DIST / exact))).astype(np.float32)
    large = np.minimum(exact + (log_ratio * np.float32(REL_BUCKETS - exact)).astype(np.int32), REL_BUCKETS - 1)
    return np.where(dist < exact, dist, large)


def _near_bias_tables(rel_bias, tq, kn, kf):
    n_off = kf // tq + 1
    assert np.all(_t5_bucket_np(np.arange(tq + 1, 4 * REL_MAX_DIST)) == REL_BUCKETS - 1)
    d_max = (n_off - 1) * tq + tq - 1
    dist = np.arange(d_max, -kn, -1)
    by_dist = (rel_bias[_t5_bucket_np(dist)] - rel_bias[REL_BUCKETS - 1]) * LOG2E
    by_dist = jnp.where((dist >= 0)[:, None], by_dist, 0.0).T.astype(F32)
    rows = [[by_dist[:, d_max - (o * tq + t):d_max - (o * tq + t) + kn] for t in range(tq)] for o in range(n_off)]
    tab = jnp.stack([jnp.stack(r, axis=1) for r in rows], axis=0)
    return tab.reshape(n_off, DSA_HEADS * tq, kn)


def _dsa_kernel(bq_ref, iq_ref, ikwq_ref, ikw_ref, ckv_ref, tab_ref, wuk_ref, wuv_ref, out_ref,
                key_ref, keyt_ref, smax_ref, smin_ref, npos_ref, nnon_ref, acc_ref, m_ref, ql_ref, wb_ref,
                *, tq, kn, kf, topk):
    H = DSA_HEADS
    j = pl.program_id(1)
    t0 = j * tq
    n_score = (t0 + tq - 1) // kf + 1
    n_far = jnp.maximum(t0 - tq, 0) // kf
    near_lo = n_far * (kf // kn)
    near_hi = (t0 + tq - 1) // kn + 1

    ql_all = (_dot(bq_ref[...], wuk_ref[...]) * (DSA_HEAD_DIM ** -0.5 * LOG2E)).astype(BF16)
    for h in range(H):
        ql_ref[h * tq:(h + 1) * tq, :] = ql_all[:, h * DSA_LATENT:(h + 1) * DSA_LATENT]
    w_idx = ikwq_ref[:, IKW_W_OFF:IKW_W_OFF + IDX_HEADS] * ((IDX_HEADS * IDX_DIM) ** -0.5)
    for h in range(IDX_HEADS):
        wb_ref[h] = jnp.broadcast_to(w_idx[:, h:h + 1], (tq, LANES))
    iq = iq_ref[...]
    qis = [iq[:, h * IDX_DIM:(h + 1) * IDX_DIM] for h in range(IDX_HEADS)]
    tpos = t0 + lax.broadcasted_iota(jnp.int32, (tq, kf), 0)
    lpos = lax.broadcasted_iota(jnp.int32, (tq, kf), 1)

    ncol_f = kf // LANES
    tile_f = lambda a: jnp.concatenate([a] * ncol_f, axis=1)

    flip = lambda a: a ^ ((a >> 31) & 0x7FFFFFFF)
    to_key = lambda v: flip(lax.bitcast_convert_type(v, jnp.int32))
    to_val = lambda k: lax.bitcast_convert_type(flip(k), F32)
    fold = lambda a: a.reshape(kf // SUBLANES, SUBLANES, tq)

    smax_ref[...] = jnp.full(smax_ref.shape, -jnp.inf, F32)
    smin_ref[...] = jnp.full(smin_ref.shape, jnp.inf, F32)
    npos_ref[...] = jnp.zeros_like(npos_ref)
    nnon_ref[...] = jnp.zeros_like(nnon_ref)

    def score_chunk(c, masked):
        s0 = pl.multiple_of(c * kf, kf)
        kidx = ikw_ref[pl.ds(s0, kf), :][:, 0:IDX_DIM].astype(BF16)
        parts = [jnp.zeros((tq, LANES), F32)] * ncol_f
        for h in range(IDX_HEADS):
            z = jnp.maximum(_dot_nt(qis[h], kidx), 0.0)
            w_h = wb_ref[h]
            parts = [p + w_h * z[:, g * LANES:(g + 1) * LANES] for g, p in enumerate(parts)]
        sc = jnp.concatenate(parts, axis=1)
        sc = jnp.where(jnp.abs(sc) < F32_MIN_NORMAL, 0.0, sc)
        key = to_key(sc)
        sc_t = sc.T
        sc_t_lo = sc_t
        if masked:
            causal = s0 + lpos <= tpos
            key = jnp.where(causal, key, INT_MIN)
            sc_t = jnp.where(causal, sc, -jnp.inf).T
            sc_t_lo = jnp.where(causal, sc, jnp.inf).T
        key_ref[:, pl.ds(s0, kf)] = key
        keyt_ref[pl.ds(s0, kf), :] = to_key(sc_t)
        smax_ref[...] = jnp.maximum(smax_ref[...], jnp.max(fold(sc_t), axis=0))
        smin_ref[...] = jnp.minimum(smin_ref[...], jnp.min(fold(sc_t_lo), axis=0))
        npos_ref[...] = npos_ref[...] + jnp.sum(fold(jnp.where(sc_t > 0.0, 1.0, 0.0)), axis=0)
        nnon_ref[...] = nnon_ref[...] + jnp.sum(fold(jnp.where(sc_t >= 0.0, 1.0, 0.0)), axis=0)

    def score_run(first, count):
        for u in range(count):
            score_chunk(first + u, False)

    def score_quad(i, carry):
        score_run(4 * i, 4)
        return carry

    n_plain = n_score - 1
    lax.fori_loop(0, n_plain // 4, score_quad, 0)

    @pl.when(n_plain % 4 >= 2)
    def _():
        score_run((n_plain // 4) * 4, 2)

    @pl.when(n_plain % 2 == 1)
    def _():
        score_run(n_plain - 1, 1)

    score_chunk(n_score - 1, True)

    per_query = lambda a, op: jnp.broadcast_to(op(a, axis=0, keepdims=True), (SUBLANES, tq))
    n_acc = 8

    def count_ge(cand):
        def body(c, acc):
            s0 = pl.multiple_of(c * kf, kf)
            ind = jnp.where(fold(keyt_ref[pl.ds(s0, kf), :]) >= cand[None], 1.0, 0.0)
            return acc + jnp.sum(ind.reshape(kf // SUBLANES // n_acc, n_acc, SUBLANES, tq), axis=0)
        acc = lax.fori_loop(0, n_score, body, jnp.zeros((n_acc, SUBLANES, tq), F32))
        return per_query(jnp.sum(acc, axis=0), jnp.sum)

    n_valid = (t0 + 1 + lax.broadcasted_iota(jnp.int32, (SUBLANES, tq), 1)).astype(F32)
    enough = n_valid >= topk
    lo0 = to_key(per_query(smin_ref[...], jnp.min))
    hi0 = to_key(per_query(smax_ref[...], jnp.max)) + 1
    n_pos = per_query(npos_ref[...], jnp.sum)
    n_non = per_query(nnon_ref[...], jnp.sum)
    pos_thr = n_pos >= topk
    zero_thr = jnp.logical_and(n_non >= topk, jnp.logical_not(pos_thr))
    cnt0 = jnp.where(pos_thr, n_pos, jnp.where(zero_thr, n_non, n_valid))
    lo0 = jnp.where(pos_thr, 1, jnp.where(zero_thr, 0, lo0))
    hi0 = jnp.where(pos_thr, jnp.where(n_pos == topk, 2, hi0), jnp.where(zero_thr, 1, 0))

    def probe(mid, lo, hi, cnt_lo):
        cnt = count_ge(mid)
        ge = cnt >= topk
        return (jnp.where(ge, mid, lo), jnp.where(cnt == topk, mid + 1, jnp.where(ge, hi, mid)),
                jnp.where(ge, cnt, cnt_lo))

    def by_value(i, state):
        lo, hi, cnt_lo = state
        mid = to_key(0.5 * to_val(lo) + 0.5 * to_val(hi))
        return probe(jnp.where(jnp.logical_and(mid > lo, mid < hi), mid, lo), lo, hi, cnt_lo)

    state = lax.fori_loop(0, VALUE_PASSES, by_value, (lo0, hi0, cnt0))

    def any_open(lo, hi):
        return jnp.max(jnp.where(jnp.logical_and(enough, hi > lo + 1), 1.0, 0.0))

    def by_key(state):
        lo, hi, cnt_lo, _ = state
        lo, hi, cnt_lo = probe((lo >> 1) + (hi >> 1) + (lo & hi & 1), lo, hi, cnt_lo)
        return lo, hi, cnt_lo, any_open(lo, hi)

    lo, _, cnt_lo, _ = lax.while_loop(lambda s: s[3] > 0.0, by_key, state + (any_open(state[0], state[1]),))
    thr_q = jnp.where(enough, lo, INT_MIN + 1)
    thr = jnp.broadcast_to(thr_q[0:1, :], (LANES, tq)).T

    def count_key(pred):
        def body(c, acc):
            s0 = pl.multiple_of(c * kf, kf)
            ind = jnp.where(pred(key_ref[:, pl.ds(s0, kf)]), 1.0, 0.0)
            for g in range(ncol_f):
                acc = acc + ind[:, g * LANES:(g + 1) * LANES]
            return acc
        acc = lax.fori_loop(0, n_score, body, jnp.zeros((tq, LANES), F32))
        return jnp.sum(acc, axis=1, keepdims=True)

    thr_f = tile_f(thr)
    surplus = jnp.where(enough, cnt_lo - topk, 0.0)

    @pl.when(jnp.max(surplus) > 0.0)
    def _():
        allowed = topk - count_key(lambda k: k > thr_f)
        r = lax.broadcasted_iota(jnp.int32, (kf, kf), 0)
        cidx = lax.broadcasted_iota(jnp.int32, (kf, kf), 1)
        upper = jnp.where(r <= cidx, 1.0, 0.0).astype(BF16)

        def body(c, seen):
            s0 = pl.multiple_of(c * kf, kf)
            key = key_ref[:, pl.ds(s0, kf)]
            tie = key == thr_f
            tie_f = jnp.where(tie, 1.0, 0.0)
            rank = _dot(tie_f.astype(BF16), upper) + seen
            key_ref[:, pl.ds(s0, kf)] = jnp.where(jnp.logical_and(tie, rank > allowed), thr_f - 1, key)
            return seen + jnp.sum(tie_f, axis=1, keepdims=True)
        lax.fori_loop(0, n_score, body, jnp.zeros((tq, 1), F32))

    m_ref[...] = jnp.full(m_ref.shape, M_FLOOR, F32)
    acc_ref[...] = jnp.zeros_like(acc_ref)

    def attend(s0, width, table_idx):
        ncol = width // LANES
        kv = ckv_ref[pl.ds(s0, width), :]
        kv_ext = jnp.concatenate([kv, jnp.ones((width, DSA_LATENT), BF16)], axis=1)
        thr_w = jnp.concatenate([thr] * ncol, axis=1)
        madd = jnp.where(key_ref[:, pl.ds(s0, width)] >= thr_w, 0.0, NEG_BIG)
        for h in range(H):
            rows = slice(h * tq, (h + 1) * tq)
            x = _dot_nt(ql_ref[rows, :], kv) + madd
            if table_idx is not None:
                x = x + tab_ref[table_idx, rows, :]
            m_old = m_ref[rows, :]
            m_new = jnp.maximum(m_old, jnp.max(x, axis=1, keepdims=True))
            alpha = jnp.exp2(m_old - m_new)
            p = jnp.exp2(x - jnp.concatenate([m_new] * ncol, axis=1))
            acc_ref[rows, :] = jnp.concatenate([alpha, alpha], axis=1) * acc_ref[rows, :] + _dot(p.astype(BF16), kv_ext)
            m_ref[rows, :] = m_new

    def far_run(first, count):
        for u in range(count):
            attend(pl.multiple_of((first + u) * kf, kf), kf, None)

    def far_quad(i, carry):
        far_run(4 * i, 4)
        return carry

    def near_chunk(c, carry):
        s0 = pl.multiple_of(c * kn, kn)
        attend(s0, kn, (t0 - s0) // tq)
        return carry

    lax.fori_loop(0, n_far // 4, far_quad, 0)

    @pl.when(n_far % 4 >= 2)
    def _():
        far_run((n_far // 4) * 4, 2)

    @pl.when(n_far % 2 == 1)
    def _():
        far_run(n_far - 1, 1)

    lax.fori_loop(near_lo, near_hi, near_chunk, 0)

    outs = []
    for h in range(H):
        a = acc_ref[h * tq:(h + 1) * tq, :]
        outs.append((a[:, :DSA_LATENT] / a[:, DSA_LATENT:]).astype(BF16))
    out_ref[...] = _dot(jnp.concatenate(outs, axis=1), wuv_ref[...]).astype(BF16)


def _dsa_tiles(seq):
    tq = min(DSA_TQ, seq)
    return tq, min(DSA_KN, seq), min(DSA_KF, seq)


def _dsa(bq, iq, ikw, ckv, tab, wuk_pad, wuv_pad, bsz, seq):
    tq, kn, kf = _dsa_tiles(seq)
    nq = seq // tq
    topk = min(IDX_TOPK_MAX, seq // 4)
    n_off = kf // tq + 1
    qblk = lambda w: pl.BlockSpec((tq, w), lambda b, j: (b * nq + j, 0))
    seqblk = lambda w: pl.BlockSpec((seq, w), lambda b, j: (b, 0))
    return pl.pallas_call(
        functools.partial(_dsa_kernel, tq=tq, kn=kn, kf=kf, topk=topk),
        grid=(bsz, nq),
        in_specs=[qblk(B_Q), qblk(I_Q), qblk(LANES), seqblk(LANES), seqblk(DSA_LATENT),
                  _resident((n_off, DSA_HEADS * tq, kn)), _resident((B_Q, DSA_HEADS * DSA_LATENT)),
                  _resident((DSA_HEADS * DSA_LATENT, B_Q))],
        out_specs=qblk(B_Q),
        out_shape=jax.ShapeDtypeStruct((bsz * seq, B_Q), BF16),
        scratch_shapes=[pltpu.VMEM((tq, seq), jnp.int32),
                        pltpu.VMEM((seq, tq), jnp.int32),
                        pltpu.VMEM((SUBLANES, tq), F32),
                        pltpu.VMEM((SUBLANES, tq), F32),
                        pltpu.VMEM((SUBLANES, tq), F32),
                        pltpu.VMEM((SUBLANES, tq), F32),
                        pltpu.VMEM((DSA_HEADS * tq, 2 * DSA_LATENT), F32),
                        pltpu.VMEM((DSA_HEADS * tq, LANES), F32),
                        pltpu.VMEM((DSA_HEADS * tq, DSA_LATENT), BF16),
                        pltpu.VMEM((IDX_HEADS, tq, LANES), F32)],
        compiler_params=_cparams(("parallel", "arbitrary")),
        name="dsa",
    )(bq, iq, ikw, ikw, ckv, tab, wuk_pad, wuv_pad)


def _kv_norm_kernel(bc_ref, g_ref, out_ref):
    x = bc_ref[...]
    out_ref[...] = (x * lax.rsqrt(jnp.mean(x * x, axis=-1, keepdims=True) + LN_EPS) * g_ref[...]).astype(BF16)


def _kv_norm(bc, g):
    n = bc.shape[0]
    tm = min(2 * TOK_TILE, n)
    return pl.pallas_call(
        _kv_norm_kernel,
        grid=(n // tm,),
        in_specs=[pl.BlockSpec((tm, DSA_LATENT), lambda i: (i, 0)), pl.BlockSpec((1, DSA_LATENT), lambda i: (0, 0))],
        out_specs=pl.BlockSpec((tm, DSA_LATENT), lambda i: (i, 0)),
        out_shape=jax.ShapeDtypeStruct((n, DSA_LATENT), BF16),
        compiler_params=_cparams(("parallel",)),
        name="kv_norm",
    )(bc, g.reshape(1, DSA_LATENT))


def _merge_kernel(x_ref, ya_ref, yb_ref, ga_ref, gb_ref, wa_ref, wb_ref, wo_ref, g_ref, b_ref, out_ref):
    merged = (jax.nn.sigmoid(ga_ref[...].astype(F32)) * _dot(ya_ref[...], wa_ref[...])
              + jax.nn.sigmoid(gb_ref[...].astype(F32)) * _dot(yb_ref[...], wb_ref[...]))
    mix = _dot(merged.astype(BF16), wo_ref[...])
    out_ref[...] = _layer_norm(DN_ALPHA * x_ref[...] + mix, g_ref[...], b_ref[...])


def _merge(x2, ya, yb, ga, gb, wa, wb, wo, g, b):
    n = x2.shape[0]
    tm = min(TOK_TILE, n)
    tok = lambda w: pl.BlockSpec((tm, w), lambda i: (i, 0))
    full = _resident
    return pl.pallas_call(
        _merge_kernel,
        grid=(n // tm,),
        in_specs=[tok(D_MODEL), tok(A_V), tok(B_Q), tok(D_MODEL), tok(D_MODEL),
                  full((A_V, D_MODEL)), full((B_Q, D_MODEL)), full((D_MODEL, D_MODEL)),
                  full((1, D_MODEL)), full((1, D_MODEL))],
        out_specs=tok(D_MODEL),
        out_shape=jax.ShapeDtypeStruct((n, D_MODEL), F32),
        compiler_params=_cparams(("parallel",)),
        name="merge_ln",
    )(x2, ya, yb, ga, gb, wa, wb, wo, g.reshape(1, D_MODEL), b.reshape(1, D_MODEL))


def _dense_ffn_kernel(x_ref, wg_ref, wu_ref, wd_ref, g_ref, b_ref, out_ref):
    x = x_ref[...]
    xb = x.astype(BF16)
    f = jnp.zeros(x.shape, F32)
    for c in range(D_FF // FF_CHUNK):
        cols = slice(c * FF_CHUNK, (c + 1) * FF_CHUNK)
        hdn = _silu(_dot(xb, wg_ref[:, cols])) * _dot(xb, wu_ref[:, cols])
        f = f + _dot(hdn.astype(BF16), wd_ref[cols, :])
    out_ref[...] = _layer_norm(DN_ALPHA * x + f, g_ref[...], b_ref[...])


def _dense_ffn(x2, wg, wu, wd, g, b):
    n = x2.shape[0]
    tm = min(TOK_TILE, n)
    tok = pl.BlockSpec((tm, D_MODEL), lambda i: (i, 0))
    full = _resident
    return pl.pallas_call(
        _dense_ffn_kernel,
        grid=(n // tm,),
        in_specs=[tok, full((D_MODEL, D_FF)), full((D_MODEL, D_FF)), full((D_FF, D_MODEL)),
                  full((1, D_MODEL)), full((1, D_MODEL))],
        out_specs=tok,
        out_shape=jax.ShapeDtypeStruct((n, D_MODEL), F32),
        compiler_params=_cparams(("parallel",)),
        name="dense_ffn_ln",
    )(x2, wg, wu, wd, g.reshape(1, D_MODEL), b.reshape(1, D_MODEL))


def _router_kernel(x_ref, rw_ref, e_ref, g_ref):
    logits = lax.dot_general(rw_ref[...], x_ref[...], NT_DIMS, preferred_element_type=F32,
                             precision=lax.Precision.HIGHEST)
    eidx = lax.broadcasted_iota(jnp.int32, logits.shape, 0)
    m1 = jnp.max(logits, axis=0, keepdims=True)
    i1 = jnp.min(jnp.where(logits == m1, eidx, N_EXPERTS), axis=0, keepdims=True)
    rest = jnp.where(eidx == i1, -jnp.inf, logits)
    m2 = jnp.max(rest, axis=0, keepdims=True)
    i2 = jnp.min(jnp.where(rest == m2, eidx, N_EXPERTS), axis=0, keepdims=True)
    e2 = jnp.exp(m2 - m1)
    den = 1.0 + e2
    e_ref[0:1, :] = i1
    e_ref[1:2, :] = i2
    g_ref[0:1, :] = 1.0 / den
    g_ref[1:2, :] = e2 / den


def _router(x2, rw_t):
    n = x2.shape[0]
    tm = min(TOK_TILE, n)
    return pl.pallas_call(
        _router_kernel,
        grid=(n // tm,),
        in_specs=[pl.BlockSpec((tm, D_MODEL), lambda i: (i, 0)), pl.BlockSpec((N_EXPERTS, D_MODEL), lambda i: (0, 0))],
        out_specs=[pl.BlockSpec((TOP_K, tm), lambda i: (0, i)), pl.BlockSpec((TOP_K, tm), lambda i: (0, i))],
        out_shape=[jax.ShapeDtypeStruct((TOP_K, n), jnp.int32), jax.ShapeDtypeStruct((TOP_K, n), F32)],
        compiler_params=_cparams(("parallel",)),
        name="router",
    )(x2, rw_t)


def _row_copy(src, src_row, dst, dst_row, sem):
    return pltpu.make_async_copy(src.at[pl.ds(src_row, 1)], dst.at[pl.ds(dst_row, 1)], sem)


def _dispatch_kernel(d0_ref, d1_ref, x_ref, xs_init_hbm, xs_hbm, sem, *, tile):
    del xs_init_hbm

    def start(r, c):
        _row_copy(x_ref, r, xs_hbm, d0_ref[0, r], sem).start(priority=0)
        _row_copy(x_ref, r, xs_hbm, d1_ref[0, r], sem).start(priority=1)
        return c
    lax.fori_loop(0, tile, start, 0, unroll=DMA_UNROLL)

    def wait(r, c):
        _row_copy(x_ref, r, xs_hbm, 0, sem).wait()
        _row_copy(x_ref, r, xs_hbm, 0, sem).wait()
        return c
    lax.fori_loop(0, tile, wait, 0, unroll=DMA_UNROLL)


def _dispatch(x2, pos, total):
    n = x2.shape[0]
    tile = min(COMB_TILE, n)
    nb = n // tile
    smem = pl.BlockSpec((None, 1, tile), lambda i: (i, 0, 0), memory_space=pltpu.SMEM)
    return pl.pallas_call(
        functools.partial(_dispatch_kernel, tile=tile),
        grid=(nb,),
        in_specs=[smem, smem, pl.BlockSpec((tile, D_MODEL), lambda i: (i, 0)), pl.BlockSpec(memory_space=pl.ANY)],
        out_specs=pl.BlockSpec(memory_space=pl.ANY),
        out_shape=jax.ShapeDtypeStruct((total, D_MODEL), F32),
        scratch_shapes=[pltpu.SemaphoreType.DMA(())],
        input_output_aliases={3: 0},
        compiler_params=_cparams(("arbitrary",)),
        name="moe_dispatch",
    )(pos[:, 0].reshape(nb, 1, tile), pos[:, 1].reshape(nb, 1, tile), x2, jnp.zeros((total, D_MODEL), F32))


def _expert_kernel(blk_e_ref, nused_ref, xs_ref, wg_ref, wu_ref, wd_ref, y_ref, xb, acc):
    i = pl.program_id(0)
    f = pl.program_id(1)
    used = i < nused_ref[0]
    last = f == pl.num_programs(1) - 1

    @pl.when(jnp.logical_and(used, f == 0))
    def _():
        xb[...] = xs_ref[...].astype(BF16)
        acc[...] = jnp.zeros_like(acc)

    @pl.when(used)
    def _():
        x = xb[...]
        hdn = _silu(_dot(x, wg_ref[...])) * _dot(x, wu_ref[...])
        acc[...] += _dot(hdn.astype(BF16), wd_ref[...])

    @pl.when(jnp.logical_and(used, last))
    def _():
        y_ref[...] = acc[...]

    @pl.when(jnp.logical_and(jnp.logical_not(used), last))
    def _():
        y_ref[...] = jnp.zeros_like(y_ref)


def _experts(xs, blk_e, nused, wg, wu, wd):
    total = xs.shape[0]
    tile = MOE_TILE
    nblk = total // tile
    nf = D_FF_EXPERT // MOE_FT
    grid_spec = pltpu.PrefetchScalarGridSpec(
        num_scalar_prefetch=2,
        grid=(nblk, nf),
        in_specs=[pl.BlockSpec((tile, D_MODEL), lambda i, f, be, nu: (i, 0)),
                  pl.BlockSpec((None, D_MODEL, MOE_FT), lambda i, f, be, nu: (be[i], 0, f)),
                  pl.BlockSpec((None, D_MODEL, MOE_FT), lambda i, f, be, nu: (be[i], 0, f)),
                  pl.BlockSpec((None, MOE_FT, D_MODEL), lambda i, f, be, nu: (be[i], f, 0))],
        out_specs=pl.BlockSpec((tile, D_MODEL), lambda i, f, be, nu: (i, 0)),
        scratch_shapes=[pltpu.VMEM((tile, D_MODEL), BF16), pltpu.VMEM((tile, D_MODEL), F32)],
    )
    return pl.pallas_call(
        _expert_kernel,
        grid_spec=grid_spec,
        out_shape=jax.ShapeDtypeStruct((total, D_MODEL), F32),
        compiler_params=_cparams(("arbitrary", "arbitrary")),
        name="experts",
    )(blk_e, nused, xs, wg, wu, wd)


def _combine_kernel(p0_ref, p1_ref, y_hbm, x_ref, gate_ref, g_ref, b_ref, out_ref, buf0, buf1, sem, *, tile):
    def start(r, c):
        _row_copy(y_hbm, p0_ref[0, r], buf0, r, sem).start(priority=0)
        _row_copy(y_hbm, p1_ref[0, r], buf1, r, sem).start(priority=1)
        return c
    lax.fori_loop(0, tile, start, 0, unroll=DMA_UNROLL)

    def wait(r, c):
        _row_copy(y_hbm, 0, buf0, r, sem).wait()
        _row_copy(y_hbm, 0, buf1, r, sem).wait()
        return c
    lax.fori_loop(0, tile, wait, 0, unroll=DMA_UNROLL)
    f = buf0[...] * gate_ref[:, 0:1] + buf1[...] * gate_ref[:, 1:2]
    out_ref[...] = _layer_norm(DN_ALPHA * x_ref[...] + f, g_ref[...], b_ref[...])


def _combine(y, pos, gates, x2, g, b):
    n = x2.shape[0]
    tile = min(COMB_TILE, n)
    nb = n // tile
    smem = pl.BlockSpec((None, 1, tile), lambda i: (i, 0, 0), memory_space=pltpu.SMEM)
    full = pl.BlockSpec((1, D_MODEL), lambda i: (0, 0))
    return pl.pallas_call(
        functools.partial(_combine_kernel, tile=tile),
        grid=(nb,),
        in_specs=[smem, smem, pl.BlockSpec(memory_space=pl.ANY),
                  pl.BlockSpec((tile, D_MODEL), lambda i: (i, 0)),
                  pl.BlockSpec((tile, TOP_K), lambda i: (i, 0)), full, full],
        out_specs=pl.BlockSpec((tile, D_MODEL), lambda i: (i, 0)),
        out_shape=jax.ShapeDtypeStruct((n, D_MODEL), F32),
        scratch_shapes=[pltpu.VMEM((tile, D_MODEL), F32), pltpu.VMEM((tile, D_MODEL), F32),
                        pltpu.SemaphoreType.DMA(())],
        compiler_params=_cparams(("arbitrary",)),
        name="moe_combine_ln",
    )(pos[:, 0].reshape(nb, 1, tile), pos[:, 1].reshape(nb, 1, tile), y, x2, gates,
      g.reshape(1, D_MODEL), b.reshape(1, D_MODEL))


def _moe(x2, router_w, wg, wu, wd, g, b):
    n = x2.shape[0]
    top_e, gates = _router(x2, router_w.T)
    flat_e = top_e.T.reshape(-1)
    onehot = (flat_e[:, None] == jnp.arange(N_EXPERTS, dtype=jnp.int32)[None, :]).astype(jnp.int32)
    rank = jnp.sum((jnp.cumsum(onehot, axis=0) - onehot) * onehot, axis=1)
    counts = jnp.sum(onehot, axis=0)
    padded = ((counts + MOE_TILE - 1) // MOE_TILE) * MOE_TILE
    pad_end = jnp.cumsum(padded)
    pad_start = pad_end - padded
    pos = (pad_start[flat_e] + rank).astype(jnp.int32).reshape(n, TOP_K)
    total = n * TOP_K + N_EXPERTS * MOE_TILE
    nblk = total // MOE_TILE
    blk_e = jnp.minimum(jnp.searchsorted(pad_end, jnp.arange(nblk, dtype=jnp.int32) * MOE_TILE, side='right'),
                        N_EXPERTS - 1).astype(jnp.int32)
    nused = (pad_end[-1:] // MOE_TILE).astype(jnp.int32)
    xs = _dispatch(x2, pos, total)
    y = _experts(xs, blk_e, nused, wg, wu, wd)
    return _combine(y, pos, gates.T, x2, g, b)


def _pad_head_proj(w_uk, w_uv):
    H, DH, DC = DSA_HEADS, DSA_HEAD_DIM, DSA_LATENT
    eye = jnp.eye(H, dtype=w_uk.dtype)
    wuk_bd = jnp.einsum('hg,hdc->hdgc', eye, w_uk).reshape(H * DH, H * DC)
    wuv_bd = jnp.einsum('hg,hcd->hcgd', eye, w_uv).reshape(H * DC, H * DH)
    return wuk_bd.astype(BF16), wuv_bd.astype(BF16)


def kernel(x, w_in, mlstm_conv_w, mlstm_gate_bias, mlstm_norm_g, dsa_kv_norm_g, dsa_w_uk, dsa_w_uv, rel_bias,
           w_branch_a, w_branch_b, w_out, ln_g, ln_b, dense_w_gate, dense_w_up, dense_w_down, router_w,
           expert_w_gate, expert_w_up, expert_w_down):
    bsz, seq, d = x.shape
    n = bsz * seq
    x2 = x.reshape(n, d)
    tab = _near_bias_tables(rel_bias, *_dsa_tiles(seq))
    for l in range(DEPTH):
        p = dict(zip([nm for nm, _, _ in IN_PROJ_OUTS], _in_proj(x2, _regroup_in_proj(w_in[l]))))
        ya = _mlstm(p["qk"], p["av"], p["aif"], p["ao"], mlstm_conv_w[l], mlstm_gate_bias[l], mlstm_norm_g[l],
                    bsz, seq)
        ckv = _kv_norm(p["bc"], dsa_kv_norm_g[l])
        wuk_pad, wuv_pad = _pad_head_proj(dsa_w_uk[l], dsa_w_uv[l])
        yb = _dsa(p["bq"], p["iq"], p["ikw"], ckv, tab, wuk_pad, wuv_pad, bsz, seq)
        x2 = _merge(x2, ya, yb, p["ga"], p["gb"], w_branch_a[l].astype(BF16), w_branch_b[l].astype(BF16),
                    w_out[l].astype(BF16), ln_g[l, 0], ln_b[l, 0])
        j = l // 2
        if l % 2 == 0:
            x2 = _dense_ffn(x2, dense_w_gate[j].astype(BF16), dense_w_up[j].astype(BF16),
                            dense_w_down[j].astype(BF16), ln_g[l, 1], ln_b[l, 1])
        else:
            x2 = _moe(x2, router_w[j], expert_w_gate[j].astype(BF16), expert_w_up[j].astype(BF16),
                      expert_w_down[j].astype(BF16), ln_g[l, 1], ln_b[l, 1])
    return x2.reshape(bsz, seq, d)
```

```python
import functools
import math

import numpy as np
import jax
import jax.numpy as jnp
from jax import lax
from jax.experimental import pallas as pl
from jax.experimental.pallas import tpu as pltpu

D_MODEL = 1024
DEPTH = 4
MLSTM_HEADS = 4
MLSTM_QK_DIM = 64
MLSTM_V_DIM = 128
MLSTM_CONV = 4
DSA_HEADS = 8
DSA_HEAD_DIM = 64
DSA_LATENT = 128
IDX_HEADS = 8
IDX_DIM = 32
IDX_TOPK_MAX = 256
REL_BUCKETS = 32
REL_MAX_DIST = 128
D_FF = 2816
N_EXPERTS = 8
TOP_K = 2
D_FF_EXPERT = 3584
DN_ALPHA = (2 * DEPTH) ** 0.25
LN_EPS = 1e-5

A_QK = 2 * MLSTM_HEADS * MLSTM_QK_DIM
A_V = MLSTM_HEADS * MLSTM_V_DIM
A_GATE = 2 * MLSTM_HEADS
B_Q = DSA_HEADS * DSA_HEAD_DIM
I_Q = IDX_HEADS * IDX_DIM
PROJ_SIZES = (A_QK, A_V, A_V, A_GATE, B_Q, DSA_LATENT, I_Q, IDX_DIM, IDX_HEADS, D_MODEL, D_MODEL)

LANES = 128
SUBLANES = 8
VMEM_LIMIT = 56 * 1024 * 1024
TOK_TILE = 1024
MLSTM_TILE = 256
MLSTM_CHUNKS_PER_STEP = 2
DSA_TQ = 256
DSA_KN = 256
DSA_KF = 512
MOE_TILE = 512
MOE_FT = 1792
COMB_TILE = 256
DMA_UNROLL = 8
FF_CHUNK = 256
INT_MIN = -(2 ** 31)
NEG_BIG = -1e30
M_FLOOR = -1e29
LOG2E = math.log2(math.e)
F32_MIN_NORMAL = 2.0 ** -126
VALUE_PASSES = 16

BF16 = jnp.bfloat16
F32 = jnp.float32
NT_DIMS = (((1,), (1,)), ((), ()))


def _cparams(sem):
    return pltpu.CompilerParams(dimension_semantics=sem, vmem_limit_bytes=VMEM_LIMIT)


def _resident(shape):
    return pl.BlockSpec(shape, lambda *_: (0,) * len(shape), pipeline_mode=pl.Buffered(1))


def _dot(a, b):
    return jnp.dot(a, b, preferred_element_type=F32)


def _dot_nt(a, b):
    return lax.dot_general(a, b, NT_DIMS, preferred_element_type=F32)


def _layer_norm(r, g, b):
    mu = jnp.mean(r, axis=-1, keepdims=True)
    d = r - mu
    var = jnp.mean(d * d, axis=-1, keepdims=True)
    return d * lax.rsqrt(var + LN_EPS) * g + b


def _silu(x):
    return x * jax.nn.sigmoid(x)


IN_PROJ_OUTS = (("qk", A_QK, F32), ("av", A_V, BF16), ("ao", A_V, BF16), ("bq", B_Q, BF16),
                ("iq", I_Q, BF16), ("ga", D_MODEL, BF16), ("gb", D_MODEL, BF16),
                ("aif", LANES, F32), ("bc", LANES, F32), ("ikw", LANES, F32))
IN_PROJ_COLS = sum(w for _, w, _ in IN_PROJ_OUTS)
IKW_W_OFF = IDX_DIM


def _regroup_in_proj(w):
    offs = np.concatenate([[0], np.cumsum(PROJ_SIZES)])
    a_qk, a_v, a_o, a_if, b_q, b_c, i_q, i_k, i_w, g_a, g_b = (w[:, offs[i]:offs[i + 1]] for i in range(11))
    pad = lambda a: jnp.pad(a, ((0, 0), (0, LANES - a.shape[1])))
    groups = [a_qk, a_v, a_o, b_q, i_q, g_a, g_b, pad(a_if), b_c, pad(jnp.concatenate([i_k, i_w], axis=1))]
    return jnp.concatenate(groups, axis=1).astype(BF16)


def _in_proj_kernel(x_ref, w_ref, *out_refs):
    xb = x_ref[...].astype(BF16)
    off = 0
    for (_, width, dtype), o_ref in zip(IN_PROJ_OUTS, out_refs):
        o_ref[...] = _dot(xb, w_ref[:, off:off + width]).astype(dtype)
        off += width


def _in_proj(x2, w):
    n = x2.shape[0]
    tm = min(TOK_TILE, n)
    return pl.pallas_call(
        _in_proj_kernel,
        grid=(n // tm,),
        in_specs=[pl.BlockSpec((tm, D_MODEL), lambda i: (i, 0)),
                  _resident((D_MODEL, IN_PROJ_COLS))],
        out_specs=[pl.BlockSpec((tm, wd), lambda i: (i, 0)) for _, wd, _ in IN_PROJ_OUTS],
        out_shape=[jax.ShapeDtypeStruct((n, wd), dt) for _, wd, dt in IN_PROJ_OUTS],
        compiler_params=_cparams(("parallel",)),
        name="in_proj",
    )(x2, w)


def _log_sigmoid(x):
    return jnp.minimum(x, 0.0) - jnp.log1p(jnp.exp(-jnp.abs(x)))


def _mlstm_kernel(qk_ref, av_ref, aif_ref, ao_ref, convw_ref, gbias_ref, ng_ref, out_ref,
                  ext_ref, ct_ref, m_ref, *, tile, chunks):
    @pl.when(pl.program_id(1) == 0)
    def _():
        ext_ref[0:SUBLANES, :] = jnp.zeros((SUBLANES, A_QK), F32)
        ct_ref[...] = jnp.zeros_like(ct_ref)
        m_ref[...] = jnp.zeros_like(m_ref)

    for c in range(chunks):
        rows = pl.ds(c * tile, tile)
        _mlstm_chunk(qk_ref.at[rows], av_ref.at[rows], aif_ref.at[rows], ao_ref.at[rows], convw_ref, gbias_ref,
                     ng_ref, out_ref.at[rows], ext_ref, ct_ref, m_ref, tile=tile)


def _mlstm_chunk(qk_ref, av_ref, aif_ref, ao_ref, convw_ref, gbias_ref, ng_ref, out_ref,
                 ext_ref, ct_ref, m_ref, *, tile):
    L = tile
    H, DK, DV = MLSTM_HEADS, MLSTM_QK_DIM, MLSTM_V_DIM

    u = qk_ref[...]
    ext_ref[SUBLANES:SUBLANES + L, :] = u
    conv = jnp.zeros((L, A_QK), F32)
    for j in range(MLSTM_CONV):
        conv = conv + ext_ref[pl.ds(SUBLANES - (MLSTM_CONV - 1) + j, L), :] * convw_ref[j:j + 1, :]
    ext_ref[0:SUBLANES, :] = u[L - SUBLANES:L, :]
    qk = _silu(conv)

    g_col = aif_ref[...] + gbias_ref[...]
    g_row = g_col.T
    row = lax.broadcasted_iota(jnp.int32, (L, L), 0)
    col = lax.broadcasted_iota(jnp.int32, (L, L), 1)
    causal = row >= col
    tril = jnp.where(causal, 1.0, 0.0).astype(F32)
    triu = jnp.where(row <= col, 1.0, 0.0).astype(F32)
    b_col = jnp.dot(tril, _log_sigmoid(g_col), preferred_element_type=F32, precision=lax.Precision.HIGHEST)
    b_row = jnp.dot(_log_sigmoid(g_row), triu, preferred_element_type=F32, precision=lax.Precision.HIGHEST)

    ones_v = jnp.ones((L, DV), BF16)
    ones_t = jnp.ones((DV, L), F32)
    for h in range(H):
        bcol = b_col[:, H + h:H + h + 1]
        icol = g_col[:, h:h + 1]
        brow = b_row[H + h:H + h + 1, :]
        irow = g_row[h:h + 1, :]
        mprev = m_ref[h:h + 1, 0:1]
        dlog = jnp.where(causal, bcol - brow + irow, -jnp.inf)
        m_t = jnp.maximum(bcol + mprev, jnp.max(dlog, axis=1, keepdims=True))
        q = qk[:, h * DK:(h + 1) * DK].astype(BF16)
        k = qk[:, H * DK + h * DK:H * DK + (h + 1) * DK] * (DK ** -0.5)
        s_w = _dot_nt(q, k.astype(BF16)) * jnp.exp(dlog - m_t)
        inter = jnp.exp(bcol + mprev - m_t)
        v_h = av_ref[:, h * DV:(h + 1) * DV]
        v_ext = jnp.concatenate([v_h, ones_v], axis=1)
        ct_st = ct_ref[h]
        ne = _dot(s_w.astype(BF16), v_ext) + inter * _dot_nt(q, ct_st.astype(BF16))
        h_out = ne[:, :DV] / jnp.maximum(jnp.abs(ne[:, DV:]), jnp.exp(-m_t))
        blast = bcol[L - 1:L, :]
        glog = blast - bcol + icol
        m_new = jnp.maximum(blast + mprev, jnp.max(glog, axis=0, keepdims=True))
        decay = jnp.exp(blast + mprev - m_new)
        w_k = k * jnp.exp(glog - m_new)
        v_ext_t = jnp.concatenate([v_h.astype(F32).T, ones_t], axis=0).astype(BF16)
        ct_ref[h] = decay * ct_st + _dot(v_ext_t, w_k.astype(BF16))
        m_ref[h:h + 1, :] = jnp.broadcast_to(m_new, (1, LANES))
        mu = jnp.mean(h_out, axis=-1, keepdims=True)
        d = h_out - mu
        hn = d * lax.rsqrt(jnp.mean(d * d, axis=-1, keepdims=True) + LN_EPS)
        y = hn * ng_ref[:, h * DV:(h + 1) * DV] * jax.nn.sigmoid(ao_ref[:, h * DV:(h + 1) * DV].astype(F32))
        out_ref[:, h * DV:(h + 1) * DV] = y.astype(BF16)


def _mlstm(qk, av, aif, ao, conv_w, gate_bias, norm_g, bsz, seq):
    tile = min(MLSTM_TILE, seq)
    chunks = min(MLSTM_CHUNKS_PER_STEP, seq // tile)
    nc = seq // (tile * chunks)
    gbias = jnp.zeros((1, LANES), F32).at[0, :A_GATE].set(gate_bias.reshape(-1))
    tok = lambda w: pl.BlockSpec((tile * chunks, w), lambda b, c: (b * nc + c, 0))
    full = lambda shape: pl.BlockSpec(shape, lambda b, c: (0,) * len(shape))
    return pl.pallas_call(
        functools.partial(_mlstm_kernel, tile=tile, chunks=chunks),
        grid=(bsz, nc),
        in_specs=[tok(A_QK), tok(A_V), tok(LANES), tok(A_V),
                  full((MLSTM_CONV, A_QK)), full((1, LANES)), full((1, A_V))],
        out_specs=tok(A_V),
        out_shape=jax.ShapeDtypeStruct((bsz * seq, A_V), BF16),
        scratch_shapes=[pltpu.VMEM((tile + SUBLANES, A_QK), F32),
                        pltpu.VMEM((MLSTM_HEADS, 2 * MLSTM_V_DIM, MLSTM_QK_DIM), F32),
                        pltpu.VMEM((SUBLANES, LANES), F32)],
        compiler_params=_cparams(("parallel", "arbitrary")),
        name="mlstm",
    )(qk, av, aif, ao, conv_w, gbias, norm_g.reshape(1, A_V))


def _t5_bucket_np(dist):
    dist = np.maximum(dist, 0)
    exact = REL_BUCKETS // 2
    log_ratio = (np.log(np.maximum(dist, 1).astype(np.float32) / np.float32(exact))
                 / np.float32(math.log(REL_MAX_DIST / exact))).astype(np.float32)
    large = np.minimum(exact + (log_ratio * np.float32(REL_BUCKETS - exact)).astype(np.int32), REL_BUCKETS - 1)
    return np.where(dist < exact, dist, large)


def _near_bias_tables(rel_bias, tq, kn, kf):
    n_off = kf // tq + 1
    assert np.all(_t5_bucket_np(np.arange(tq + 1, 4 * REL_MAX_DIST)) == REL_BUCKETS - 1)
    d_max = (n_off - 1) * tq + tq - 1
    dist = np.arange(d_max, -kn, -1)
    by_dist = (rel_bias[_t5_bucket_np(dist)] - rel_bias[REL_BUCKETS - 1]) * LOG2E
    by_dist = jnp.where((dist >= 0)[:, None], by_dist, 0.0).T.astype(F32)
    rows = [[by_dist[:, d_max - (o * tq + t):d_max - (o * tq + t) + kn] for t in range(tq)] for o in range(n_off)]
    tab = jnp.stack([jnp.stack(r, axis=1) for r in rows], axis=0)
    return tab.reshape(n_off, DSA_HEADS * tq, kn)


def _dsa_kernel(bq_ref, iq_ref, ikwq_ref, ikw_ref, ckv_ref, tab_ref, wuk_ref, wuv_ref, out_ref,
                key_ref, keyt_ref, smax_ref, smin_ref, npos_ref, nnon_ref, acc_ref, m_ref, ql_ref, wb_ref,
                *, tq, kn, kf, topk):
    H = DSA_HEADS
    j = pl.program_id(1)
    t0 = j * tq
    n_score = (t0 + tq - 1) // kf + 1
    n_far = jnp.maximum(t0 - tq, 0) // kf
    near_lo = n_far * (kf // kn)
    near_hi = (t0 + tq - 1) // kn + 1

    ql_all = (_dot(bq_ref[...], wuk_ref[...]) * (DSA_HEAD_DIM ** -0.5 * LOG2E)).astype(BF16)
    for h in range(H):
        ql_ref[h * tq:(h + 1) * tq, :] = ql_all[:, h * DSA_LATENT:(h + 1) * DSA_LATENT]
    w_idx = ikwq_ref[:, IKW_W_OFF:IKW_W_OFF + IDX_HEADS] * ((IDX_HEADS * IDX_DIM) ** -0.5)
    for h in range(IDX_HEADS):
        wb_ref[h] = jnp.broadcast_to(w_idx[:, h:h + 1], (tq, LANES))
    iq = iq_ref[...]
    qis = [iq[:, h * IDX_DIM:(h + 1) * IDX_DIM] for h in range(IDX_HEADS)]
    tpos = t0 + lax.broadcasted_iota(jnp.int32, (tq, kf), 0)
    lpos = lax.broadcasted_iota(jnp.int32, (tq, kf), 1)

    ncol_f = kf // LANES
    tile_f = lambda a: jnp.concatenate([a] * ncol_f, axis=1)

    flip = lambda a: a ^ ((a >> 31) & 0x7FFFFFFF)
    to_key = lambda v: flip(lax.bitcast_convert_type(v, jnp.int32))
    to_val = lambda k: lax.bitcast_convert_type(flip(k), F32)
    fold = lambda a: a.reshape(kf // SUBLANES, SUBLANES, tq)

    smax_ref[...] = jnp.full(smax_ref.shape, -jnp.inf, F32)
    smin_ref[...] = jnp.full(smin_ref.shape, jnp.inf, F32)
    npos_ref[...] = jnp.zeros_like(npos_ref)
    nnon_ref[...] = jnp.zeros_like(nnon_ref)

    def score_chunk(c, masked):
        s0 = pl.multiple_of(c * kf, kf)
        kidx = ikw_ref[pl.ds(s0, kf), :][:, 0:IDX_DIM].astype(BF16)
        parts = [jnp.zeros((tq, LANES), F32)] * ncol_f
        for h in range(IDX_HEADS):
            z = jnp.maximum(_dot_nt(qis[h], kidx), 0.0)
            w_h = wb_ref[h]
            parts = [p + w_h * z[:, g * LANES:(g + 1) * LANES] for g, p in enumerate(parts)]
        sc = jnp.concatenate(parts, axis=1)
        sc = jnp.where(jnp.abs(sc) < F32_MIN_NORMAL, 0.0, sc)
        key = to_key(sc)
        sc_t = sc.T
        sc_t_lo = sc_t
        if masked:
            causal = s0 + lpos <= tpos
            key = jnp.where(causal, key, INT_MIN)
            sc_t = jnp.where(causal, sc, -jnp.inf).T
            sc_t_lo = jnp.where(causal, sc, jnp.inf).T
        key_ref[:, pl.ds(s0, kf)] = key
        keyt_ref[pl.ds(s0, kf), :] = to_key(sc_t)
        smax_ref[...] = jnp.maximum(smax_ref[...], jnp.max(fold(sc_t), axis=0))
        smin_ref[...] = jnp.minimum(smin_ref[...], jnp.min(fold(sc_t_lo), axis=0))
        npos_ref[...] = npos_ref[...] + jnp.sum(fold(jnp.where(sc_t > 0.0, 1.0, 0.0)), axis=0)
        nnon_ref[...] = nnon_ref[...] + jnp.sum(fold(jnp.where(sc_t >= 0.0, 1.0, 0.0)), axis=0)

    def score_run(first, count):
        for u in range(count):
            score_chunk(first + u, False)

    def score_quad(i, carry):
        score_run(4 * i, 4)
        return carry

    n_plain = n_score - 1
    lax.fori_loop(0, n_plain // 4, score_quad, 0)

    @pl.when(n_plain % 4 >= 2)
    def _():
        score_run((n_plain // 4) * 4, 2)

    @pl.when(n_plain % 2 == 1)
    def _():
        score_run(n_plain - 1, 1)

    score_chunk(n_score - 1, True)

    per_query = lambda a, op: jnp.broadcast_to(op(a, axis=0, keepdims=True), (SUBLANES, tq))
    n_acc = 4

    def count_ge(cand):
        def body(c, acc):
            s0 = pl.multiple_of(c * kf, kf)
            ind = jnp.where(fold(keyt_ref[pl.ds(s0, kf), :]) >= cand[None], 1.0, 0.0)
            return acc + jnp.sum(ind.reshape(kf // SUBLANES // n_acc, n_acc, SUBLANES, tq), axis=0)
        acc = lax.fori_loop(0, n_score, body, jnp.zeros((n_acc, SUBLANES, tq), F32))
        return per_query(jnp.sum(acc, axis=0), jnp.sum)

    n_valid = (t0 + 1 + lax.broadcasted_iota(jnp.int32, (SUBLANES, tq), 1)).astype(F32)
    enough = n_valid >= topk
    lo0 = to_key(per_query(smin_ref[...], jnp.min))
    hi0 = to_key(per_query(smax_ref[...], jnp.max)) + 1
    n_pos = per_query(npos_ref[...], jnp.sum)
    n_non = per_query(nnon_ref[...], jnp.sum)
    pos_thr = n_pos >= topk
    zero_thr = jnp.logical_and(n_non >= topk, jnp.logical_not(pos_thr))
    cnt0 = jnp.where(pos_thr, n_pos, jnp.where(zero_thr, n_non, n_valid))
    lo0 = jnp.where(pos_thr, 1, jnp.where(zero_thr, 0, lo0))
    hi0 = jnp.where(pos_thr, jnp.where(n_pos == topk, 2, hi0), jnp.where(zero_thr, 1, 0))

    def probe(mid, lo, hi, cnt_lo):
        cnt = count_ge(mid)
        ge = cnt >= topk
        return (jnp.where(ge, mid, lo), jnp.where(cnt == topk, mid + 1, jnp.where(ge, hi, mid)),
                jnp.where(ge, cnt, cnt_lo))

    def by_value(i, state):
        lo, hi, cnt_lo = state
        mid = to_key(0.5 * to_val(lo) + 0.5 * to_val(hi))
        return probe(jnp.where(jnp.logical_and(mid > lo, mid < hi), mid, lo), lo, hi, cnt_lo)

    state = lax.fori_loop(0, VALUE_PASSES, by_value, (lo0, hi0, cnt0))

    def any_open(lo, hi):
        return jnp.max(jnp.where(jnp.logical_and(enough, hi > lo + 1), 1.0, 0.0))

    def by_key(state):
        lo, hi, cnt_lo, _ = state
        lo, hi, cnt_lo = probe((lo >> 1) + (hi >> 1) + (lo & hi & 1), lo, hi, cnt_lo)
        return lo, hi, cnt_lo, any_open(lo, hi)

    lo, _, cnt_lo, _ = lax.while_loop(lambda s: s[3] > 0.0, by_key, state + (any_open(state[0], state[1]),))
    thr_q = jnp.where(enough, lo, INT_MIN + 1)
    thr = jnp.broadcast_to(thr_q[0:1, :], (LANES, tq)).T

    def count_key(pred):
        def body(c, acc):
            s0 = pl.multiple_of(c * kf, kf)
            ind = jnp.where(pred(key_ref[:, pl.ds(s0, kf)]), 1.0, 0.0)
            for g in range(ncol_f):
                acc = acc + ind[:, g * LANES:(g + 1) * LANES]
            return acc
        acc = lax.fori_loop(0, n_score, body, jnp.zeros((tq, LANES), F32))
        return jnp.sum(acc, axis=1, keepdims=True)

    thr_f = tile_f(thr)
    surplus = jnp.where(enough, cnt_lo - topk, 0.0)

    @pl.when(jnp.max(surplus) > 0.0)
    def _():
        allowed = topk - count_key(lambda k: k > thr_f)
        r = lax.broadcasted_iota(jnp.int32, (kf, kf), 0)
        cidx = lax.broadcasted_iota(jnp.int32, (kf, kf), 1)
        upper = jnp.where(r <= cidx, 1.0, 0.0).astype(BF16)

        def body(c, seen):
            s0 = pl.multiple_of(c * kf, kf)
            key = key_ref[:, pl.ds(s0, kf)]
            tie = key == thr_f
            tie_f = jnp.where(tie, 1.0, 0.0)
            rank = _dot(tie_f.astype(BF16), upper) + seen
            key_ref[:, pl.ds(s0, kf)] = jnp.where(jnp.logical_and(tie, rank > allowed), thr_f - 1, key)
            return seen + jnp.sum(tie_f, axis=1, keepdims=True)
        lax.fori_loop(0, n_score, body, jnp.zeros((tq, 1), F32))

    m_ref[...] = jnp.full(m_ref.shape, M_FLOOR, F32)
    acc_ref[...] = jnp.zeros_like(acc_ref)

    def attend(s0, width, table_idx):
        ncol = width // LANES
        kv = ckv_ref[pl.ds(s0, width), :]
        kv_ext = jnp.concatenate([kv, jnp.ones((width, DSA_LATENT), BF16)], axis=1)
        thr_w = jnp.concatenate([thr] * ncol, axis=1)
        madd = jnp.where(key_ref[:, pl.ds(s0, width)] >= thr_w, 0.0, NEG_BIG)
        for h in range(H):
            rows = slice(h * tq, (h + 1) * tq)
            x = _dot_nt(ql_ref[rows, :], kv) + madd
            if table_idx is not None:
                x = x + tab_ref[table_idx, rows, :]
            m_old = m_ref[rows, :]
            m_new = jnp.maximum(m_old, jnp.max(x, axis=1, keepdims=True))
            alpha = jnp.exp2(m_old - m_new)
            p = jnp.exp2(x - jnp.concatenate([m_new] * ncol, axis=1))
            acc_ref[rows, :] = jnp.concatenate([alpha, alpha], axis=1) * acc_ref[rows, :] + _dot(p.astype(BF16), kv_ext)
            m_ref[rows, :] = m_new

    def far_run(first, count):
        for u in range(count):
            attend(pl.multiple_of((first + u) * kf, kf), kf, None)

    def far_quad(i, carry):
        far_run(4 * i, 4)
        return carry

    def near_run(first, count):
        for u in range(count):
            s0 = pl.multiple_of((first + u) * kn, kn)
            attend(s0, kn, (t0 - s0) // tq)

    def near_pair(i, carry):
        near_run(near_lo + 2 * i, 2)
        return carry

    lax.fori_loop(0, n_far // 4, far_quad, 0)

    @pl.when(n_far % 4 >= 2)
    def _():
        far_run((n_far // 4) * 4, 2)

    @pl.when(n_far % 2 == 1)
    def _():
        far_run(n_far - 1, 1)

    lax.fori_loop(0, (near_hi - near_lo) // 2, near_pair, 0)

    @pl.when((near_hi - near_lo) % 2 == 1)
    def _():
        near_run(near_hi - 1, 1)

    outs = []
    for h in range(H):
        a = acc_ref[h * tq:(h + 1) * tq, :]
        outs.append((a[:, :DSA_LATENT] / a[:, DSA_LATENT:]).astype(BF16))
    out_ref[...] = _dot(jnp.concatenate(outs, axis=1), wuv_ref[...]).astype(BF16)


def _dsa_tiles(seq):
    tq = min(DSA_TQ, seq)
    return tq, min(DSA_KN, seq), min(DSA_KF, seq)


def _dsa(bq, iq, ikw, ckv, tab, wuk_pad, wuv_pad, bsz, seq):
    tq, kn, kf = _dsa_tiles(seq)
    nq = seq // tq
    topk = min(IDX_TOPK_MAX, seq // 4)
    n_off = kf // tq + 1
    qblk = lambda w: pl.BlockSpec((tq, w), lambda b, j: (b * nq + j, 0))
    seqblk = lambda w: pl.BlockSpec((seq, w), lambda b, j: (b, 0))
    return pl.pallas_call(
        functools.partial(_dsa_kernel, tq=tq, kn=kn, kf=kf, topk=topk),
        grid=(bsz, nq),
        in_specs=[qblk(B_Q), qblk(I_Q), qblk(LANES), seqblk(LANES), seqblk(DSA_LATENT),
                  _resident((n_off, DSA_HEADS * tq, kn)), _resident((B_Q, DSA_HEADS * DSA_LATENT)),
                  _resident((DSA_HEADS * DSA_LATENT, B_Q))],
        out_specs=qblk(B_Q),
        out_shape=jax.ShapeDtypeStruct((bsz * seq, B_Q), BF16),
        scratch_shapes=[pltpu.VMEM((tq, seq), jnp.int32),
                        pltpu.VMEM((seq, tq), jnp.int32),
                        pltpu.VMEM((SUBLANES, tq), F32),
                        pltpu.VMEM((SUBLANES, tq), F32),
                        pltpu.VMEM((SUBLANES, tq), F32),
                        pltpu.VMEM((SUBLANES, tq), F32),
                        pltpu.VMEM((DSA_HEADS * tq, 2 * DSA_LATENT), F32),
                        pltpu.VMEM((DSA_HEADS * tq, LANES), F32),
                        pltpu.VMEM((DSA_HEADS * tq, DSA_LATENT), BF16),
                        pltpu.VMEM((IDX_HEADS, tq, LANES), F32)],
        compiler_params=_cparams(("parallel", "arbitrary")),
        name="dsa",
    )(bq, iq, ikw, ikw, ckv, tab, wuk_pad, wuv_pad)


def _kv_norm_kernel(bc_ref, g_ref, out_ref):
    x = bc_ref[...]
    out_ref[...] = (x * lax.rsqrt(jnp.mean(x * x, axis=-1, keepdims=True) + LN_EPS) * g_ref[...]).astype(BF16)


def _kv_norm(bc, g):
    n = bc.shape[0]
    tm = min(2 * TOK_TILE, n)
    return pl.pallas_call(
        _kv_norm_kernel,
        grid=(n // tm,),
        in_specs=[pl.BlockSpec((tm, DSA_LATENT), lambda i: (i, 0)), pl.BlockSpec((1, DSA_LATENT), lambda i: (0, 0))],
        out_specs=pl.BlockSpec((tm, DSA_LATENT), lambda i: (i, 0)),
        out_shape=jax.ShapeDtypeStruct((n, DSA_LATENT), BF16),
        compiler_params=_cparams(("parallel",)),
        name="kv_norm",
    )(bc, g.reshape(1, DSA_LATENT))


def _merge_kernel(x_ref, ya_ref, yb_ref, ga_ref, gb_ref, wa_ref, wb_ref, wo_ref, g_ref, b_ref, out_ref):
    merged = (jax.nn.sigmoid(ga_ref[...].astype(F32)) * _dot(ya_ref[...], wa_ref[...])
              + jax.nn.sigmoid(gb_ref[...].astype(F32)) * _dot(yb_ref[...], wb_ref[...]))
    mix = _dot(merged.astype(BF16), wo_ref[...])
    out_ref[...] = _layer_norm(DN_ALPHA * x_ref[...] + mix, g_ref[...], b_ref[...])


def _merge(x2, ya, yb, ga, gb, wa, wb, wo, g, b):
    n = x2.shape[0]
    tm = min(TOK_TILE, n)
    tok = lambda w: pl.BlockSpec((tm, w), lambda i: (i, 0))
    full = _resident
    return pl.pallas_call(
        _merge_kernel,
        grid=(n // tm,),
        in_specs=[tok(D_MODEL), tok(A_V), tok(B_Q), tok(D_MODEL), tok(D_MODEL),
                  full((A_V, D_MODEL)), full((B_Q, D_MODEL)), full((D_MODEL, D_MODEL)),
                  full((1, D_MODEL)), full((1, D_MODEL))],
        out_specs=tok(D_MODEL),
        out_shape=jax.ShapeDtypeStruct((n, D_MODEL), F32),
        compiler_params=_cparams(("parallel",)),
        name="merge_ln",
    )(x2, ya, yb, ga, gb, wa, wb, wo, g.reshape(1, D_MODEL), b.reshape(1, D_MODEL))


def _dense_ffn_kernel(x_ref, wg_ref, wu_ref, wd_ref, g_ref, b_ref, out_ref):
    x = x_ref[...]
    xb = x.astype(BF16)
    f = jnp.zeros(x.shape, F32)
    for c in range(D_FF // FF_CHUNK):
        cols = slice(c * FF_CHUNK, (c + 1) * FF_CHUNK)
        hdn = _silu(_dot(xb, wg_ref[:, cols])) * _dot(xb, wu_ref[:, cols])
        f = f + _dot(hdn.astype(BF16), wd_ref[cols, :])
    out_ref[...] = _layer_norm(DN_ALPHA * x + f, g_ref[...], b_ref[...])


def _dense_ffn(x2, wg, wu, wd, g, b):
    n = x2.shape[0]
    tm = min(TOK_TILE, n)
    tok = pl.BlockSpec((tm, D_MODEL), lambda i: (i, 0))
    full = _resident
    return pl.pallas_call(
        _dense_ffn_kernel,
        grid=(n // tm,),
        in_specs=[tok, full((D_MODEL, D_FF)), full((D_MODEL, D_FF)), full((D_FF, D_MODEL)),
                  full((1, D_MODEL)), full((1, D_MODEL))],
        out_specs=tok,
        out_shape=jax.ShapeDtypeStruct((n, D_MODEL), F32),
        compiler_params=_cparams(("parallel",)),
        name="dense_ffn_ln",
    )(x2, wg, wu, wd, g.reshape(1, D_MODEL), b.reshape(1, D_MODEL))


def _router_kernel(x_ref, rw_ref, e_ref, g_ref):
    logits = lax.dot_general(rw_ref[...], x_ref[...], NT_DIMS, preferred_element_type=F32,
                             precision=lax.Precision.HIGHEST)
    eidx = lax.broadcasted_iota(jnp.int32, logits.shape, 0)
    m1 = jnp.max(logits, axis=0, keepdims=True)
    i1 = jnp.min(jnp.where(logits == m1, eidx, N_EXPERTS), axis=0, keepdims=True)
    rest = jnp.where(eidx == i1, -jnp.inf, logits)
    m2 = jnp.max(rest, axis=0, keepdims=True)
    i2 = jnp.min(jnp.where(rest == m2, eidx, N_EXPERTS), axis=0, keepdims=True)
    e2 = jnp.exp(m2 - m1)
    den = 1.0 + e2
    e_ref[0:1, :] = i1
    e_ref[1:2, :] = i2
    g_ref[0:1, :] = 1.0 / den
    g_ref[1:2, :] = e2 / den


def _router(x2, rw_t):
    n = x2.shape[0]
    tm = min(TOK_TILE, n)
    return pl.pallas_call(
        _router_kernel,
        grid=(n // tm,),
        in_specs=[pl.BlockSpec((tm, D_MODEL), lambda i: (i, 0)), pl.BlockSpec((N_EXPERTS, D_MODEL), lambda i: (0, 0))],
        out_specs=[pl.BlockSpec((TOP_K, tm), lambda i: (0, i)), pl.BlockSpec((TOP_K, tm), lambda i: (0, i))],
        out_shape=[jax.ShapeDtypeStruct((TOP_K, n), jnp.int32), jax.ShapeDtypeStruct((TOP_K, n), F32)],
        compiler_params=_cparams(("parallel",)),
        name="router",
    )(x2, rw_t)


def _row_copy(src, src_row, dst, dst_row, sem):
    return pltpu.make_async_copy(src.at[pl.ds(src_row, 1)], dst.at[pl.ds(dst_row, 1)], sem)


def _dispatch_kernel(d0_ref, d1_ref, x_ref, xs_init_hbm, xs_hbm, sem, *, tile):
    del xs_init_hbm

    def start(r, c):
        _row_copy(x_ref, r, xs_hbm, d0_ref[0, r], sem).start(priority=0)
        _row_copy(x_ref, r, xs_hbm, d1_ref[0, r], sem).start(priority=1)
        return c
    lax.fori_loop(0, tile, start, 0, unroll=DMA_UNROLL)

    def wait(r, c):
        _row_copy(x_ref, r, xs_hbm, 0, sem).wait()
        _row_copy(x_ref, r, xs_hbm, 0, sem).wait()
        return c
    lax.fori_loop(0, tile, wait, 0, unroll=DMA_UNROLL)


def _dispatch(x2, pos, total):
    n = x2.shape[0]
    tile = min(COMB_TILE, n)
    nb = n // tile
    smem = pl.BlockSpec((None, 1, tile), lambda i: (i, 0, 0), memory_space=pltpu.SMEM)
    return pl.pallas_call(
        functools.partial(_dispatch_kernel, tile=tile),
        grid=(nb,),
        in_specs=[smem, smem, pl.BlockSpec((tile, D_MODEL), lambda i: (i, 0)), pl.BlockSpec(memory_space=pl.ANY)],
        out_specs=pl.BlockSpec(memory_space=pl.ANY),
        out_shape=jax.ShapeDtypeStruct((total, D_MODEL), F32),
        scratch_shapes=[pltpu.SemaphoreType.DMA(())],
        input_output_aliases={3: 0},
        compiler_params=_cparams(("arbitrary",)),
        name="moe_dispatch",
    )(pos[:, 0].reshape(nb, 1, tile), pos[:, 1].reshape(nb, 1, tile), x2, jnp.zeros((total, D_MODEL), F32))


def _expert_kernel(blk_e_ref, nused_ref, xs_ref, wg_ref, wu_ref, wd_ref, y_ref, xb, acc):
    i = pl.program_id(0)
    f = pl.program_id(1)
    used = i < nused_ref[0]
    last = f == pl.num_programs(1) - 1

    @pl.when(jnp.logical_and(used, f == 0))
    def _():
        xb[...] = xs_ref[...].astype(BF16)
        acc[...] = jnp.zeros_like(acc)

    @pl.when(used)
    def _():
        x = xb[...]
        hdn = _silu(_dot(x, wg_ref[...])) * _dot(x, wu_ref[...])
        acc[...] += _dot(hdn.astype(BF16), wd_ref[...])

    @pl.when(jnp.logical_and(used, last))
    def _():
        y_ref[...] = acc[...]

    @pl.when(jnp.logical_and(jnp.logical_not(used), last))
    def _():
        y_ref[...] = jnp.zeros_like(y_ref)


def _experts(xs, blk_e, nused, wg, wu, wd):
    total = xs.shape[0]
    tile = MOE_TILE
    nblk = total // tile
    nf = D_FF_EXPERT // MOE_FT
    grid_spec = pltpu.PrefetchScalarGridSpec(
        num_scalar_prefetch=2,
        grid=(nblk, nf),
        in_specs=[pl.BlockSpec((tile, D_MODEL), lambda i, f, be, nu: (i, 0)),
                  pl.BlockSpec((None, D_MODEL, MOE_FT), lambda i, f, be, nu: (be[i], 0, f)),
                  pl.BlockSpec((None, D_MODEL, MOE_FT), lambda i, f, be, nu: (be[i], 0, f)),
                  pl.BlockSpec((None, MOE_FT, D_MODEL), lambda i, f, be, nu: (be[i], f, 0))],
        out_specs=pl.BlockSpec((tile, D_MODEL), lambda i, f, be, nu: (i, 0)),
        scratch_shapes=[pltpu.VMEM((tile, D_MODEL), BF16), pltpu.VMEM((tile, D_MODEL), F32)],
    )
    return pl.pallas_call(
        _expert_kernel,
        grid_spec=grid_spec,
        out_shape=jax.ShapeDtypeStruct((total, D_MODEL), F32),
        compiler_params=_cparams(("arbitrary", "arbitrary")),
        name="experts",
    )(blk_e, nused, xs, wg, wu, wd)


def _combine_kernel(p0_ref, p1_ref, y_hbm, x_ref, gate_ref, g_ref, b_ref, out_ref, buf0, buf1, sem, *, tile):
    def start(r, c):
        _row_copy(y_hbm, p0_ref[0, r], buf0, r, sem).start(priority=0)
        _row_copy(y_hbm, p1_ref[0, r], buf1, r, sem).start(priority=1)
        return c
    lax.fori_loop(0, tile, start, 0, unroll=DMA_UNROLL)

    def wait(r, c):
        _row_copy(y_hbm, 0, buf0, r, sem).wait()
        _row_copy(y_hbm, 0, buf1, r, sem).wait()
        return c
    lax.fori_loop(0, tile, wait, 0, unroll=DMA_UNROLL)
    f = buf0[...] * gate_ref[:, 0:1] + buf1[...] * gate_ref[:, 1:2]
    out_ref[...] = _layer_norm(DN_ALPHA * x_ref[...] + f, g_ref[...], b_ref[...])


def _combine(y, pos, gates, x2, g, b):
    n = x2.shape[0]
    tile = min(COMB_TILE, n)
    nb = n // tile
    smem = pl.BlockSpec((None, 1, tile), lambda i: (i, 0, 0), memory_space=pltpu.SMEM)
    full = pl.BlockSpec((1, D_MODEL), lambda i: (0, 0))
    return pl.pallas_call(
        functools.partial(_combine_kernel, tile=tile),
        grid=(nb,),
        in_specs=[smem, smem, pl.BlockSpec(memory_space=pl.ANY),
                  pl.BlockSpec((tile, D_MODEL), lambda i: (i, 0)),
                  pl.BlockSpec((tile, TOP_K), lambda i: (i, 0)), full, full],
        out_specs=pl.BlockSpec((tile, D_MODEL), lambda i: (i, 0)),
        out_shape=jax.ShapeDtypeStruct((n, D_MODEL), F32),
        scratch_shapes=[pltpu.VMEM((tile, D_MODEL), F32), pltpu.VMEM((tile, D_MODEL), F32),
                        pltpu.SemaphoreType.DMA(())],
        compiler_params=_cparams(("arbitrary",)),
        name="moe_combine_ln",
    )(pos[:, 0].reshape(nb, 1, tile), pos[:, 1].reshape(nb, 1, tile), y, x2, gates,
      g.reshape(1, D_MODEL), b.reshape(1, D_MODEL))


def _moe(x2, router_w, wg, wu, wd, g, b):
    n = x2.shape[0]
    top_e, gates = _router(x2, router_w.T)
    flat_e = top_e.T.reshape(-1)
    onehot = (flat_e[:, None] == jnp.arange(N_EXPERTS, dtype=jnp.int32)[None, :]).astype(jnp.int32)
    rank = jnp.sum((jnp.cumsum(onehot, axis=0) - onehot) * onehot, axis=1)
    counts = jnp.sum(onehot, axis=0)
    padded = ((counts + MOE_TILE - 1) // MOE_TILE) * MOE_TILE
    pad_end = jnp.cumsum(padded)
    pad_start = pad_end - padded
    pos = (pad_start[flat_e] + rank).astype(jnp.int32).reshape(n, TOP_K)
    total = n * TOP_K + N_EXPERTS * MOE_TILE
    nblk = total // MOE_TILE
    blk_e = jnp.minimum(jnp.searchsorted(pad_end, jnp.arange(nblk, dtype=jnp.int32) * MOE_TILE, side='right'),
                        N_EXPERTS - 1).astype(jnp.int32)
    nused = (pad_end[-1:] // MOE_TILE).astype(jnp.int32)
    xs = _dispatch(x2, pos, total)
    y = _experts(xs, blk_e, nused, wg, wu, wd)
    return _combine(y, pos, gates.T, x2, g, b)


def _pad_head_proj(w_uk, w_uv):
    H, DH, DC = DSA_HEADS, DSA_HEAD_DIM, DSA_LATENT
    eye = jnp.eye(H, dtype=w_uk.dtype)
    wuk_bd = jnp.einsum('hg,hdc->hdgc', eye, w_uk).reshape(H * DH, H * DC)
    wuv_bd = jnp.einsum('hg,hcd->hcgd', eye, w_uv).reshape(H * DC, H * DH)
    return wuk_bd.astype(BF16), wuv_bd.astype(BF16)


def kernel(x, w_in, mlstm_conv_w, mlstm_gate_bias, mlstm_norm_g, dsa_kv_norm_g, dsa_w_uk, dsa_w_uv, rel_bias,
           w_branch_a, w_branch_b, w_out, ln_g, ln_b, dense_w_gate, dense_w_up, dense_w_down, router_w,
           expert_w_gate, expert_w_up, expert_w_down):
    bsz, seq, d = x.shape
    n = bsz * seq
    x2 = x.reshape(n, d)
    tab = _near_bias_tables(rel_bias, *_dsa_tiles(seq))
    for l in range(DEPTH):
        p = dict(zip([nm for nm, _, _ in IN_PROJ_OUTS], _in_proj(x2, _regroup_in_proj(w_in[l]))))
        ya = _mlstm(p["qk"], p["av"], p["aif"], p["ao"], mlstm_conv_w[l], mlstm_gate_bias[l], mlstm_norm_g[l],
                    bsz, seq)
        ckv = _kv_norm(p["bc"], dsa_kv_norm_g[l])
        wuk_pad, wuv_pad = _pad_head_proj(dsa_w_uk[l], dsa_w_uv[l])
        yb = _dsa(p["bq"], p["iq"], p["ikw"], ckv, tab, wuk_pad, wuv_pad, bsz, seq)
        x2 = _merge(x2, ya, yb, p["ga"], p["gb"], w_branch_a[l].astype(BF16), w_branch_b[l].astype(BF16),
                    w_out[l].astype(BF16), ln_g[l, 0], ln_b[l, 0])
        j = l // 2
        if l % 2 == 0:
            x2 = _dense_ffn(x2, dense_w_gate[j].astype(BF16), dense_w_up[j].astype(BF16),
                            dense_w_down[j].astype(BF16), ln_g[l, 1], ln_b[l, 1])
        else:
            x2 = _moe(x2, router_w[j], expert_w_gate[j].astype(BF16), expert_w_up[j].astype(BF16),
                      expert_w_down[j].astype(BF16), ln_g[l, 1], ln_b[l, 1])
    return x2.reshape(bsz, seq, d)
```

```python
import functools
import math

import numpy as np
import jax
import jax.numpy as jnp
from jax import lax
from jax.experimental import pallas as pl
from jax.experimental.pallas import tpu as pltpu

D_MODEL = 1024
DEPTH = 4
MLSTM_HEADS = 4
MLSTM_QK_DIM = 64
MLSTM_V_DIM = 128
MLSTM_CONV = 4
DSA_HEADS = 8
DSA_HEAD_DIM = 64
DSA_LATENT = 128
IDX_HEADS = 8
IDX_DIM = 32
IDX_TOPK_MAX = 256
REL_BUCKETS = 32
REL_MAX_DIST = 128
D_FF = 2816
N_EXPERTS = 8
TOP_K = 2
D_FF_EXPERT = 3584
DN_ALPHA = (2 * DEPTH) ** 0.25
LN_EPS = 1e-5

A_QK = 2 * MLSTM_HEADS * MLSTM_QK_DIM
A_V = MLSTM_HEADS * MLSTM_V_DIM
A_GATE = 2 * MLSTM_HEADS
B_Q = DSA_HEADS * DSA_HEAD_DIM
I_Q = IDX_HEADS * IDX_DIM
PROJ_SIZES = (A_QK, A_V, A_V, A_GATE, B_Q, DSA_LATENT, I_Q, IDX_DIM, IDX_HEADS, D_MODEL, D_MODEL)

LANES = 128
SUBLANES = 8
VMEM_LIMIT = 56 * 1024 * 1024
TOK_TILE = 1024
MLSTM_TILE = 256
MLSTM_CHUNKS_PER_STEP = 2
DSA_TQ = 256
DSA_KN = 256
DSA_KF = 512
MOE_TILE = 512
MOE_FT = 1792
COMB_TILE = 256
DMA_UNROLL = 8
FF_CHUNK = 256
INT_MIN = -(2 ** 31)
NEG_BIG = -1e30
M_FLOOR = -1e29
LOG2E = math.log2(math.e)
F32_MIN_NORMAL = 2.0 ** -126
VALUE_PASSES = 16

BF16 = jnp.bfloat16
F32 = jnp.float32
NT_DIMS = (((1,), (1,)), ((), ()))


def _cparams(sem):
    return pltpu.CompilerParams(dimension_semantics=sem, vmem_limit_bytes=VMEM_LIMIT)


def _resident(shape):
    return pl.BlockSpec(shape, lambda *_: (0,) * len(shape), pipeline_mode=pl.Buffered(1))


def _dot(a, b):
    return jnp.dot(a, b, preferred_element_type=F32)


def _dot_nt(a, b):
    return lax.dot_general(a, b, NT_DIMS, preferred_element_type=F32)


def _layer_norm(r, g, b):
    mu = jnp.mean(r, axis=-1, keepdims=True)
    d = r - mu
    var = jnp.mean(d * d, axis=-1, keepdims=True)
    return d * lax.rsqrt(var + LN_EPS) * g + b


def _silu(x):
    return x * jax.nn.sigmoid(x)


IN_PROJ_OUTS = (("qk", A_QK, F32), ("av", A_V, BF16), ("ao", A_V, BF16), ("bq", B_Q, BF16),
                ("iq", I_Q, BF16), ("ga", D_MODEL, BF16), ("gb", D_MODEL, BF16),
                ("aif", LANES, F32), ("bc", LANES, BF16), ("ikw", LANES, F32))
IN_PROJ_COLS = sum(w for _, w, _ in IN_PROJ_OUTS)
IKW_W_OFF = IDX_DIM


def _regroup_in_proj(w):
    offs = np.concatenate([[0], np.cumsum(PROJ_SIZES)])
    a_qk, a_v, a_o, a_if, b_q, b_c, i_q, i_k, i_w, g_a, g_b = (w[:, offs[i]:offs[i + 1]] for i in range(11))
    pad = lambda a: jnp.pad(a, ((0, 0), (0, LANES - a.shape[1])))
    groups = [a_qk, a_v, a_o, b_q, i_q, g_a, g_b, pad(a_if), b_c, pad(jnp.concatenate([i_k, i_w], axis=1))]
    return jnp.concatenate(groups, axis=1).astype(BF16)


def _in_proj_kernel(x_ref, w_ref, kvg_ref, *out_refs):
    xb = x_ref[...].astype(BF16)
    off = 0
    for (name, width, dtype), o_ref in zip(IN_PROJ_OUTS, out_refs):
        y = _dot(xb, w_ref[:, off:off + width])
        if name == "bc":
            y = y * lax.rsqrt(jnp.mean(y * y, axis=-1, keepdims=True) + LN_EPS) * kvg_ref[...]
        o_ref[...] = y.astype(dtype)
        off += width


def _in_proj(x2, w, kv_norm_g):
    n = x2.shape[0]
    tm = min(TOK_TILE, n)
    return pl.pallas_call(
        _in_proj_kernel,
        grid=(n // tm,),
        in_specs=[pl.BlockSpec((tm, D_MODEL), lambda i: (i, 0)),
                  _resident((D_MODEL, IN_PROJ_COLS)), _resident((1, DSA_LATENT))],
        out_specs=[pl.BlockSpec((tm, wd), lambda i: (i, 0)) for _, wd, _ in IN_PROJ_OUTS],
        out_shape=[jax.ShapeDtypeStruct((n, wd), dt) for _, wd, dt in IN_PROJ_OUTS],
        compiler_params=_cparams(("parallel",)),
        name="in_proj",
    )(x2, w, kv_norm_g.reshape(1, DSA_LATENT))


def _log_sigmoid(x):
    return jnp.minimum(x, 0.0) - jnp.log1p(jnp.exp(-jnp.abs(x)))


def _mlstm_kernel(qk_ref, av_ref, aif_ref, ao_ref, convw_ref, gbias_ref, ng_ref, out_ref,
                  ext_ref, ct_ref, m_ref, *, tile, chunks):
    @pl.when(pl.program_id(1) == 0)
    def _():
        ext_ref[0:SUBLANES, :] = jnp.zeros((SUBLANES, A_QK), F32)
        ct_ref[...] = jnp.zeros_like(ct_ref)
        m_ref[...] = jnp.zeros_like(m_ref)

    for c in range(chunks):
        rows = pl.ds(c * tile, tile)
        _mlstm_chunk(qk_ref.at[rows], av_ref.at[rows], aif_ref.at[rows], ao_ref.at[rows], convw_ref, gbias_ref,
                     ng_ref, out_ref.at[rows], ext_ref, ct_ref, m_ref, tile=tile)


def _mlstm_chunk(qk_ref, av_ref, aif_ref, ao_ref, convw_ref, gbias_ref, ng_ref, out_ref,
                 ext_ref, ct_ref, m_ref, *, tile):
    L = tile
    H, DK, DV = MLSTM_HEADS, MLSTM_QK_DIM, MLSTM_V_DIM

    u = qk_ref[...]
    ext_ref[SUBLANES:SUBLANES + L, :] = u
    conv = jnp.zeros((L, A_QK), F32)
    for j in range(MLSTM_CONV):
        conv = conv + ext_ref[pl.ds(SUBLANES - (MLSTM_CONV - 1) + j, L), :] * convw_ref[j:j + 1, :]
    ext_ref[0:SUBLANES, :] = u[L - SUBLANES:L, :]
    qk = _silu(conv)

    g_col = aif_ref[...] + gbias_ref[...]
    g_row = g_col.T
    row = lax.broadcasted_iota(jnp.int32, (L, L), 0)
    col = lax.broadcasted_iota(jnp.int32, (L, L), 1)
    causal = row >= col
    tril = jnp.where(causal, 1.0, 0.0).astype(F32)
    triu = jnp.where(row <= col, 1.0, 0.0).astype(F32)
    b_col = jnp.dot(tril, _log_sigmoid(g_col), preferred_element_type=F32, precision=lax.Precision.HIGHEST)
    b_row = jnp.dot(_log_sigmoid(g_row), triu, preferred_element_type=F32, precision=lax.Precision.HIGHEST)

    ones_v = jnp.ones((L, DV), BF16)
    ones_t = jnp.ones((DV, L), F32)
    for h in range(H):
        bcol = b_col[:, H + h:H + h + 1]
        icol = g_col[:, h:h + 1]
        brow = b_row[H + h:H + h + 1, :]
        irow = g_row[h:h + 1, :]
        mprev = m_ref[h:h + 1, 0:1]
        dlog = jnp.where(causal, bcol - brow + irow, -jnp.inf)
        m_t = jnp.maximum(bcol + mprev, jnp.max(dlog, axis=1, keepdims=True))
        q = qk[:, h * DK:(h + 1) * DK].astype(BF16)
        k = qk[:, H * DK + h * DK:H * DK + (h + 1) * DK] * (DK ** -0.5)
        s_w = _dot_nt(q, k.astype(BF16)) * jnp.exp(dlog - m_t)
        inter = jnp.exp(bcol + mprev - m_t)
        v_h = av_ref[:, h * DV:(h + 1) * DV]
        v_ext = jnp.concatenate([v_h, ones_v], axis=1)
        ct_st = ct_ref[h]
        ne = _dot(s_w.astype(BF16), v_ext) + inter * _dot_nt(q, ct_st.astype(BF16))
        h_out = ne[:, :DV] / jnp.maximum(jnp.abs(ne[:, DV:]), jnp.exp(-m_t))
        blast = bcol[L - 1:L, :]
        glog = blast - bcol + icol
        m_new = jnp.maximum(blast + mprev, jnp.max(glog, axis=0, keepdims=True))
        decay = jnp.exp(blast + mprev - m_new)
        w_k = k * jnp.exp(glog - m_new)
        v_ext_t = jnp.concatenate([v_h.astype(F32).T, ones_t], axis=0).astype(BF16)
        ct_ref[h] = decay * ct_st + _dot(v_ext_t, w_k.astype(BF16))
        m_ref[h:h + 1, :] = jnp.broadcast_to(m_new, (1, LANES))
        mu = jnp.mean(h_out, axis=-1, keepdims=True)
        d = h_out - mu
        hn = d * lax.rsqrt(jnp.mean(d * d, axis=-1, keepdims=True) + LN_EPS)
        y = hn * ng_ref[:, h * DV:(h + 1) * DV] * jax.nn.sigmoid(ao_ref[:, h * DV:(h + 1) * DV].astype(F32))
        out_ref[:, h * DV:(h + 1) * DV] = y.astype(BF16)


def _mlstm(qk, av, aif, ao, conv_w, gate_bias, norm_g, bsz, seq):
    tile = min(MLSTM_TILE, seq)
    chunks = min(MLSTM_CHUNKS_PER_STEP, seq // tile)
    nc = seq // (tile * chunks)
    gbias = jnp.zeros((1, LANES), F32).at[0, :A_GATE].set(gate_bias.reshape(-1))
    tok = lambda w: pl.BlockSpec((tile * chunks, w), lambda b, c: (b * nc + c, 0))
    full = lambda shape: pl.BlockSpec(shape, lambda b, c: (0,) * len(shape))
    return pl.pallas_call(
        functools.partial(_mlstm_kernel, tile=tile, chunks=chunks),
        grid=(bsz, nc),
        in_specs=[tok(A_QK), tok(A_V), tok(LANES), tok(A_V),
                  full((MLSTM_CONV, A_QK)), full((1, LANES)), full((1, A_V))],
        out_specs=tok(A_V),
        out_shape=jax.ShapeDtypeStruct((bsz * seq, A_V), BF16),
        scratch_shapes=[pltpu.VMEM((tile + SUBLANES, A_QK), F32),
                        pltpu.VMEM((MLSTM_HEADS, 2 * MLSTM_V_DIM, MLSTM_QK_DIM), F32),
                        pltpu.VMEM((SUBLANES, LANES), F32)],
        compiler_params=_cparams(("parallel", "arbitrary")),
        name="mlstm",
    )(qk, av, aif, ao, conv_w, gbias, norm_g.reshape(1, A_V))


def _t5_bucket_np(dist):
    dist = np.maximum(dist, 0)
    exact = REL_BUCKETS // 2
    log_ratio = (np.log(np.maximum(dist, 1).astype(np.float32) / np.float32(exact))
                 / np.float32(math.log(REL_MAX_DIST / exact))).astype(np.float32)
    large = np.minimum(exact + (log_ratio * np.float32(REL_BUCKETS - exact)).astype(np.int32), REL_BUCKETS - 1)
    return np.where(dist < exact, dist, large)


def _near_bias_tables(rel_bias, tq, kn, kf):
    n_off = kf // tq + 1
    assert np.all(_t5_bucket_np(np.arange(tq + 1, 4 * REL_MAX_DIST)) == REL_BUCKETS - 1)
    d_max = (n_off - 1) * tq + tq - 1
    dist = np.arange(d_max, -kn, -1)
    by_dist = (rel_bias[_t5_bucket_np(dist)] - rel_bias[REL_BUCKETS - 1]) * LOG2E
    by_dist = jnp.where((dist >= 0)[:, None], by_dist, 0.0).T.astype(F32)
    rows = [[by_dist[:, d_max - (o * tq + t):d_max - (o * tq + t) + kn] for t in range(tq)] for o in range(n_off)]
    tab = jnp.stack([jnp.stack(r, axis=1) for r in rows], axis=0)
    return tab.reshape(n_off, DSA_HEADS * tq, kn)


def _dsa_kernel(bq_ref, iq_ref, ikwq_ref, ikw_ref, ckv_ref, tab_ref, wuk_ref, wuv_ref, out_ref,
                key_ref, keyt_ref, smax_ref, smin_ref, npos_ref, nnon_ref, acc_ref, m_ref, ql_ref, wb_ref,
                *, tq, kn, kf, topk):
    H = DSA_HEADS
    j = pl.program_id(1)
    t0 = j * tq
    n_score = (t0 + tq - 1) // kf + 1
    n_far = jnp.maximum(t0 - tq, 0) // kf
    near_lo = n_far * (kf // kn)
    near_hi = (t0 + tq - 1) // kn + 1

    ql_all = (_dot(bq_ref[...], wuk_ref[...]) * (DSA_HEAD_DIM ** -0.5 * LOG2E)).astype(BF16)
    for h in range(H):
        ql_ref[h * tq:(h + 1) * tq, :] = ql_all[:, h * DSA_LATENT:(h + 1) * DSA_LATENT]
    w_idx = ikwq_ref[:, IKW_W_OFF:IKW_W_OFF + IDX_HEADS] * ((IDX_HEADS * IDX_DIM) ** -0.5)
    for h in range(IDX_HEADS):
        wb_ref[h] = jnp.broadcast_to(w_idx[:, h:h + 1], (tq, LANES))
    iq = iq_ref[...]
    qis = [iq[:, h * IDX_DIM:(h + 1) * IDX_DIM] for h in range(IDX_HEADS)]
    tpos = t0 + lax.broadcasted_iota(jnp.int32, (tq, kf), 0)
    lpos = lax.broadcasted_iota(jnp.int32, (tq, kf), 1)

    ncol_f = kf // LANES
    tile_f = lambda a: jnp.concatenate([a] * ncol_f, axis=1)

    flip = lambda a: a ^ ((a >> 31) & 0x7FFFFFFF)
    to_key = lambda v: flip(lax.bitcast_convert_type(v, jnp.int32))
    to_val = lambda k: lax.bitcast_convert_type(flip(k), F32)
    fold = lambda a: a.reshape(kf // SUBLANES, SUBLANES, tq)

    smax_ref[...] = jnp.full(smax_ref.shape, -jnp.inf, F32)
    smin_ref[...] = jnp.full(smin_ref.shape, jnp.inf, F32)
    npos_ref[...] = jnp.zeros_like(npos_ref)
    nnon_ref[...] = jnp.zeros_like(nnon_ref)

    def score_chunk(c, masked):
        s0 = pl.multiple_of(c * kf, kf)
        kidx = ikw_ref[pl.ds(s0, kf), :][:, 0:IDX_DIM].astype(BF16)
        parts = [jnp.zeros((tq, LANES), F32)] * ncol_f
        for h in range(IDX_HEADS):
            z = jnp.maximum(_dot_nt(qis[h], kidx), 0.0)
            w_h = wb_ref[h]
            parts = [p + w_h * z[:, g * LANES:(g + 1) * LANES] for g, p in enumerate(parts)]
        sc = jnp.concatenate(parts, axis=1)
        sc = jnp.where(jnp.abs(sc) < F32_MIN_NORMAL, 0.0, sc)
        key = to_key(sc)
        sc_t = sc.T
        sc_t_lo = sc_t
        if masked:
            causal = s0 + lpos <= tpos
            key = jnp.where(causal, key, INT_MIN)
            sc_t = jnp.where(causal, sc, -jnp.inf).T
            sc_t_lo = jnp.where(causal, sc, jnp.inf).T
        key_ref[:, pl.ds(s0, kf)] = key
        keyt_ref[pl.ds(s0, kf), :] = to_key(sc_t)
        smax_ref[...] = jnp.maximum(smax_ref[...], jnp.max(fold(sc_t), axis=0))
        smin_ref[...] = jnp.minimum(smin_ref[...], jnp.min(fold(sc_t_lo), axis=0))
        npos_ref[...] = npos_ref[...] + jnp.sum(fold(jnp.where(sc_t > 0.0, 1.0, 0.0)), axis=0)
        nnon_ref[...] = nnon_ref[...] + jnp.sum(fold(jnp.where(sc_t >= 0.0, 1.0, 0.0)), axis=0)

    def score_run(first, count):
        for u in range(count):
            score_chunk(first + u, False)

    def score_quad(i, carry):
        score_run(4 * i, 4)
        return carry

    n_plain = n_score - 1
    lax.fori_loop(0, n_plain // 4, score_quad, 0)

    @pl.when(n_plain % 4 >= 2)
    def _():
        score_run((n_plain // 4) * 4, 2)

    @pl.when(n_plain % 2 == 1)
    def _():
        score_run(n_plain - 1, 1)

    score_chunk(n_score - 1, True)

    per_query = lambda a, op: jnp.broadcast_to(op(a, axis=0, keepdims=True), (SUBLANES, tq))
    n_acc = 4

    def count_ge(cand):
        def body(c, acc):
            s0 = pl.multiple_of(c * kf, kf)
            ind = jnp.where(fold(keyt_ref[pl.ds(s0, kf), :]) >= cand[None], 1.0, 0.0)
            return acc + jnp.sum(ind.reshape(kf // SUBLANES // n_acc, n_acc, SUBLANES, tq), axis=0)
        acc = lax.fori_loop(0, n_score, body, jnp.zeros((n_acc, SUBLANES, tq), F32))
        return per_query(jnp.sum(acc, axis=0), jnp.sum)

    n_valid = (t0 + 1 + lax.broadcasted_iota(jnp.int32, (SUBLANES, tq), 1)).astype(F32)
    enough = n_valid >= topk
    lo0 = to_key(per_query(smin_ref[...], jnp.min))
    hi0 = to_key(per_query(smax_ref[...], jnp.max)) + 1
    n_pos = per_query(npos_ref[...], jnp.sum)
    n_non = per_query(nnon_ref[...], jnp.sum)
    pos_thr = n_pos >= topk
    zero_thr = jnp.logical_and(n_non >= topk, jnp.logical_not(pos_thr))
    cnt0 = jnp.where(pos_thr, n_pos, jnp.where(zero_thr, n_non, n_valid))
    lo0 = jnp.where(pos_thr, 1, jnp.where(zero_thr, 0, lo0))
    hi0 = jnp.where(pos_thr, jnp.where(n_pos == topk, 2, hi0), jnp.where(zero_thr, 1, 0))

    def probe(mid, lo, hi, cnt_lo):
        cnt = count_ge(mid)
        ge = cnt >= topk
        return (jnp.where(ge, mid, lo), jnp.where(cnt == topk, mid + 1, jnp.where(ge, hi, mid)),
                jnp.where(ge, cnt, cnt_lo))

    def by_value(i, state):
        lo, hi, cnt_lo = state
        mid = to_key(0.5 * to_val(lo) + 0.5 * to_val(hi))
        return probe(jnp.where(jnp.logical_and(mid > lo, mid < hi), mid, lo), lo, hi, cnt_lo)

    state = lax.fori_loop(0, VALUE_PASSES, by_value, (lo0, hi0, cnt0))

    def any_open(lo, hi):
        return jnp.max(jnp.where(jnp.logical_and(enough, hi > lo + 1), 1.0, 0.0))

    def by_key(state):
        lo, hi, cnt_lo, _ = state
        lo, hi, cnt_lo = probe((lo >> 1) + (hi >> 1) + (lo & hi & 1), lo, hi, cnt_lo)
        return lo, hi, cnt_lo, any_open(lo, hi)

    lo, _, cnt_lo, _ = lax.while_loop(lambda s: s[3] > 0.0, by_key, state + (any_open(state[0], state[1]),))
    thr_q = jnp.where(enough, lo, INT_MIN + 1)
    thr = jnp.broadcast_to(thr_q[0:1, :], (LANES, tq)).T

    def count_key(pred):
        def body(c, acc):
            s0 = pl.multiple_of(c * kf, kf)
            ind = jnp.where(pred(key_ref[:, pl.ds(s0, kf)]), 1.0, 0.0)
            for g in range(ncol_f):
                acc = acc + ind[:, g * LANES:(g + 1) * LANES]
            return acc
        acc = lax.fori_loop(0, n_score, body, jnp.zeros((tq, LANES), F32))
        return jnp.sum(acc, axis=1, keepdims=True)

    thr_f = tile_f(thr)
    surplus = jnp.where(enough, cnt_lo - topk, 0.0)

    @pl.when(jnp.max(surplus) > 0.0)
    def _():
        allowed = topk - count_key(lambda k: k > thr_f)
        r = lax.broadcasted_iota(jnp.int32, (kf, kf), 0)
        cidx = lax.broadcasted_iota(jnp.int32, (kf, kf), 1)
        upper = jnp.where(r <= cidx, 1.0, 0.0).astype(BF16)

        def body(c, seen):
            s0 = pl.multiple_of(c * kf, kf)
            key = key_ref[:, pl.ds(s0, kf)]
            tie = key == thr_f
            tie_f = jnp.where(tie, 1.0, 0.0)
            rank = _dot(tie_f.astype(BF16), upper) + seen
            key_ref[:, pl.ds(s0, kf)] = jnp.where(jnp.logical_and(tie, rank > allowed), thr_f - 1, key)
            return seen + jnp.sum(tie_f, axis=1, keepdims=True)
        lax.fori_loop(0, n_score, body, jnp.zeros((tq, 1), F32))

    m_ref[...] = jnp.full(m_ref.shape, M_FLOOR, F32)
    acc_ref[...] = jnp.zeros_like(acc_ref)

    def attend(s0, width, table_idx):
        ncol = width // LANES
        kv = ckv_ref[pl.ds(s0, width), :]
        kv_ext = jnp.concatenate([kv, jnp.ones((width, DSA_LATENT), BF16)], axis=1)
        thr_w = jnp.concatenate([thr] * ncol, axis=1)
        madd = jnp.where(key_ref[:, pl.ds(s0, width)] >= thr_w, 0.0, NEG_BIG)
        for h in range(H):
            rows = slice(h * tq, (h + 1) * tq)
            x = _dot_nt(ql_ref[rows, :], kv) + madd
            if table_idx is not None:
                x = x + tab_ref[table_idx, rows, :]
            m_old = m_ref[rows, :]
            m_new = jnp.maximum(m_old, jnp.max(x, axis=1, keepdims=True))
            alpha = jnp.exp2(m_old - m_new)
            p = jnp.exp2(x - jnp.concatenate([m_new] * ncol, axis=1))
            acc_ref[rows, :] = jnp.concatenate([alpha, alpha], axis=1) * acc_ref[rows, :] + _dot(p.astype(BF16), kv_ext)
            m_ref[rows, :] = m_new

    def far_run(first, count):
        for u in range(count):
            attend(pl.multiple_of((first + u) * kf, kf), kf, None)

    def far_quad(i, carry):
        far_run(4 * i, 4)
        return carry

    def near_run(first, count):
        for u in range(count):
            s0 = pl.multiple_of((first + u) * kn, kn)
            attend(s0, kn, (t0 - s0) // tq)

    def near_pair(i, carry):
        near_run(near_lo + 2 * i, 2)
        return carry

    lax.fori_loop(0, n_far // 4, far_quad, 0)

    @pl.when(n_far % 4 >= 2)
    def _():
        far_run((n_far // 4) * 4, 2)

    @pl.when(n_far % 2 == 1)
    def _():
        far_run(n_far - 1, 1)

    lax.fori_loop(0, (near_hi - near_lo) // 2, near_pair, 0)

    @pl.when((near_hi - near_lo) % 2 == 1)
    def _():
        near_run(near_hi - 1, 1)

    outs = []
    for h in range(H):
        a = acc_ref[h * tq:(h + 1) * tq, :]
        outs.append((a[:, :DSA_LATENT] / a[:, DSA_LATENT:]).astype(BF16))
    out_ref[...] = _dot(jnp.concatenate(outs, axis=1), wuv_ref[...]).astype(BF16)


def _dsa_tiles(seq):
    tq = min(DSA_TQ, seq)
    return tq, min(DSA_KN, seq), min(DSA_KF, seq)


def _dsa(bq, iq, ikw, ckv, tab, wuk_pad, wuv_pad, bsz, seq):
    tq, kn, kf = _dsa_tiles(seq)
    nq = seq // tq
    topk = min(IDX_TOPK_MAX, seq // 4)
    n_off = kf // tq + 1
    qblk = lambda w: pl.BlockSpec((tq, w), lambda b, j: (b * nq + j, 0))
    seqblk = lambda w: pl.BlockSpec((seq, w), lambda b, j: (b, 0))
    return pl.pallas_call(
        functools.partial(_dsa_kernel, tq=tq, kn=kn, kf=kf, topk=topk),
        grid=(bsz, nq),
        in_specs=[qblk(B_Q), qblk(I_Q), qblk(LANES), seqblk(LANES), seqblk(DSA_LATENT),
                  _resident((n_off, DSA_HEADS * tq, kn)), _resident((B_Q, DSA_HEADS * DSA_LATENT)),
                  _resident((DSA_HEADS * DSA_LATENT, B_Q))],
        out_specs=qblk(B_Q),
        out_shape=jax.ShapeDtypeStruct((bsz * seq, B_Q), BF16),
        scratch_shapes=[pltpu.VMEM((tq, seq), jnp.int32),
                        pltpu.VMEM((seq, tq), jnp.int32),
                        pltpu.VMEM((SUBLANES, tq), F32),
                        pltpu.VMEM((SUBLANES, tq), F32),
                        pltpu.VMEM((SUBLANES, tq), F32),
                        pltpu.VMEM((SUBLANES, tq), F32),
                        pltpu.VMEM((DSA_HEADS * tq, 2 * DSA_LATENT), F32),
                        pltpu.VMEM((DSA_HEADS * tq, LANES), F32),
                        pltpu.VMEM((DSA_HEADS * tq, DSA_LATENT), BF16),
                        pltpu.VMEM((IDX_HEADS, tq, LANES), F32)],
        compiler_params=_cparams(("parallel", "arbitrary")),
        name="dsa",
    )(bq, iq, ikw, ikw, ckv, tab, wuk_pad, wuv_pad)


def _merge_kernel(x_ref, ya_ref, yb_ref, ga_ref, gb_ref, wa_ref, wb_ref, wo_ref, g_ref, b_ref, out_ref):
    merged = (jax.nn.sigmoid(ga_ref[...].astype(F32)) * _dot(ya_ref[...], wa_ref[...])
              + jax.nn.sigmoid(gb_ref[...].astype(F32)) * _dot(yb_ref[...], wb_ref[...]))
    mix = _dot(merged.astype(BF16), wo_ref[...])
    out_ref[...] = _layer_norm(DN_ALPHA * x_ref[...] + mix, g_ref[...], b_ref[...])


def _merge(x2, ya, yb, ga, gb, wa, wb, wo, g, b):
    n = x2.shape[0]
    tm = min(TOK_TILE, n)
    tok = lambda w: pl.BlockSpec((tm, w), lambda i: (i, 0))
    full = _resident
    return pl.pallas_call(
        _merge_kernel,
        grid=(n // tm,),
        in_specs=[tok(D_MODEL), tok(A_V), tok(B_Q), tok(D_MODEL), tok(D_MODEL),
                  full((A_V, D_MODEL)), full((B_Q, D_MODEL)), full((D_MODEL, D_MODEL)),
                  full((1, D_MODEL)), full((1, D_MODEL))],
        out_specs=tok(D_MODEL),
        out_shape=jax.ShapeDtypeStruct((n, D_MODEL), F32),
        compiler_params=_cparams(("parallel",)),
        name="merge_ln",
    )(x2, ya, yb, ga, gb, wa, wb, wo, g.reshape(1, D_MODEL), b.reshape(1, D_MODEL))


def _dense_ffn_kernel(x_ref, wg_ref, wu_ref, wd_ref, g_ref, b_ref, out_ref):
    x = x_ref[...]
    xb = x.astype(BF16)
    f = jnp.zeros(x.shape, F32)
    for c in range(D_FF // FF_CHUNK):
        cols = slice(c * FF_CHUNK, (c + 1) * FF_CHUNK)
        hdn = _silu(_dot(xb, wg_ref[:, cols])) * _dot(xb, wu_ref[:, cols])
        f = f + _dot(hdn.astype(BF16), wd_ref[cols, :])
    out_ref[...] = _layer_norm(DN_ALPHA * x + f, g_ref[...], b_ref[...])


def _dense_ffn(x2, wg, wu, wd, g, b):
    n = x2.shape[0]
    tm = min(TOK_TILE, n)
    tok = pl.BlockSpec((tm, D_MODEL), lambda i: (i, 0))
    full = _resident
    return pl.pallas_call(
        _dense_ffn_kernel,
        grid=(n // tm,),
        in_specs=[tok, full((D_MODEL, D_FF)), full((D_MODEL, D_FF)), full((D_FF, D_MODEL)),
                  full((1, D_MODEL)), full((1, D_MODEL))],
        out_specs=tok,
        out_shape=jax.ShapeDtypeStruct((n, D_MODEL), F32),
        compiler_params=_cparams(("parallel",)),
        name="dense_ffn_ln",
    )(x2, wg, wu, wd, g.reshape(1, D_MODEL), b.reshape(1, D_MODEL))


def _router_kernel(x_ref, rw_ref, e_ref, g_ref):
    logits = lax.dot_general(rw_ref[...], x_ref[...], NT_DIMS, preferred_element_type=F32,
                             precision=lax.Precision.HIGHEST)
    eidx = lax.broadcasted_iota(jnp.int32, logits.shape, 0)
    m1 = jnp.max(logits, axis=0, keepdims=True)
    i1 = jnp.min(jnp.where(logits == m1, eidx, N_EXPERTS), axis=0, keepdims=True)
    rest = jnp.where(eidx == i1, -jnp.inf, logits)
    m2 = jnp.max(rest, axis=0, keepdims=True)
    i2 = jnp.min(jnp.where(rest == m2, eidx, N_EXPERTS), axis=0, keepdims=True)
    e2 = jnp.exp(m2 - m1)
    den = 1.0 + e2
    e_ref[0:1, :] = i1
    e_ref[1:2, :] = i2
    g_ref[0:1, :] = 1.0 / den
    g_ref[1:2, :] = e2 / den


def _router(x2, rw_t):
    n = x2.shape[0]
    tm = min(TOK_TILE, n)
    return pl.pallas_call(
        _router_kernel,
        grid=(n // tm,),
        in_specs=[pl.BlockSpec((tm, D_MODEL), lambda i: (i, 0)), pl.BlockSpec((N_EXPERTS, D_MODEL), lambda i: (0, 0))],
        out_specs=[pl.BlockSpec((TOP_K, tm), lambda i: (0, i)), pl.BlockSpec((TOP_K, tm), lambda i: (0, i))],
        out_shape=[jax.ShapeDtypeStruct((TOP_K, n), jnp.int32), jax.ShapeDtypeStruct((TOP_K, n), F32)],
        compiler_params=_cparams(("parallel",)),
        name="router",
    )(x2, rw_t)


def _row_copy(src, src_row, dst, dst_row, sem):
    return pltpu.make_async_copy(src.at[pl.ds(src_row, 1)], dst.at[pl.ds(dst_row, 1)], sem)


def _dispatch_kernel(d0_ref, d1_ref, x_ref, xs_init_hbm, xs_hbm, sem, *, tile):
    del xs_init_hbm

    def start(r, c):
        _row_copy(x_ref, r, xs_hbm, d0_ref[0, r], sem).start(priority=0)
        _row_copy(x_ref, r, xs_hbm, d1_ref[0, r], sem).start(priority=1)
        return c
    lax.fori_loop(0, tile, start, 0, unroll=DMA_UNROLL)

    def wait(r, c):
        _row_copy(x_ref, r, xs_hbm, 0, sem).wait()
        _row_copy(x_ref, r, xs_hbm, 0, sem).wait()
        return c
    lax.fori_loop(0, tile, wait, 0, unroll=DMA_UNROLL)


def _dispatch(x2, pos, total):
    n = x2.shape[0]
    tile = min(COMB_TILE, n)
    nb = n // tile
    smem = pl.BlockSpec((None, 1, tile), lambda i: (i, 0, 0), memory_space=pltpu.SMEM)
    return pl.pallas_call(
        functools.partial(_dispatch_kernel, tile=tile),
        grid=(nb,),
        in_specs=[smem, smem, pl.BlockSpec((tile, D_MODEL), lambda i: (i, 0)), pl.BlockSpec(memory_space=pl.ANY)],
        out_specs=pl.BlockSpec(memory_space=pl.ANY),
        out_shape=jax.ShapeDtypeStruct((total, D_MODEL), F32),
        scratch_shapes=[pltpu.SemaphoreType.DMA(())],
        input_output_aliases={3: 0},
        compiler_params=_cparams(("arbitrary",)),
        name="moe_dispatch",
    )(pos[:, 0].reshape(nb, 1, tile), pos[:, 1].reshape(nb, 1, tile), x2, jnp.zeros((total, D_MODEL), F32))


def _expert_kernel(blk_e_ref, nused_ref, xs_ref, wg_ref, wu_ref, wd_ref, y_ref, xb, acc):
    i = pl.program_id(0)
    f = pl.program_id(1)
    used = i < nused_ref[0]
    last = f == pl.num_programs(1) - 1

    @pl.when(jnp.logical_and(used, f == 0))
    def _():
        xb[...] = xs_ref[...].astype(BF16)
        acc[...] = jnp.zeros_like(acc)

    @pl.when(used)
    def _():
        x = xb[...]
        hdn = _silu(_dot(x, wg_ref[...])) * _dot(x, wu_ref[...])
        acc[...] += _dot(hdn.astype(BF16), wd_ref[...])

    @pl.when(jnp.logical_and(used, last))
    def _():
        y_ref[...] = acc[...]

    @pl.when(jnp.logical_and(jnp.logical_not(used), last))
    def _():
        y_ref[...] = jnp.zeros_like(y_ref)


def _experts(xs, blk_e, nused, wg, wu, wd):
    total = xs.shape[0]
    tile = MOE_TILE
    nblk = total // tile
    nf = D_FF_EXPERT // MOE_FT
    grid_spec = pltpu.PrefetchScalarGridSpec(
        num_scalar_prefetch=2,
        grid=(nblk, nf),
        in_specs=[pl.BlockSpec((tile, D_MODEL), lambda i, f, be, nu: (i, 0)),
                  pl.BlockSpec((None, D_MODEL, MOE_FT), lambda i, f, be, nu: (be[i], 0, f)),
                  pl.BlockSpec((None, D_MODEL, MOE_FT), lambda i, f, be, nu: (be[i], 0, f)),
                  pl.BlockSpec((None, MOE_FT, D_MODEL), lambda i, f, be, nu: (be[i], f, 0))],
        out_specs=pl.BlockSpec((tile, D_MODEL), lambda i, f, be, nu: (i, 0)),
        scratch_shapes=[pltpu.VMEM((tile, D_MODEL), BF16), pltpu.VMEM((tile, D_MODEL), F32)],
    )
    return pl.pallas_call(
        _expert_kernel,
        grid_spec=grid_spec,
        out_shape=jax.ShapeDtypeStruct((total, D_MODEL), F32),
        compiler_params=_cparams(("arbitrary", "arbitrary")),
        name="experts",
    )(blk_e, nused, xs, wg, wu, wd)


def _combine_kernel(p0_ref, p1_ref, y_hbm, x_ref, gate_ref, g_ref, b_ref, out_ref, buf0, buf1, sem, *, tile):
    def start(r, c):
        _row_copy(y_hbm, p0_ref[0, r], buf0, r, sem).start(priority=0)
        _row_copy(y_hbm, p1_ref[0, r], buf1, r, sem).start(priority=1)
        return c
    lax.fori_loop(0, tile, start, 0, unroll=DMA_UNROLL)

    def wait(r, c):
        _row_copy(y_hbm, 0, buf0, r, sem).wait()
        _row_copy(y_hbm, 0, buf1, r, sem).wait()
        return c
    lax.fori_loop(0, tile, wait, 0, unroll=DMA_UNROLL)
    f = buf0[...] * gate_ref[:, 0:1] + buf1[...] * gate_ref[:, 1:2]
    out_ref[...] = _layer_norm(DN_ALPHA * x_ref[...] + f, g_ref[...], b_ref[...])


def _combine(y, pos, gates, x2, g, b):
    n = x2.shape[0]
    tile = min(COMB_TILE, n)
    nb = n // tile
    smem = pl.BlockSpec((None, 1, tile), lambda i: (i, 0, 0), memory_space=pltpu.SMEM)
    full = pl.BlockSpec((1, D_MODEL), lambda i: (0, 0))
    return pl.pallas_call(
        functools.partial(_combine_kernel, tile=tile),
        grid=(nb,),
        in_specs=[smem, smem, pl.BlockSpec(memory_space=pl.ANY),
                  pl.BlockSpec((tile, D_MODEL), lambda i: (i, 0)),
                  pl.BlockSpec((tile, TOP_K), lambda i: (i, 0)), full, full],
        out_specs=pl.BlockSpec((tile, D_MODEL), lambda i: (i, 0)),
        out_shape=jax.ShapeDtypeStruct((n, D_MODEL), F32),
        scratch_shapes=[pltpu.VMEM((tile, D_MODEL), F32), pltpu.VMEM((tile, D_MODEL), F32),
                        pltpu.SemaphoreType.DMA(())],
        compiler_params=_cparams(("arbitrary",)),
        name="moe_combine_ln",
    )(pos[:, 0].reshape(nb, 1, tile), pos[:, 1].reshape(nb, 1, tile), y, x2, gates,
      g.reshape(1, D_MODEL), b.reshape(1, D_MODEL))


def _moe(x2, router_w, wg, wu, wd, g, b):
    n = x2.shape[0]
    top_e, gates = _router(x2, router_w.T)
    flat_e = top_e.T.reshape(-1)
    onehot = (flat_e[:, None] == jnp.arange(N_EXPERTS, dtype=jnp.int32)[None, :]).astype(jnp.int32)
    rank = jnp.sum((jnp.cumsum(onehot, axis=0) - onehot) * onehot, axis=1)
    counts = jnp.sum(onehot, axis=0)
    padded = ((counts + MOE_TILE - 1) // MOE_TILE) * MOE_TILE
    pad_end = jnp.cumsum(padded)
    pad_start = pad_end - padded
    pos = (pad_start[flat_e] + rank).astype(jnp.int32).reshape(n, TOP_K)
    total = n * TOP_K + N_EXPERTS * MOE_TILE
    nblk = total // MOE_TILE
    blk_e = jnp.minimum(jnp.searchsorted(pad_end, jnp.arange(nblk, dtype=jnp.int32) * MOE_TILE, side='right'),
                        N_EXPERTS - 1).astype(jnp.int32)
    nused = (pad_end[-1:] // MOE_TILE).astype(jnp.int32)
    xs = _dispatch(x2, pos, total)
    y = _experts(xs, blk_e, nused, wg, wu, wd)
    return _combine(y, pos, gates.T, x2, g, b)


def _pad_head_proj(w_uk, w_uv):
    H, DH, DC = DSA_HEADS, DSA_HEAD_DIM, DSA_LATENT
    eye = jnp.eye(H, dtype=w_uk.dtype)
    wuk_bd = jnp.einsum('hg,hdc->hdgc', eye, w_uk).reshape(H * DH, H * DC)
    wuv_bd = jnp.einsum('hg,hcd->hcgd', eye, w_uv).reshape(H * DC, H * DH)
    return wuk_bd.astype(BF16), wuv_bd.astype(BF16)


def kernel(x, w_in, mlstm_conv_w, mlstm_gate_bias, mlstm_norm_g, dsa_kv_norm_g, dsa_w_uk, dsa_w_uv, rel_bias,
           w_branch_a, w_branch_b, w_out, ln_g, ln_b, dense_w_gate, dense_w_up, dense_w_down, router_w,
           expert_w_gate, expert_w_up, expert_w_down):
    bsz, seq, d = x.shape
    n = bsz * seq
    x2 = x.reshape(n, d)
    tab = _near_bias_tables(rel_bias, *_dsa_tiles(seq))
    for l in range(DEPTH):
        p = dict(zip([nm for nm, _, _ in IN_PROJ_OUTS],
                     _in_proj(x2, _regroup_in_proj(w_in[l]), dsa_kv_norm_g[l])))
        ya = _mlstm(p["qk"], p["av"], p["aif"], p["ao"], mlstm_conv_w[l], mlstm_gate_bias[l], mlstm_norm_g[l],
                    bsz, seq)
        wuk_pad, wuv_pad = _pad_head_proj(dsa_w_uk[l], dsa_w_uv[l])
        yb = _dsa(p["bq"], p["iq"], p["ikw"], p["bc"], tab, wuk_pad, wuv_pad, bsz, seq)
        x2 = _merge(x2, ya, yb, p["ga"], p["gb"], w_branch_a[l].astype(BF16), w_branch_b[l].astype(BF16),
                    w_out[l].astype(BF16), ln_g[l, 0], ln_b[l, 0])
        j = l // 2
        if l % 2 == 0:
            x2 = _dense_ffn(x2, dense_w_gate[j].astype(BF16), dense_w_up[j].astype(BF16),
                            dense_w_down[j].astype(BF16), ln_g[l, 1], ln_b[l, 1])
        else:
            x2 = _moe(x2, router_w[j], expert_w_gate[j].astype(BF16), expert_w_up[j].astype(BF16),
                      expert_w_down[j].astype(BF16), ln_g[l, 1], ln_b[l, 1])
    return x2.reshape(bsz, seq, d)
```
